```python
import math
import jax
import jax.numpy as jnp
from jax import lax
import numpy as np


D_MODEL = 2048
BATCH = 4
SEQ = 2048
DEPTH = 1

RET_QK_DIM = 128
RET_V_DIM = 256
RET_HEADS = D_MODEL // RET_V_DIM
RET_QK_WIDTH = RET_HEADS * RET_QK_DIM
RET_WIDTH = RET_HEADS * RET_V_DIM
CHUNK = 128

DIFF_HEAD_DIM = 128
DIFF_V_DIM = 2 * DIFF_HEAD_DIM
DIFF_HEADS = D_MODEL // DIFF_V_DIM
DIFF_QK_WIDTH = DIFF_HEADS * 2 * DIFF_HEAD_DIM
DIFF_WIDTH = DIFF_HEADS * DIFF_V_DIM
Q_BLOCK = 128
ROPE_THETA = 10000.0

NORM_EPS = 1e-6
SUBLN_EPS = 1e-5

SPLIT_SIZES = (RET_QK_WIDTH, RET_QK_WIDTH, RET_WIDTH, RET_WIDTH,
               DIFF_QK_WIDTH, DIFF_QK_WIDTH, DIFF_WIDTH, DIFF_WIDTH,
               D_MODEL, D_MODEL)
SPLIT_POINTS = tuple(int(v) for v in np.cumsum(SPLIT_SIZES)[:-1])
IN_WIDTH = int(sum(SPLIT_SIZES))

kernel_name = 'hybrid_retention_diffattn_gated_block'


def rmsnorm(x, w, eps):
    xf = x.astype(jnp.float32)
    y = xf * lax.rsqrt(jnp.mean(xf * xf, axis=-1, keepdims=True) + eps)
    if w is not None:
        y = y * w.astype(jnp.float32)
    return y.astype(x.dtype)


def rotary_tables(pos, inv_freq, dtype):
    ang = pos[:, None] * inv_freq[None, :]
    emb = jnp.concatenate([ang, ang], axis=-1)
    return jnp.cos(emb).astype(dtype)[:, None, :], jnp.sin(emb).astype(dtype)[:, None, :]


def rotate_half(x):
    x1, x2 = jnp.split(x, 2, axis=-1)
    return jnp.concatenate([-x2, x1], axis=-1)


def apply_rotary(x, cos, sin):
    return x * cos + rotate_half(x) * sin


def retention(q, k, v):
    b, s, h, dk = q.shape
    dv = v.shape[-1]
    n = s // CHUNK
    log_gamma = jnp.log1p(-jnp.exp2(-5.0 - jnp.arange(h, dtype=jnp.float32)))
    qc = q.reshape(b, n, CHUNK, h, dk)
    kc = k.reshape(b, n, CHUNK, h, dk)
    vc = v.reshape(b, n, CHUNK, h, dv)
    idx = jnp.arange(CHUNK, dtype=jnp.float32)
    rel = idx[:, None] - idx[None, :]
    decay_mask = jnp.where(rel >= 0.0,
                           jnp.exp(log_gamma[:, None, None] * jnp.maximum(rel, 0.0)[None]),
                           0.0)
    scores = jnp.einsum('bnqhd,bnkhd->bnhqk', qc, kc) * decay_mask
    o_intra = jnp.einsum('bnhqk,bnkhe->bnqhe', scores, vc)
    zeta = jnp.exp(log_gamma[None, :] * (CHUNK - 1.0 - idx)[:, None])
    kv = jnp.einsum('bnkhd,bnkhe->bnhde', kc, vc * zeta[:, :, None])
    chunk_decay = jnp.exp(log_gamma * CHUNK)[:, None, None]

    def step(state, kv_i):
        return state * chunk_decay + kv_i, state

    init = jnp.zeros((b, h, dk, dv), kv.dtype)
    _, prev = lax.scan(step, init, jnp.moveaxis(kv, 1, 0))
    prev = jnp.moveaxis(prev, 0, 1)
    xi = jnp.exp(log_gamma[None, :] * (idx + 1.0)[:, None])
    o_cross = jnp.einsum('bnqhd,bnhde->bnqhe', qc, prev) * xi[:, :, None]
    return (o_intra + o_cross).reshape(b, s, h, dv)


def diff_attention(q, k, v, lam):
    b, s, h, _, d = q.shape
    n = s // Q_BLOCK
    qb = (q * (d ** -0.5)).reshape(b, n, Q_BLOCK, h, 2, d)
    qb = jnp.moveaxis(qb, 1, 0)
    key_pos = jnp.arange(s)

    def block(args):
        q_blk, i = args
        q_pos = i * Q_BLOCK + jnp.arange(Q_BLOCK)
        scores = jnp.einsum('bqhcd,bkhcd->bhcqk', q_blk, k).astype(jnp.float32)
        causal = key_pos[None, :] <= q_pos[:, None]
        scores = jnp.where(causal, scores, -jnp.inf)
        probs = jax.nn.softmax(scores, axis=-1)
        attn = probs[:, :, 0] - lam * probs[:, :, 1]
        return jnp.einsum('bhqk,bkhe->bqhe', attn.astype(v.dtype), v)

    out = lax.map(block, (qb, jnp.arange(n)))
    return jnp.moveaxis(out, 0, 1).reshape(b, s, h, 2 * d)


def setup_inputs(seed: int = 0) -> dict:
    key = jax.random.key(seed)
    ks = jax.random.split(key, 12)
    f32 = jnp.float32
    x = jax.random.normal(ks[0], (BATCH, SEQ, D_MODEL), f32)
    norm_w = 1.0 + 0.02 * jax.random.normal(ks[1], (DEPTH, D_MODEL), f32)
    w_in = jax.random.normal(ks[2], (DEPTH, D_MODEL, IN_WIDTH), f32) * D_MODEL ** -0.5
    w_ret_up = jax.random.normal(ks[3], (DEPTH, RET_WIDTH, D_MODEL), f32) * RET_WIDTH ** -0.5
    w_diff_up = jax.random.normal(ks[4], (DEPTH, DIFF_WIDTH, D_MODEL), f32) * DIFF_WIDTH ** -0.5
    w_out = jax.random.normal(ks[5], (DEPTH, D_MODEL, D_MODEL), f32) * D_MODEL ** -0.5
    lambda_q1 = 0.1 * jax.random.normal(ks[6], (DEPTH, DIFF_HEAD_DIM), f32)
    lambda_k1 = 0.1 * jax.random.normal(ks[7], (DEPTH, DIFF_HEAD_DIM), f32)
    lambda_q2 = 0.1 * jax.random.normal(ks[8], (DEPTH, DIFF_HEAD_DIM), f32)
    lambda_k2 = 0.1 * jax.random.normal(ks[9], (DEPTH, DIFF_HEAD_DIM), f32)
    subln_w = 1.0 + 0.02 * jax.random.normal(ks[10], (DEPTH, DIFF_V_DIM), f32)
    final_norm_w = 1.0 + 0.02 * jax.random.normal(ks[11], (D_MODEL,), f32)
    return {'x': x, 'norm_w': norm_w, 'w_in': w_in, 'w_ret_up': w_ret_up,
            'w_diff_up': w_diff_up, 'w_out': w_out, 'lambda_q1': lambda_q1,
            'lambda_k1': lambda_k1, 'lambda_q2': lambda_q2, 'lambda_k2': lambda_k2,
            'subln_w': subln_w, 'final_norm_w': final_norm_w}


def reference(x, norm_w, w_in, w_ret_up, w_diff_up, w_out, lambda_q1, lambda_k1,
              lambda_q2, lambda_k2, subln_w, final_norm_w):
    b, s, _ = x.shape
    pos = jnp.arange(s, dtype=jnp.float32)
    ret_inv_freq = jnp.exp(-math.log(10000.0) * jnp.linspace(0.0, 1.0, RET_QK_DIM // 2, dtype=jnp.float32))
    rope_inv_freq = ROPE_THETA ** (-jnp.arange(0, DIFF_HEAD_DIM, 2, dtype=jnp.float32) / DIFF_HEAD_DIM)
    ret_cos, ret_sin = rotary_tables(pos, ret_inv_freq, x.dtype)
    rope_cos, rope_sin = rotary_tables(pos, rope_inv_freq, x.dtype)

    h = x
    for l in range(DEPTH):
        xn = rmsnorm(h, norm_w[l], NORM_EPS)
        proj = xn @ w_in[l]
        (rq, rk, rv, rz, dq, dk, dv, dz, g_ret, g_diff) = jnp.split(proj, SPLIT_POINTS, axis=-1)

        rq = apply_rotary(rq.reshape(b, s, RET_HEADS, RET_QK_DIM), ret_cos, ret_sin)
        rk = apply_rotary(rk.reshape(b, s, RET_HEADS, RET_QK_DIM) * (RET_QK_DIM ** -0.5), ret_cos, ret_sin)
        rv = rv.reshape(b, s, RET_HEADS, RET_V_DIM)
        o_ret = retention(rq, rk, rv).astype(h.dtype)
        o_ret = rmsnorm(o_ret, None, NORM_EPS).reshape(b, s, RET_WIDTH)
        o_ret = o_ret * jax.nn.silu(rz)

        dq = apply_rotary(dq.reshape(b, s, 2 * DIFF_HEADS, DIFF_HEAD_DIM), rope_cos, rope_sin)
        dk = apply_rotary(dk.reshape(b, s, 2 * DIFF_HEADS, DIFF_HEAD_DIM), rope_cos, rope_sin)
        dq = dq.reshape(b, s, DIFF_HEADS, 2, DIFF_HEAD_DIM)
        dk = dk.reshape(b, s, DIFF_HEADS, 2, DIFF_HEAD_DIM)
        dv = dv.reshape(b, s, DIFF_HEADS, DIFF_V_DIM)
        lam_init = 0.8 - 0.6 * math.exp(-0.3 * l)
        lam = (jnp.exp(jnp.sum(lambda_q1[l].astype(jnp.float32) * lambda_k1[l].astype(jnp.float32)))
               - jnp.exp(jnp.sum(lambda_q2[l].astype(jnp.float32) * lambda_k2[l].astype(jnp.float32)))
               + lam_init)
        o_diff = diff_attention(dq, dk, dv, lam).astype(h.dtype)
        o_diff = rmsnorm(o_diff, subln_w[l], SUBLN_EPS) * (1.0 - lam_init)
        o_diff = o_diff.reshape(b, s, DIFF_WIDTH) * jax.nn.silu(dz)

        mixed = (jax.nn.sigmoid(g_ret) * (o_ret @ w_ret_up[l])
                 + jax.nn.sigmoid(g_diff) * (o_diff @ w_diff_up[l]))
        h = h + mixed @ w_out[l]

    return rmsnorm(h, final_norm_w, NORM_EPS)
```

```python
import functools
import math

import jax
import jax.numpy as jnp
from jax import lax
from jax.experimental import pallas as pl
from jax.experimental.pallas import tpu as pltpu

F32 = jnp.float32
BF16 = jnp.bfloat16

D_MODEL = 2048
RET_QK_DIM = 128
RET_V_DIM = 256
RET_HEADS = D_MODEL // RET_V_DIM
DIFF_HEAD_DIM = 128
DIFF_V_DIM = 2 * DIFF_HEAD_DIM
DIFF_HEADS = D_MODEL // DIFF_V_DIM
ROPE_THETA = 10000.0
NORM_EPS = 1e-6
SUBLN_EPS = 1e-5
LAMBDA_INIT = 0.8 - 0.6 * math.exp(-0.3 * 0)

RET_QK_WIDTH = RET_HEADS * RET_QK_DIM
RET_WIDTH = RET_HEADS * RET_V_DIM
DIFF_QK_WIDTH = DIFF_HEADS * 2 * DIFF_HEAD_DIM
DIFF_WIDTH = DIFF_HEADS * DIFF_V_DIM
IN_WIDTH = 2 * RET_QK_WIDTH + 2 * RET_WIDTH + 2 * DIFF_QK_WIDTH + 2 * DIFF_WIDTH + 2 * D_MODEL

LANES = 128
SLAB_RQ = 0
SLAB_RK = SLAB_RQ + RET_QK_WIDTH // LANES
SLAB_RV = SLAB_RK + RET_QK_WIDTH // LANES
SLAB_RZ = SLAB_RV + RET_WIDTH // LANES
SLAB_DQ = SLAB_RZ + RET_WIDTH // LANES
SLAB_DK = SLAB_DQ + DIFF_QK_WIDTH // LANES
SLAB_DV = SLAB_DK + DIFF_QK_WIDTH // LANES
SLAB_DZ = SLAB_DV + DIFF_WIDTH // LANES
SLAB_GR = SLAB_DZ + DIFF_WIDTH // LANES
SLAB_GD = SLAB_GR + D_MODEL // LANES
N_SLABS = IN_WIDTH // LANES

VMEM_LIMIT = 56 * 1024 * 1024

RET_CHUNK = 256
ATT_BLOCK = 256


def _params(n_axes, vmem=VMEM_LIMIT):
    return pltpu.CompilerParams(dimension_semantics=("arbitrary",) * n_axes,
                                vmem_limit_bytes=vmem)


def _rmsnorm_kernel(x_ref, w_ref, o_ref):
    x = x_ref[...]
    ms = jnp.mean(x * x, axis=-1, keepdims=True)
    o_ref[...] = (x * lax.rsqrt(ms + NORM_EPS) * w_ref[...]).astype(o_ref.dtype)


def _input_rmsnorm(x2, w, tm=512):
    m, d = x2.shape
    return pl.pallas_call(
        _rmsnorm_kernel,
        out_shape=jax.ShapeDtypeStruct((m, d), BF16),
        grid=(m // tm,),
        in_specs=[pl.BlockSpec((tm, d), lambda i: (i, 0)),
                  pl.BlockSpec((1, d), lambda i: (0, 0))],
        out_specs=pl.BlockSpec((tm, d), lambda i: (i, 0)),
        compiler_params=_params(1),
        name="input_rmsnorm",
    )(x2, w)


def _cast_weight(w_ref, wbf_ref, rows=128):
    def body(i, c):
        r = pl.multiple_of(i * rows, rows)
        wbf_ref[pl.ds(r, rows), :] = w_ref[pl.ds(r, rows), :].astype(BF16)
        return c
    lax.fori_loop(0, w_ref.shape[0] // rows, body, 0)


def _inproj_kernel(x_ref, w_ref, o_ref, wbf_ref):
    @pl.when(pl.program_id(1) == 0)
    def _():
        _cast_weight(w_ref, wbf_ref)

    acc = jnp.dot(x_ref[...], wbf_ref[...], preferred_element_type=F32)
    for j in range(o_ref.shape[0]):
        o_ref[j] = acc[:, j * LANES:(j + 1) * LANES].astype(o_ref.dtype)


def _input_projection(xn, w_in, tm=1024, tn=1024):
    m, d = xn.shape
    n = w_in.shape[1]
    return pl.pallas_call(
        _inproj_kernel,
        out_shape=jax.ShapeDtypeStruct((n // LANES, m, LANES), BF16),
        grid=(n // tn, m // tm),
        in_specs=[pl.BlockSpec((tm, d), lambda j, i: (i, 0)),
                  pl.BlockSpec((d, tn), lambda j, i: (0, j))],
        out_specs=pl.BlockSpec((tn // LANES, tm, LANES), lambda j, i: (j, i, 0)),
        scratch_shapes=[pltpu.VMEM((d, tn), BF16)],
        compiler_params=_params(2),
        name="input_projection",
    )(xn, w_in)


def _rotary(x, cos, sin_signed):
    return x * cos + pltpu.roll(x, x.shape[-1] // 2, axis=x.ndim - 1) * sin_signed


def _silu(z):
    return z * jax.nn.sigmoid(z)


def _retention_kernel(lg_ref, q_ref, k_ref, v_ref, z_ref, cos_ref, sin_ref, o_ref,
                      qr_ref, kr_ref, state_ref):
    s_len = q_ref.shape[0]
    c = RET_CHUNK
    lg = lg_ref[pl.program_id(1)]

    cos = cos_ref[...]
    sin = sin_ref[...]
    qr_ref[...] = _rotary(q_ref[...].astype(F32), cos, sin).astype(BF16)
    kr_ref[...] = _rotary(k_ref[...].astype(F32) * (RET_QK_DIM ** -0.5), cos, sin).astype(BF16)

    row = lax.broadcasted_iota(jnp.int32, (c, c), 0).astype(F32)
    col = lax.broadcasted_iota(jnp.int32, (c, c), 1).astype(F32)
    rel = row - col
    decay_mask = jnp.where(rel >= 0.0, jnp.exp(lg * jnp.maximum(rel, 0.0)), 0.0)
    rowv = lax.broadcasted_iota(jnp.int32, (c, RET_V_DIM), 0).astype(F32)
    zeta = jnp.exp(lg * (c - 1.0 - rowv))
    xi = jnp.exp(lg * (rowv + 1.0))
    chunk_decay = jnp.exp(jnp.zeros((1, RET_V_DIM), F32) + lg * c)

    state_ref[...] = jnp.zeros_like(state_ref)

    def chunk(n, carry):
        r = pl.multiple_of(n * c, c)
        qc = qr_ref[pl.ds(r, c), :]
        kc = kr_ref[pl.ds(r, c), :]
        vc = jnp.concatenate([v_ref[0, pl.ds(r, c), :], v_ref[1, pl.ds(r, c), :]], axis=-1)
        zc = jnp.concatenate([z_ref[0, pl.ds(r, c), :], z_ref[1, pl.ds(r, c), :]], axis=-1)

        scores = lax.dot_general(qc, kc, (((1,), (1,)), ((), ())),
                                 preferred_element_type=F32) * decay_mask
        o = jnp.dot(scores.astype(BF16), vc, preferred_element_type=F32)
        prev = state_ref[...]
        o = o + jnp.dot(qc, prev.astype(BF16), preferred_element_type=F32) * xi
        kv = lax.dot_general(kc, (vc.astype(F32) * zeta).astype(BF16),
                             (((0,), (0,)), ((), ())), preferred_element_type=F32)
        state_ref[...] = prev * chunk_decay + kv

        ms = jnp.mean(o * o, axis=-1, keepdims=True)
        o = o * lax.rsqrt(ms + NORM_EPS)
        o_ref[pl.ds(r, c), :] = (o * _silu(zc.astype(F32))).astype(o_ref.dtype)
        return carry

    lax.fori_loop(0, s_len // c, chunk, 0)


def _retention(proj4, log_gamma, cos, sin_signed):
    _, b, s, _ = proj4.shape
    qk_spec = lambda base: pl.BlockSpec((None, None, s, LANES),
                                        lambda bi, h, base=base: (base + h, bi, 0, 0))
    pair_spec = lambda base: pl.BlockSpec((2, None, s, LANES),
                                          lambda bi, h, base=base: (base // 2 + h, bi, 0, 0))
    table_spec = pl.BlockSpec((s, LANES), lambda bi, h: (0, 0))
    return pl.pallas_call(
        _retention_kernel,
        out_shape=jax.ShapeDtypeStruct((RET_HEADS, b, s, RET_V_DIM), BF16),
        grid=(b, RET_HEADS),
        in_specs=[pl.BlockSpec(memory_space=pltpu.SMEM),
                  qk_spec(SLAB_RQ), qk_spec(SLAB_RK), pair_spec(SLAB_RV), pair_spec(SLAB_RZ),
                  table_spec, table_spec],
        out_specs=pl.BlockSpec((None, None, s, RET_V_DIM), lambda bi, h: (h, bi, 0, 0)),
        scratch_shapes=[pltpu.VMEM((s, RET_QK_DIM), BF16),
                        pltpu.VMEM((s, RET_QK_DIM), BF16),
                        pltpu.VMEM((RET_QK_DIM, RET_V_DIM), F32)],
        compiler_params=_params(2),
        name="retention",
    )(log_gamma, proj4, proj4, proj4, proj4, cos, sin_signed)


def _diff_attn_kernel(q_ref, k_ref, v_ref, z_ref, cos_ref, sin_ref,
                      lq1_ref, lk1_ref, lq2_ref, lk2_ref, subln_ref, o_ref,
                      kr_ref, vc_ref):
    t = ATT_BLOCK
    qi = pl.program_id(2)

    @pl.when(qi == 0)
    def _():
        cos = cos_ref[...]
        sin = sin_ref[...]
        for c in range(2):
            kr_ref[c] = _rotary(k_ref[c].astype(F32), cos, sin).astype(BF16)
        vc_ref[...] = jnp.concatenate([v_ref[0], v_ref[1]], axis=-1)

    r0 = pl.multiple_of(qi * t, t)
    cos_q = cos_ref[pl.ds(r0, t), :]
    sin_q = sin_ref[pl.ds(r0, t), :]
    qs = [_rotary(q_ref[c].astype(F32) * (DIFF_HEAD_DIM ** -0.5), cos_q, sin_q).astype(BF16)
          for c in range(2)]

    def scores_of(c, r):
        return lax.dot_general(qs[c], kr_ref[c, pl.ds(r, t), :], (((1,), (1,)), ((), ())),
                               preferred_element_type=F32)

    def update(c, s, vblk, m, l, acc):
        m_new = jnp.maximum(m, jnp.max(s, axis=-1, keepdims=True))
        alpha = jnp.exp(m - m_new)
        p = jnp.exp(s - m_new)
        l = alpha * l + jnp.sum(p, axis=-1, keepdims=True)
        acc = alpha * acc + jnp.dot(p.astype(BF16), vblk, preferred_element_type=F32)
        return m_new, l, acc

    def kv_step(j, carry):
        r = pl.multiple_of(j * t, t)
        vblk = vc_ref[pl.ds(r, t), :]
        out = []
        for c in range(2):
            m, l, acc = carry[3 * c:3 * c + 3]
            out.extend(update(c, scores_of(c, r), vblk, m, l, acc))
        return tuple(out)

    init = []
    for c in range(2):
        init.extend([jnp.full((t, 1), -jnp.inf, F32), jnp.zeros((t, 1), F32),
                     jnp.zeros((t, DIFF_V_DIM), F32)])
    carry = lax.fori_loop(0, qi, kv_step, tuple(init))

    causal = (lax.broadcasted_iota(jnp.int32, (t, t), 1)
              <= lax.broadcasted_iota(jnp.int32, (t, t), 0))
    vblk = vc_ref[pl.ds(r0, t), :]
    res = []
    for c in range(2):
        m, l, acc = carry[3 * c:3 * c + 3]
        s = jnp.where(causal, scores_of(c, r0), -jnp.inf)
        m, l, acc = update(c, s, vblk, m, l, acc)
        res.append(acc / l)

    lam = (jnp.exp(jnp.sum(lq1_ref[...] * lk1_ref[...], axis=-1, keepdims=True))
           - jnp.exp(jnp.sum(lq2_ref[...] * lk2_ref[...], axis=-1, keepdims=True))
           + LAMBDA_INIT)
    o = res[0] - lam * res[1]
    ms = jnp.mean(o * o, axis=-1, keepdims=True)
    o = o * lax.rsqrt(ms + SUBLN_EPS) * subln_ref[...] * (1.0 - LAMBDA_INIT)
    z = jnp.concatenate([z_ref[0], z_ref[1]], axis=-1).astype(F32)
    o_ref[...] = (o * _silu(z)).astype(o_ref.dtype)


def _diff_attention(proj4, cos, sin_signed, lq1, lk1, lq2, lk2, subln_w):
    _, b, s, _ = proj4.shape
    t = ATT_BLOCK
    blk = lambda base: pl.BlockSpec((2, None, t, LANES),
                                    lambda bi, h, qi, base=base: (base // 2 + h, bi, qi, 0))
    full = lambda base: pl.BlockSpec((2, None, s, LANES),
                                     lambda bi, h, qi, base=base: (base // 2 + h, bi, 0, 0))
    table_spec = pl.BlockSpec((s, LANES), lambda bi, h, qi: (0, 0))
    vec = lambda n: pl.BlockSpec((1, n), lambda bi, h, qi: (0, 0))
    return pl.pallas_call(
        _diff_attn_kernel,
        out_shape=jax.ShapeDtypeStruct((DIFF_HEADS, b, s, DIFF_V_DIM), BF16),
        grid=(b, DIFF_HEADS, s // t),
        in_specs=[blk(SLAB_DQ), full(SLAB_DK), full(SLAB_DV), blk(SLAB_DZ),
                  table_spec, table_spec,
                  vec(DIFF_HEAD_DIM), vec(DIFF_HEAD_DIM), vec(DIFF_HEAD_DIM), vec(DIFF_HEAD_DIM),
                  vec(DIFF_V_DIM)],
        out_specs=pl.BlockSpec((None, None, t, DIFF_V_DIM), lambda bi, h, qi: (h, bi, qi, 0)),
        scratch_shapes=[pltpu.VMEM((2, s, DIFF_HEAD_DIM), BF16),
                        pltpu.VMEM((s, DIFF_V_DIM), BF16)],
        compiler_params=_params(3),
        name="diff_attention",
    )(proj4, proj4, proj4, proj4, cos, sin_signed, lq1, lk1, lq2, lk2, subln_w)


def _merge_kernel(a_ref, b_ref, w1_ref, w2_ref, g1_ref, g2_ref, o_ref, w1bf_ref, w2bf_ref):
    @pl.when(pl.program_id(1) == 0)
    def _():
        _cast_weight(w1_ref, w1bf_ref)
        _cast_weight(w2_ref, w2bf_ref)

    kw = a_ref.shape[2]

    def up(x_ref, wbf_ref):
        acc = None
        for k in range(x_ref.shape[0]):
            part = jnp.dot(x_ref[k], wbf_ref[k * kw:(k + 1) * kw, :], preferred_element_type=F32)
            acc = part if acc is None else acc + part
        return acc

    up1 = up(a_ref, w1bf_ref)
    up2 = up(b_ref, w2bf_ref)
    for j in range(g1_ref.shape[0]):
        sl = slice(j * LANES, (j + 1) * LANES)
        o_ref[:, sl] = (jax.nn.sigmoid(g1_ref[j].astype(F32)) * up1[:, sl]
                        + jax.nn.sigmoid(g2_ref[j].astype(F32)) * up2[:, sl]).astype(o_ref.dtype)


def _gated_merge(o_ret, o_diff, w_ret_up, w_diff_up, proj, tm=512, tn=512):
    nk, m, kw = o_ret.shape
    d = w_ret_up.shape[1]
    gs = tn // LANES
    x_spec = pl.BlockSpec((nk, tm, kw), lambda j, i: (0, i, 0))
    w_spec = pl.BlockSpec((nk * kw, tn), lambda j, i: (0, j))
    g_spec = lambda base: pl.BlockSpec((gs, tm, LANES),
                                       lambda j, i, base=base: (base // gs + j, i, 0))
    return pl.pallas_call(
        _merge_kernel,
        out_shape=jax.ShapeDtypeStruct((m, d), BF16),
        grid=(d // tn, m // tm),
        in_specs=[x_spec, x_spec, w_spec, w_spec, g_spec(SLAB_GR), g_spec(SLAB_GD)],
        out_specs=pl.BlockSpec((tm, tn), lambda j, i: (i, j)),
        scratch_shapes=[pltpu.VMEM((nk * kw, tn), BF16), pltpu.VMEM((nk * kw, tn), BF16)],
        compiler_params=_params(2),
        name="gated_merge",
    )(o_ret, o_diff, w_ret_up, w_diff_up, proj, proj)


def _outproj_kernel(mx_ref, w_ref, x_ref, fw_ref, o_ref, wbf_ref):
    @pl.when(pl.program_id(0) == 0)
    def _():
        _cast_weight(w_ref, wbf_ref)

    h = x_ref[...] + jnp.dot(mx_ref[...], wbf_ref[...], preferred_element_type=F32)
    ms = jnp.mean(h * h, axis=-1, keepdims=True)
    o_ref[...] = h * lax.rsqrt(ms + NORM_EPS) * fw_ref[...]


def _output_projection(mixed, w_out, x2, final_w, tm=256):
    m, d = x2.shape
    return pl.pallas_call(
        _outproj_kernel,
        out_shape=jax.ShapeDtypeStruct((m, d), F32),
        grid=(m // tm,),
        in_specs=[pl.BlockSpec((tm, d), lambda i: (i, 0)),
                  pl.BlockSpec((d, d), lambda i: (0, 0), pipeline_mode=pl.Buffered(1)),
                  pl.BlockSpec((tm, d), lambda i: (i, 0)),
                  pl.BlockSpec((1, d), lambda i: (0, 0))],
        out_specs=pl.BlockSpec((tm, d), lambda i: (i, 0)),
        scratch_shapes=[pltpu.VMEM((d, d), BF16)],
        compiler_params=_params(1),
        name="output_projection",
    )(mixed, w_out, x2, final_w)


def _rotary_tables(s, inv_freq):
    pos = jnp.arange(s, dtype=F32)
    ang = pos[:, None] * inv_freq[None, :]
    emb = jnp.concatenate([ang, ang], axis=-1)
    half = emb.shape[-1] // 2
    sign = jnp.concatenate([-jnp.ones((half,), F32), jnp.ones((half,), F32)])
    return jnp.cos(emb), jnp.sin(emb) * sign[None, :]


def kernel(x, norm_w, w_in, w_ret_up, w_diff_up, w_out, lambda_q1, lambda_k1, lambda_q2,
           lambda_k2, subln_w, final_norm_w):
    b, s, d = x.shape
    assert d == D_MODEL and w_in.shape == (1, D_MODEL, IN_WIDTH)
    x2 = x.reshape(b * s, d)

    ret_inv_freq = jnp.exp(-math.log(10000.0) * jnp.linspace(0.0, 1.0, RET_QK_DIM // 2, dtype=F32))
    rope_inv_freq = ROPE_THETA ** (-jnp.arange(0, DIFF_HEAD_DIM, 2, dtype=F32) / DIFF_HEAD_DIM)
    ret_cos, ret_sin = _rotary_tables(s, ret_inv_freq)
    rope_cos, rope_sin = _rotary_tables(s, rope_inv_freq)
    log_gamma = jnp.log1p(-jnp.exp2(-5.0 - jnp.arange(RET_HEADS, dtype=F32)))

    xn = _input_rmsnorm(x2, norm_w[0][None, :])
    proj = _input_projection(xn, w_in[0])
    proj4 = proj.reshape(N_SLABS, b, s, LANES)

    o_ret = _retention(proj4, log_gamma, ret_cos, ret_sin)
    o_diff = _diff_attention(proj4, rope_cos, rope_sin, lambda_q1, lambda_k1, lambda_q2,
                             lambda_k2, subln_w)
    o_ret = o_ret.reshape(RET_HEADS, b * s, RET_V_DIM)
    o_diff = o_diff.reshape(DIFF_HEADS, b * s, DIFF_V_DIM)

    mixed = _gated_merge(o_ret, o_diff, w_ret_up[0], w_diff_up[0], proj)
    out = _output_projection(mixed, w_out[0], x2, final_norm_w[None, :])
    return out.reshape(b, s, d)
```

```python
import functools
import math

import jax
import jax.numpy as jnp
from jax import lax
from jax.experimental import pallas as pl
from jax.experimental.pallas import tpu as pltpu

F32 = jnp.float32
BF16 = jnp.bfloat16

D_MODEL = 2048
RET_QK_DIM = 128
RET_V_DIM = 256
RET_HEADS = D_MODEL // RET_V_DIM
DIFF_HEAD_DIM = 128
DIFF_V_DIM = 2 * DIFF_HEAD_DIM
DIFF_HEADS = D_MODEL // DIFF_V_DIM
ROPE_THETA = 10000.0
NORM_EPS = 1e-6
SUBLN_EPS = 1e-5
LAMBDA_INIT = 0.8 - 0.6 * math.exp(-0.3 * 0)

RET_QK_WIDTH = RET_HEADS * RET_QK_DIM
RET_WIDTH = RET_HEADS * RET_V_DIM
DIFF_QK_WIDTH = DIFF_HEADS * 2 * DIFF_HEAD_DIM
DIFF_WIDTH = DIFF_HEADS * DIFF_V_DIM
IN_WIDTH = 2 * RET_QK_WIDTH + 2 * RET_WIDTH + 2 * DIFF_QK_WIDTH + 2 * DIFF_WIDTH + 2 * D_MODEL

LANES = 128
SLAB_RQ = 0
SLAB_RK = SLAB_RQ + RET_QK_WIDTH // LANES
SLAB_RV = SLAB_RK + RET_QK_WIDTH // LANES
SLAB_RZ = SLAB_RV + RET_WIDTH // LANES
SLAB_DQ = SLAB_RZ + RET_WIDTH // LANES
SLAB_DK = SLAB_DQ + DIFF_QK_WIDTH // LANES
SLAB_DV = SLAB_DK + DIFF_QK_WIDTH // LANES
SLAB_DZ = SLAB_DV + DIFF_WIDTH // LANES
SLAB_GR = SLAB_DZ + DIFF_WIDTH // LANES
SLAB_GD = SLAB_GR + D_MODEL // LANES
N_SLABS = IN_WIDTH // LANES

VMEM_LIMIT = 56 * 1024 * 1024

RET_CHUNK = 256
ATT_BLOCK = 256


def _params(n_axes, vmem=VMEM_LIMIT):
    return pltpu.CompilerParams(dimension_semantics=("arbitrary",) * n_axes,
                                vmem_limit_bytes=vmem)


def _rmsnorm_kernel(x_ref, w_ref, o_ref):
    x = x_ref[...]
    ms = jnp.mean(x * x, axis=-1, keepdims=True)
    o_ref[...] = (x * lax.rsqrt(ms + NORM_EPS) * w_ref[...]).astype(o_ref.dtype)


def _input_rmsnorm(x2, w, tm=512):
    m, d = x2.shape
    return pl.pallas_call(
        _rmsnorm_kernel,
        out_shape=jax.ShapeDtypeStruct((m, d), BF16),
        grid=(m // tm,),
        in_specs=[pl.BlockSpec((tm, d), lambda i: (i, 0)),
                  pl.BlockSpec((1, d), lambda i: (0, 0))],
        out_specs=pl.BlockSpec((tm, d), lambda i: (i, 0)),
        compiler_params=_params(1),
        name="input_rmsnorm",
    )(x2, w)


def _cast_weight(w_ref, wbf_ref, rows=128):
    def body(i, c):
        r = pl.multiple_of(i * rows, rows)
        wbf_ref[pl.ds(r, rows), :] = w_ref[pl.ds(r, rows), :].astype(BF16)
        return c
    lax.fori_loop(0, w_ref.shape[0] // rows, body, 0)


def _inproj_kernel(x_ref, w_ref, o_ref, wbf_ref):
    @pl.when(pl.program_id(1) == 0)
    def _():
        _cast_weight(w_ref, wbf_ref)

    acc = jnp.dot(x_ref[...], wbf_ref[...], preferred_element_type=F32)
    for j in range(o_ref.shape[0]):
        o_ref[j] = acc[:, j * LANES:(j + 1) * LANES].astype(o_ref.dtype)


def _input_projection(xn, w_in, tm=1024, tn=1024):
    m, d = xn.shape
    n = w_in.shape[1]
    return pl.pallas_call(
        _inproj_kernel,
        out_shape=jax.ShapeDtypeStruct((n // LANES, m, LANES), BF16),
        grid=(n // tn, m // tm),
        in_specs=[pl.BlockSpec((tm, d), lambda j, i: (i, 0)),
                  pl.BlockSpec((d, tn), lambda j, i: (0, j))],
        out_specs=pl.BlockSpec((tn // LANES, tm, LANES), lambda j, i: (j, i, 0)),
        scratch_shapes=[pltpu.VMEM((d, tn), BF16)],
        compiler_params=_params(2),
        name="input_projection",
    )(xn, w_in)


def _rotary(x, cos, sin_signed):
    return x * cos + pltpu.roll(x, x.shape[-1] // 2, axis=x.ndim - 1) * sin_signed


def _silu(z):
    return z * jax.nn.sigmoid(z)


def _retention_kernel(lg_ref, q_ref, k_ref, v_ref, z_ref, cos_ref, sin_ref, o_ref,
                      qr_ref, kr_ref, state_ref):
    s_len = q_ref.shape[0]
    c = RET_CHUNK
    lg = lg_ref[pl.program_id(1)]

    cos = cos_ref[...]
    sin = sin_ref[...]
    qr_ref[...] = _rotary(q_ref[...].astype(F32), cos, sin).astype(BF16)
    kr_ref[...] = _rotary(k_ref[...].astype(F32) * (RET_QK_DIM ** -0.5), cos, sin).astype(BF16)

    row = lax.broadcasted_iota(jnp.int32, (c, c), 0).astype(F32)
    col = lax.broadcasted_iota(jnp.int32, (c, c), 1).astype(F32)
    rel = row - col
    decay_mask = jnp.where(rel >= 0.0, jnp.exp(lg * jnp.maximum(rel, 0.0)), 0.0)
    rowv = lax.broadcasted_iota(jnp.int32, (c, RET_V_DIM), 0).astype(F32)
    zeta = jnp.exp(lg * (c - 1.0 - rowv))
    xi = jnp.exp(lg * (rowv + 1.0))
    chunk_decay = jnp.exp(jnp.zeros((1, RET_V_DIM), F32) + lg * c)

    state_ref[...] = jnp.zeros_like(state_ref)

    def chunk(n, carry):
        r = pl.multiple_of(n * c, c)
        qc = qr_ref[pl.ds(r, c), :]
        kc = kr_ref[pl.ds(r, c), :]
        vc = jnp.concatenate([v_ref[0, pl.ds(r, c), :], v_ref[1, pl.ds(r, c), :]], axis=-1)
        zc = jnp.concatenate([z_ref[0, pl.ds(r, c), :], z_ref[1, pl.ds(r, c), :]], axis=-1)

        scores = lax.dot_general(qc, kc, (((1,), (1,)), ((), ())),
                                 preferred_element_type=F32) * decay_mask
        o = jnp.dot(scores.astype(BF16), vc, preferred_element_type=F32)
        prev = state_ref[...]
        o = o + jnp.dot(qc, prev.astype(BF16), preferred_element_type=F32) * xi
        kv = lax.dot_general(kc, (vc.astype(F32) * zeta).astype(BF16),
                             (((0,), (0,)), ((), ())), preferred_element_type=F32)
        state_ref[...] = prev * chunk_decay + kv

        ms = jnp.mean(o * o, axis=-1, keepdims=True)
        o = o * lax.rsqrt(ms + NORM_EPS)
        o_ref[pl.ds(r, c), :] = (o * _silu(zc.astype(F32))).astype(o_ref.dtype)
        return carry

    lax.fori_loop(0, s_len // c, chunk, 0)


def _retention(proj4, log_gamma, cos, sin_signed):
    _, b, s, _ = proj4.shape
    qk_spec = lambda base: pl.BlockSpec((None, None, s, LANES),
                                        lambda bi, h, base=base: (base + h, bi, 0, 0))
    pair_spec = lambda base: pl.BlockSpec((2, None, s, LANES),
                                          lambda bi, h, base=base: (base // 2 + h, bi, 0, 0))
    table_spec = pl.BlockSpec((s, LANES), lambda bi, h: (0, 0))
    return pl.pallas_call(
        _retention_kernel,
        out_shape=jax.ShapeDtypeStruct((RET_HEADS, b, s, RET_V_DIM), BF16),
        grid=(b, RET_HEADS),
        in_specs=[pl.BlockSpec(memory_space=pltpu.SMEM),
                  qk_spec(SLAB_RQ), qk_spec(SLAB_RK), pair_spec(SLAB_RV), pair_spec(SLAB_RZ),
                  table_spec, table_spec],
        out_specs=pl.BlockSpec((None, None, s, RET_V_DIM), lambda bi, h: (h, bi, 0, 0)),
        scratch_shapes=[pltpu.VMEM((s, RET_QK_DIM), BF16),
                        pltpu.VMEM((s, RET_QK_DIM), BF16),
                        pltpu.VMEM((RET_QK_DIM, RET_V_DIM), F32)],
        compiler_params=_params(2),
        name="retention",
    )(log_gamma, proj4, proj4, proj4, proj4, cos, sin_signed)


def _diff_attn_kernel(q_ref, k_ref, v_ref, z_ref, cos_ref, sin_ref,
                      lq1_ref, lk1_ref, lq2_ref, lk2_ref, subln_ref, o_ref,
                      qr_ref, kr_ref, vc_ref):
    t = ATT_BLOCK
    s_len = vc_ref.shape[0]
    cos = cos_ref[...]
    sin = sin_ref[...]
    q_scale = (DIFF_HEAD_DIM ** -0.5) * math.log2(math.e)
    for c in range(2):
        kr_ref[c] = _rotary(k_ref[c].astype(F32), cos, sin).astype(BF16)
        qr_ref[c] = _rotary(q_ref[c].astype(F32) * q_scale, cos, sin).astype(BF16)
    vc_ref[...] = jnp.concatenate([v_ref[0], v_ref[1]], axis=-1)

    lam = (jnp.exp(jnp.sum(lq1_ref[...] * lk1_ref[...], axis=-1, keepdims=True))
           - jnp.exp(jnp.sum(lq2_ref[...] * lk2_ref[...], axis=-1, keepdims=True))
           + LAMBDA_INIT)
    causal = (lax.broadcasted_iota(jnp.int32, (t, t), 1)
              <= lax.broadcasted_iota(jnp.int32, (t, t), 0))
    nt_dims = (((1,), (1,)), ((), ()))

    for n in range(s_len // t):
        r0 = n * t
        res = []
        for c in range(2):
            q = qr_ref[c, r0:r0 + t, :]
            s_diag = jnp.where(causal,
                               lax.dot_general(q, kr_ref[c, r0:r0 + t, :], nt_dims,
                                               preferred_element_type=F32), -jnp.inf)
            m = jnp.max(s_diag, axis=-1, keepdims=True)
            if n > 0:
                s_off = lax.dot_general(q, kr_ref[c, 0:r0, :], nt_dims,
                                        preferred_element_type=F32)
                m = jnp.maximum(m, jnp.max(s_off, axis=-1, keepdims=True))
            p_diag = jnp.exp2(s_diag - m)
            l = jnp.sum(p_diag, axis=-1, keepdims=True)
            acc = jnp.dot(p_diag.astype(BF16), vc_ref[r0:r0 + t, :], preferred_element_type=F32)
            if n > 0:
                p_off = jnp.exp2(s_off - m)
                l = l + jnp.sum(p_off, axis=-1, keepdims=True)
                acc = acc + jnp.dot(p_off.astype(BF16), vc_ref[0:r0, :],
                                    preferred_element_type=F32)
            res.append((acc, l))
        o = res[0][0] * (1.0 / res[0][1]) - res[1][0] * (lam / res[1][1])
        ms = jnp.mean(o * o, axis=-1, keepdims=True)
        o = o * lax.rsqrt(ms + SUBLN_EPS) * subln_ref[...] * (1.0 - LAMBDA_INIT)
        z = jnp.concatenate([z_ref[0, r0:r0 + t, :], z_ref[1, r0:r0 + t, :]], axis=-1).astype(F32)
        o_ref[r0:r0 + t, :] = (o * _silu(z)).astype(o_ref.dtype)


def _diff_attention(proj4, cos, sin_signed, lq1, lk1, lq2, lk2, subln_w):
    _, b, s, _ = proj4.shape
    full = lambda base: pl.BlockSpec((2, None, s, LANES),
                                     lambda bi, h, base=base: (base // 2 + h, bi, 0, 0))
    table_spec = pl.BlockSpec((s, LANES), lambda bi, h: (0, 0))
    vec = lambda n: pl.BlockSpec((1, n), lambda bi, h: (0, 0))
    return pl.pallas_call(
        _diff_attn_kernel,
        out_shape=jax.ShapeDtypeStruct((DIFF_HEADS, b, s, DIFF_V_DIM), BF16),
        grid=(b, DIFF_HEADS),
        in_specs=[full(SLAB_DQ), full(SLAB_DK), full(SLAB_DV), full(SLAB_DZ),
                  table_spec, table_spec,
                  vec(DIFF_HEAD_DIM), vec(DIFF_HEAD_DIM), vec(DIFF_HEAD_DIM), vec(DIFF_HEAD_DIM),
                  vec(DIFF_V_DIM)],
        out_specs=pl.BlockSpec((None, None, s, DIFF_V_DIM), lambda bi, h: (h, bi, 0, 0)),
        scratch_shapes=[pltpu.VMEM((2, s, DIFF_HEAD_DIM), BF16),
                        pltpu.VMEM((2, s, DIFF_HEAD_DIM), BF16),
                        pltpu.VMEM((s, DIFF_V_DIM), BF16)],
        compiler_params=_params(2),
        name="diff_attention",
    )(proj4, proj4, proj4, proj4, cos, sin_signed, lq1, lk1, lq2, lk2, subln_w)


def _merge_kernel(a_ref, b_ref, w1_ref, w2_ref, g1_ref, g2_ref, o_ref, w1bf_ref, w2bf_ref):
    @pl.when(pl.program_id(1) == 0)
    def _():
        _cast_weight(w1_ref, w1bf_ref)
        _cast_weight(w2_ref, w2bf_ref)

    kw = a_ref.shape[2]

    def up(x_ref, wbf_ref):
        acc = None
        for k in range(x_ref.shape[0]):
            part = jnp.dot(x_ref[k], wbf_ref[k * kw:(k + 1) * kw, :], preferred_element_type=F32)
            acc = part if acc is None else acc + part
        return acc

    up1 = up(a_ref, w1bf_ref)
    up2 = up(b_ref, w2bf_ref)
    for j in range(g1_ref.shape[0]):
        sl = slice(j * LANES, (j + 1) * LANES)
        o_ref[:, sl] = (jax.nn.sigmoid(g1_ref[j].astype(F32)) * up1[:, sl]
                        + jax.nn.sigmoid(g2_ref[j].astype(F32)) * up2[:, sl]).astype(o_ref.dtype)


def _gated_merge(o_ret, o_diff, w_ret_up, w_diff_up, proj, tm=512, tn=512):
    nk, m, kw = o_ret.shape
    d = w_ret_up.shape[1]
    gs = tn // LANES
    x_spec = pl.BlockSpec((nk, tm, kw), lambda j, i: (0, i, 0))
    w_spec = pl.BlockSpec((nk * kw, tn), lambda j, i: (0, j))
    g_spec = lambda base: pl.BlockSpec((gs, tm, LANES),
                                       lambda j, i, base=base: (base // gs + j, i, 0))
    return pl.pallas_call(
        _merge_kernel,
        out_shape=jax.ShapeDtypeStruct((m, d), BF16),
        grid=(d // tn, m // tm),
        in_specs=[x_spec, x_spec, w_spec, w_spec, g_spec(SLAB_GR), g_spec(SLAB_GD)],
        out_specs=pl.BlockSpec((tm, tn), lambda j, i: (i, j)),
        scratch_shapes=[pltpu.VMEM((nk * kw, tn), BF16), pltpu.VMEM((nk * kw, tn), BF16)],
        compiler_params=_params(2),
        name="gated_merge",
    )(o_ret, o_diff, w_ret_up, w_diff_up, proj, proj)


def _outproj_kernel(mx_ref, w_ref, x_ref, fw_ref, o_ref, wbf_ref):
    @pl.when(pl.program_id(0) == 0)
    def _():
        _cast_weight(w_ref, wbf_ref)

    h = x_ref[...] + jnp.dot(mx_ref[...], wbf_ref[...], preferred_element_type=F32)
    ms = jnp.mean(h * h, axis=-1, keepdims=True)
    o_ref[...] = h * lax.rsqrt(ms + NORM_EPS) * fw_ref[...]


def _output_projection(mixed, w_out, x2, final_w, tm=256):
    m, d = x2.shape
    return pl.pallas_call(
        _outproj_kernel,
        out_shape=jax.ShapeDtypeStruct((m, d), F32),
        grid=(m // tm,),
        in_specs=[pl.BlockSpec((tm, d), lambda i: (i, 0)),
                  pl.BlockSpec((d, d), lambda i: (0, 0), pipeline_mode=pl.Buffered(1)),
                  pl.BlockSpec((tm, d), lambda i: (i, 0)),
                  pl.BlockSpec((1, d), lambda i: (0, 0))],
        out_specs=pl.BlockSpec((tm, d), lambda i: (i, 0)),
        scratch_shapes=[pltpu.VMEM((d, d), BF16)],
        compiler_params=_params(1),
        name="output_projection",
    )(mixed, w_out, x2, final_w)


def _rotary_tables(s, inv_freq):
    pos = jnp.arange(s, dtype=F32)
    ang = pos[:, None] * inv_freq[None, :]
    emb = jnp.concatenate([ang, ang], axis=-1)
    half = emb.shape[-1] // 2
    sign = jnp.concatenate([-jnp.ones((half,), F32), jnp.ones((half,), F32)])
    return jnp.cos(emb), jnp.sin(emb) * sign[None, :]


def kernel(x, norm_w, w_in, w_ret_up, w_diff_up, w_out, lambda_q1, lambda_k1, lambda_q2,
           lambda_k2, subln_w, final_norm_w):
    b, s, d = x.shape
    assert d == D_MODEL and w_in.shape == (1, D_MODEL, IN_WIDTH)
    x2 = x.reshape(b * s, d)

    ret_inv_freq = jnp.exp(-math.log(10000.0) * jnp.linspace(0.0, 1.0, RET_QK_DIM // 2, dtype=F32))
    rope_inv_freq = ROPE_THETA ** (-jnp.arange(0, DIFF_HEAD_DIM, 2, dtype=F32) / DIFF_HEAD_DIM)
    ret_cos, ret_sin = _rotary_tables(s, ret_inv_freq)
    rope_cos, rope_sin = _rotary_tables(s, rope_inv_freq)
    log_gamma = jnp.log1p(-jnp.exp2(-5.0 - jnp.arange(RET_HEADS, dtype=F32)))

    xn = _input_rmsnorm(x2, norm_w[0][None, :])
    proj = _input_projection(xn, w_in[0])
    proj4 = proj.reshape(N_SLABS, b, s, LANES)

    o_ret = _retention(proj4, log_gamma, ret_cos, ret_sin)
    o_diff = _diff_attention(proj4, rope_cos, rope_sin, lambda_q1, lambda_k1, lambda_q2,
                             lambda_k2, subln_w)
    o_ret = o_ret.reshape(RET_HEADS, b * s, RET_V_DIM)
    o_diff = o_diff.reshape(DIFF_HEADS, b * s, DIFF_V_DIM)

    mixed = _gated_merge(o_ret, o_diff, w_ret_up[0], w_diff_up[0], proj)
    out = _output_projection(mixed, w_out[0], x2, final_norm_w[None, :])
    return out.reshape(b, s, d)
```

```python
import functools
import math

import jax
import jax.numpy as jnp
from jax import lax
from jax.experimental import pallas as pl
from jax.experimental.pallas import tpu as pltpu

F32 = jnp.float32
BF16 = jnp.bfloat16

D_MODEL = 2048
RET_QK_DIM = 128
RET_V_DIM = 256
RET_HEADS = D_MODEL // RET_V_DIM
DIFF_HEAD_DIM = 128
DIFF_V_DIM = 2 * DIFF_HEAD_DIM
DIFF_HEADS = D_MODEL // DIFF_V_DIM
ROPE_THETA = 10000.0
NORM_EPS = 1e-6
SUBLN_EPS = 1e-5
LAMBDA_INIT = 0.8 - 0.6 * math.exp(-0.3 * 0)

RET_QK_WIDTH = RET_HEADS * RET_QK_DIM
RET_WIDTH = RET_HEADS * RET_V_DIM
DIFF_QK_WIDTH = DIFF_HEADS * 2 * DIFF_HEAD_DIM
DIFF_WIDTH = DIFF_HEADS * DIFF_V_DIM
IN_WIDTH = 2 * RET_QK_WIDTH + 2 * RET_WIDTH + 2 * DIFF_QK_WIDTH + 2 * DIFF_WIDTH + 2 * D_MODEL

LANES = 128
SLAB_RQ = 0
SLAB_RK = SLAB_RQ + RET_QK_WIDTH // LANES
SLAB_RV = SLAB_RK + RET_QK_WIDTH // LANES
SLAB_RZ = SLAB_RV + RET_WIDTH // LANES
SLAB_DQ = SLAB_RZ + RET_WIDTH // LANES
SLAB_DK = SLAB_DQ + DIFF_QK_WIDTH // LANES
SLAB_DV = SLAB_DK + DIFF_QK_WIDTH // LANES
SLAB_DZ = SLAB_DV + DIFF_WIDTH // LANES
SLAB_GR = SLAB_DZ + DIFF_WIDTH // LANES
SLAB_GD = SLAB_GR + D_MODEL // LANES
N_SLABS = IN_WIDTH // LANES

VMEM_LIMIT = 56 * 1024 * 1024

RET_CHUNK = 256
ATT_BLOCK = 256


def _params(n_axes, vmem=VMEM_LIMIT):
    return pltpu.CompilerParams(dimension_semantics=("arbitrary",) * n_axes,
                                vmem_limit_bytes=vmem)


def _rmsnorm_kernel(x_ref, w_ref, o_ref):
    x = x_ref[...]
    ms = jnp.mean(x * x, axis=-1, keepdims=True)
    o_ref[...] = (x * lax.rsqrt(ms + NORM_EPS) * w_ref[...]).astype(o_ref.dtype)


def _input_rmsnorm(x2, w, tm=512):
    m, d = x2.shape
    return pl.pallas_call(
        _rmsnorm_kernel,
        out_shape=jax.ShapeDtypeStruct((m, d), BF16),
        grid=(m // tm,),
        in_specs=[pl.BlockSpec((tm, d), lambda i: (i, 0)),
                  pl.BlockSpec((1, d), lambda i: (0, 0))],
        out_specs=pl.BlockSpec((tm, d), lambda i: (i, 0)),
        compiler_params=_params(1),
        name="input_rmsnorm",
    )(x2, w)


def _cast_weight(w_ref, wbf_ref, rows=128):
    def body(i, c):
        r = pl.multiple_of(i * rows, rows)
        wbf_ref[pl.ds(r, rows), :] = w_ref[pl.ds(r, rows), :].astype(BF16)
        return c
    lax.fori_loop(0, w_ref.shape[0] // rows, body, 0)


def _inproj_kernel(x_ref, w_ref, o_ref, wbf_ref):
    @pl.when(pl.program_id(1) == 0)
    def _():
        _cast_weight(w_ref, wbf_ref)

    acc = jnp.dot(x_ref[...], wbf_ref[...], preferred_element_type=F32)
    for j in range(o_ref.shape[0]):
        o_ref[j] = acc[:, j * LANES:(j + 1) * LANES].astype(o_ref.dtype)


def _input_projection(xn, w_in, tm=1024, tn=1024):
    m, d = xn.shape
    n = w_in.shape[1]
    return pl.pallas_call(
        _inproj_kernel,
        out_shape=jax.ShapeDtypeStruct((n // LANES, m, LANES), BF16),
        grid=(n // tn, m // tm),
        in_specs=[pl.BlockSpec((tm, d), lambda j, i: (i, 0)),
                  pl.BlockSpec((d, tn), lambda j, i: (0, j))],
        out_specs=pl.BlockSpec((tn // LANES, tm, LANES), lambda j, i: (j, i, 0)),
        scratch_shapes=[pltpu.VMEM((d, tn), BF16)],
        compiler_params=_params(2),
        name="input_projection",
    )(xn, w_in)


def _rotary(x, cos, sin_signed):
    return x * cos + pltpu.roll(x, x.shape[-1] // 2, axis=x.ndim - 1) * sin_signed


def _silu(z):
    return z * jax.nn.sigmoid(z)


def _retention_kernel(lg_ref, q_ref, k_ref, v_ref, z_ref, cos_ref, sin_ref, o_ref,
                      qr_ref, qx_ref, kr_ref, kz_ref, vc_ref):
    s_len = q_ref.shape[0]
    c = RET_CHUNK
    n_chunks = s_len // c
    lg = lg_ref[pl.program_id(1)]

    rowq = lax.broadcasted_iota(jnp.int32, (c, RET_QK_DIM), 0).astype(F32)
    zeta = jnp.exp(lg * (c - 1.0 - rowq))
    xi = jnp.exp(lg * (rowq + 1.0))
    for n in range(n_chunks):
        sl = slice(n * c, (n + 1) * c)
        cos = cos_ref[sl, :]
        sin = sin_ref[sl, :]
        qf = _rotary(q_ref[sl, :].astype(F32), cos, sin)
        kf = _rotary(k_ref[sl, :].astype(F32) * (RET_QK_DIM ** -0.5), cos, sin)
        qr_ref[sl, :] = qf.astype(BF16)
        qx_ref[sl, :] = (qf * xi).astype(BF16)
        kr_ref[sl, :] = kf.astype(BF16)
        kz_ref[sl, :] = (kf * zeta).astype(BF16)
    vc_ref[...] = jnp.concatenate([v_ref[0], v_ref[1]], axis=-1)

    row = lax.broadcasted_iota(jnp.int32, (c, c), 0).astype(F32)
    col = lax.broadcasted_iota(jnp.int32, (c, c), 1).astype(F32)
    rel = row - col
    decay_mask = jnp.where(rel >= 0.0, jnp.exp(lg * jnp.maximum(rel, 0.0)), 0.0)
    chunk_decay = jnp.exp(jnp.zeros((1, RET_V_DIM), F32) + lg * c)

    state = jnp.zeros((RET_QK_DIM, RET_V_DIM), F32)
    for n in range(n_chunks):
        sl = slice(n * c, (n + 1) * c)
        vc = vc_ref[sl, :]
        scores = lax.dot_general(qr_ref[sl, :], kr_ref[sl, :], (((1,), (1,)), ((), ())),
                                 preferred_element_type=F32) * decay_mask
        o = jnp.dot(scores.astype(BF16), vc, preferred_element_type=F32)
        o = o + jnp.dot(qx_ref[sl, :], state.astype(BF16), preferred_element_type=F32)
        if n + 1 < n_chunks:
            kv = lax.dot_general(kz_ref[sl, :], vc, (((0,), (0,)), ((), ())),
                                 preferred_element_type=F32)
            state = state * chunk_decay + kv

        ms = jnp.mean(o * o, axis=-1, keepdims=True)
        o = o * lax.rsqrt(ms + NORM_EPS)
        z = jnp.concatenate([z_ref[0, sl, :], z_ref[1, sl, :]], axis=-1).astype(F32)
        o_ref[sl, :] = (o * _silu(z)).astype(o_ref.dtype)


def _retention(proj4, log_gamma, cos, sin_signed):
    _, b, s, _ = proj4.shape
    qk_spec = lambda base: pl.BlockSpec((None, None, s, LANES),
                                        lambda bi, h, base=base: (base + h, bi, 0, 0))
    pair_spec = lambda base: pl.BlockSpec((2, None, s, LANES),
                                          lambda bi, h, base=base: (base // 2 + h, bi, 0, 0))
    table_spec = pl.BlockSpec((s, LANES), lambda bi, h: (0, 0))
    return pl.pallas_call(
        _retention_kernel,
        out_shape=jax.ShapeDtypeStruct((RET_HEADS, b, s, RET_V_DIM), BF16),
        grid=(b, RET_HEADS),
        in_specs=[pl.BlockSpec(memory_space=pltpu.SMEM),
                  qk_spec(SLAB_RQ), qk_spec(SLAB_RK), pair_spec(SLAB_RV), pair_spec(SLAB_RZ),
                  table_spec, table_spec],
        out_specs=pl.BlockSpec((None, None, s, RET_V_DIM), lambda bi, h: (h, bi, 0, 0)),
        scratch_shapes=[pltpu.VMEM((s, RET_QK_DIM), BF16)] * 4
                       + [pltpu.VMEM((s, RET_V_DIM), BF16)],
        compiler_params=_params(2),
        name="retention",
    )(log_gamma, proj4, proj4, proj4, proj4, cos, sin_signed)


def _diff_attn_kernel(q_ref, k_ref, v_ref, z_ref, cos_ref, sin_ref,
                      lq1_ref, lk1_ref, lq2_ref, lk2_ref, subln_ref, o_ref,
                      qr_ref, kr_ref, vc_ref):
    t = ATT_BLOCK
    s_len = vc_ref.shape[0]
    cos = cos_ref[...]
    sin = sin_ref[...]
    q_scale = (DIFF_HEAD_DIM ** -0.5) * math.log2(math.e)
    for c in range(2):
        kr_ref[c] = _rotary(k_ref[c].astype(F32), cos, sin).astype(BF16)
        qr_ref[c] = _rotary(q_ref[c].astype(F32) * q_scale, cos, sin).astype(BF16)
    vc_ref[...] = jnp.concatenate([v_ref[0], v_ref[1]], axis=-1)

    lam = (jnp.exp(jnp.sum(lq1_ref[...] * lk1_ref[...], axis=-1, keepdims=True))
           - jnp.exp(jnp.sum(lq2_ref[...] * lk2_ref[...], axis=-1, keepdims=True))
           + LAMBDA_INIT)
    causal = (lax.broadcasted_iota(jnp.int32, (t, t), 1)
              <= lax.broadcasted_iota(jnp.int32, (t, t), 0))
    nt_dims = (((1,), (1,)), ((), ()))

    for n in range(s_len // t):
        r0 = n * t
        res = []
        for c in range(2):
            q = qr_ref[c, r0:r0 + t, :]
            s_diag = jnp.where(causal,
                               lax.dot_general(q, kr_ref[c, r0:r0 + t, :], nt_dims,
                                               preferred_element_type=F32), -jnp.inf)
            m = jnp.max(s_diag, axis=-1, keepdims=True)
            if n > 0:
                s_off = lax.dot_general(q, kr_ref[c, 0:r0, :], nt_dims,
                                        preferred_element_type=F32)
                m = jnp.maximum(m, jnp.max(s_off, axis=-1, keepdims=True))
            p_diag = jnp.exp2(s_diag - m)
            l = jnp.sum(p_diag, axis=-1, keepdims=True)
            acc = jnp.dot(p_diag.astype(BF16), vc_ref[r0:r0 + t, :], preferred_element_type=F32)
            if n > 0:
                p_off = jnp.exp2(s_off - m)
                l = l + jnp.sum(p_off, axis=-1, keepdims=True)
                acc = acc + jnp.dot(p_off.astype(BF16), vc_ref[0:r0, :],
                                    preferred_element_type=F32)
            res.append((acc, l))
        o = res[0][0] * (1.0 / res[0][1]) - res[1][0] * (lam / res[1][1])
        ms = jnp.mean(o * o, axis=-1, keepdims=True)
        o = o * lax.rsqrt(ms + SUBLN_EPS) * subln_ref[...] * (1.0 - LAMBDA_INIT)
        z = jnp.concatenate([z_ref[0, r0:r0 + t, :], z_ref[1, r0:r0 + t, :]], axis=-1).astype(F32)
        o_ref[r0:r0 + t, :] = (o * _silu(z)).astype(o_ref.dtype)


def _diff_attention(proj4, cos, sin_signed, lq1, lk1, lq2, lk2, subln_w):
    _, b, s, _ = proj4.shape
    full = lambda base: pl.BlockSpec((2, None, s, LANES),
                                     lambda bi, h, base=base: (base // 2 + h, bi, 0, 0))
    table_spec = pl.BlockSpec((s, LANES), lambda bi, h: (0, 0))
    vec = lambda n: pl.BlockSpec((1, n), lambda bi, h: (0, 0))
    return pl.pallas_call(
        _diff_attn_kernel,
        out_shape=jax.ShapeDtypeStruct((DIFF_HEADS, b, s, DIFF_V_DIM), BF16),
        grid=(b, DIFF_HEADS),
        in_specs=[full(SLAB_DQ), full(SLAB_DK), full(SLAB_DV), full(SLAB_DZ),
                  table_spec, table_spec,
                  vec(DIFF_HEAD_DIM), vec(DIFF_HEAD_DIM), vec(DIFF_HEAD_DIM), vec(DIFF_HEAD_DIM),
                  vec(DIFF_V_DIM)],
        out_specs=pl.BlockSpec((None, None, s, DIFF_V_DIM), lambda bi, h: (h, bi, 0, 0)),
        scratch_shapes=[pltpu.VMEM((2, s, DIFF_HEAD_DIM), BF16),
                        pltpu.VMEM((2, s, DIFF_HEAD_DIM), BF16),
                        pltpu.VMEM((s, DIFF_V_DIM), BF16)],
        compiler_params=_params(2),
        name="diff_attention",
    )(proj4, proj4, proj4, proj4, cos, sin_signed, lq1, lk1, lq2, lk2, subln_w)


def _merge_kernel(a_ref, b_ref, w1_ref, w2_ref, g1_ref, g2_ref, o_ref, w1bf_ref, w2bf_ref):
    @pl.when(pl.program_id(1) == 0)
    def _():
        _cast_weight(w1_ref, w1bf_ref)
        _cast_weight(w2_ref, w2bf_ref)

    kw = a_ref.shape[2]

    def up(x_ref, wbf_ref):
        acc = None
        for k in range(x_ref.shape[0]):
            part = jnp.dot(x_ref[k], wbf_ref[k * kw:(k + 1) * kw, :], preferred_element_type=F32)
            acc = part if acc is None else acc + part
        return acc

    up1 = up(a_ref, w1bf_ref)
    up2 = up(b_ref, w2bf_ref)
    for j in range(g1_ref.shape[0]):
        sl = slice(j * LANES, (j + 1) * LANES)
        o_ref[:, sl] = (jax.nn.sigmoid(g1_ref[j].astype(F32)) * up1[:, sl]
                        + jax.nn.sigmoid(g2_ref[j].astype(F32)) * up2[:, sl]).astype(o_ref.dtype)


def _gated_merge(o_ret, o_diff, w_ret_up, w_diff_up, proj, tm=512, tn=512):
    nk, m, kw = o_ret.shape
    d = w_ret_up.shape[1]
    gs = tn // LANES
    x_spec = pl.BlockSpec((nk, tm, kw), lambda j, i: (0, i, 0))
    w_spec = pl.BlockSpec((nk * kw, tn), lambda j, i: (0, j))
    g_spec = lambda base: pl.BlockSpec((gs, tm, LANES),
                                       lambda j, i, base=base: (base // gs + j, i, 0))
    return pl.pallas_call(
        _merge_kernel,
        out_shape=jax.ShapeDtypeStruct((m, d), BF16),
        grid=(d // tn, m // tm),
        in_specs=[x_spec, x_spec, w_spec, w_spec, g_spec(SLAB_GR), g_spec(SLAB_GD)],
        out_specs=pl.BlockSpec((tm, tn), lambda j, i: (i, j)),
        scratch_shapes=[pltpu.VMEM((nk * kw, tn), BF16), pltpu.VMEM((nk * kw, tn), BF16)],
        compiler_params=_params(2),
        name="gated_merge",
    )(o_ret, o_diff, w_ret_up, w_diff_up, proj, proj)


def _outproj_kernel(mx_ref, w_ref, x_ref, fw_ref, o_ref, wbf_ref):
    @pl.when(pl.program_id(0) == 0)
    def _():
        _cast_weight(w_ref, wbf_ref)

    h = x_ref[...] + jnp.dot(mx_ref[...], wbf_ref[...], preferred_element_type=F32)
    ms = jnp.mean(h * h, axis=-1, keepdims=True)
    o_ref[...] = h * lax.rsqrt(ms + NORM_EPS) * fw_ref[...]


def _output_projection(mixed, w_out, x2, final_w, tm=256):
    m, d = x2.shape
    return pl.pallas_call(
        _outproj_kernel,
        out_shape=jax.ShapeDtypeStruct((m, d), F32),
        grid=(m // tm,),
        in_specs=[pl.BlockSpec((tm, d), lambda i: (i, 0)),
                  pl.BlockSpec((d, d), lambda i: (0, 0), pipeline_mode=pl.Buffered(1)),
                  pl.BlockSpec((tm, d), lambda i: (i, 0)),
                  pl.BlockSpec((1, d), lambda i: (0, 0))],
        out_specs=pl.BlockSpec((tm, d), lambda i: (i, 0)),
        scratch_shapes=[pltpu.VMEM((d, d), BF16)],
        compiler_params=_params(1),
        name="output_projection",
    )(mixed, w_out, x2, final_w)


def _rotary_tables(s, inv_freq):
    pos = jnp.arange(s, dtype=F32)
    ang = pos[:, None] * inv_freq[None, :]
    emb = jnp.concatenate([ang, ang], axis=-1)
    half = emb.shape[-1] // 2
    sign = jnp.concatenate([-jnp.ones((half,), F32), jnp.ones((half,), F32)])
    return jnp.cos(emb), jnp.sin(emb) * sign[None, :]


def kernel(x, norm_w, w_in, w_ret_up, w_diff_up, w_out, lambda_q1, lambda_k1, lambda_q2,
           lambda_k2, subln_w, final_norm_w):
    b, s, d = x.shape
    assert d == D_MODEL and w_in.shape == (1, D_MODEL, IN_WIDTH)
    x2 = x.reshape(b * s, d)

    ret_inv_freq = jnp.exp(-math.log(10000.0) * jnp.linspace(0.0, 1.0, RET_QK_DIM // 2, dtype=F32))
    rope_inv_freq = ROPE_THETA ** (-jnp.arange(0, DIFF_HEAD_DIM, 2, dtype=F32) / DIFF_HEAD_DIM)
    ret_cos, ret_sin = _rotary_tables(s, ret_inv_freq)
    rope_cos, rope_sin = _rotary_tables(s, rope_inv_freq)
    log_gamma = jnp.log1p(-jnp.exp2(-5.0 - jnp.arange(RET_HEADS, dtype=F32)))

    xn = _input_rmsnorm(x2, norm_w[0][None, :])
    proj = _input_projection(xn, w_in[0])
    proj4 = proj.reshape(N_SLABS, b, s, LANES)

    o_ret = _retention(proj4, log_gamma, ret_cos, ret_sin)
    o_diff = _diff_attention(proj4, rope_cos, rope_sin, lambda_q1, lambda_k1, lambda_q2,
                             lambda_k2, subln_w)
    o_ret = o_ret.reshape(RET_HEADS, b * s, RET_V_DIM)
    o_diff = o_diff.reshape(DIFF_HEADS, b * s, DIFF_V_DIM)

    mixed = _gated_merge(o_ret, o_diff, w_ret_up[0], w_diff_up[0], proj)
    out = _output_projection(mixed, w_out[0], x2, final_norm_w[None, :])
    return out.reshape(b, s, d)
```

```python
import functools
import math

import jax
import jax.numpy as jnp
from jax import lax
from jax.experimental import pallas as pl
from jax.experimental.pallas import tpu as pltpu

F32 = jnp.float32
BF16 = jnp.bfloat16

D_MODEL = 2048
RET_QK_DIM = 128
RET_V_DIM = 256
RET_HEADS = D_MODEL // RET_V_DIM
DIFF_HEAD_DIM = 128
DIFF_V_DIM = 2 * DIFF_HEAD_DIM
DIFF_HEADS = D_MODEL // DIFF_V_DIM
ROPE_THETA = 10000.0
NORM_EPS = 1e-6
SUBLN_EPS = 1e-5
LAMBDA_INIT = 0.8 - 0.6 * math.exp(-0.3 * 0)

RET_QK_WIDTH = RET_HEADS * RET_QK_DIM
RET_WIDTH = RET_HEADS * RET_V_DIM
DIFF_QK_WIDTH = DIFF_HEADS * 2 * DIFF_HEAD_DIM
DIFF_WIDTH = DIFF_HEADS * DIFF_V_DIM
IN_WIDTH = 2 * RET_QK_WIDTH + 2 * RET_WIDTH + 2 * DIFF_QK_WIDTH + 2 * DIFF_WIDTH + 2 * D_MODEL

LANES = 128
SLAB_RQ = 0
SLAB_RK = SLAB_RQ + RET_QK_WIDTH // LANES
SLAB_RV = SLAB_RK + RET_QK_WIDTH // LANES
SLAB_RZ = SLAB_RV + RET_WIDTH // LANES
SLAB_DQ = SLAB_RZ + RET_WIDTH // LANES
SLAB_DK = SLAB_DQ + DIFF_QK_WIDTH // LANES
SLAB_DV = SLAB_DK + DIFF_QK_WIDTH // LANES
SLAB_DZ = SLAB_DV + DIFF_WIDTH // LANES
SLAB_GR = SLAB_DZ + DIFF_WIDTH // LANES
SLAB_GD = SLAB_GR + D_MODEL // LANES
N_SLABS = IN_WIDTH // LANES

VMEM_LIMIT = 56 * 1024 * 1024

RET_CHUNK = 256
ATT_BLOCK = 256
ATT_SKEW = 1


def _params(n_axes, vmem=VMEM_LIMIT, flags=None):
    return pltpu.CompilerParams(dimension_semantics=("arbitrary",) * n_axes,
                                vmem_limit_bytes=vmem, flags=flags)


def _rmsnorm_kernel(x_ref, w_ref, o_ref):
    x = x_ref[...]
    ms = jnp.mean(x * x, axis=-1, keepdims=True)
    o_ref[...] = (x * lax.rsqrt(ms + NORM_EPS) * w_ref[...]).astype(o_ref.dtype)


def _input_rmsnorm(x2, w, tm=512):
    m, d = x2.shape
    return pl.pallas_call(
        _rmsnorm_kernel,
        out_shape=jax.ShapeDtypeStruct((m, d), BF16),
        grid=(m // tm,),
        in_specs=[pl.BlockSpec((tm, d), lambda i: (i, 0)),
                  pl.BlockSpec((1, d), lambda i: (0, 0))],
        out_specs=pl.BlockSpec((tm, d), lambda i: (i, 0)),
        compiler_params=_params(1),
        name="input_rmsnorm",
    )(x2, w)


def _cast_weight(w_ref, wbf_ref, rows=128):
    def body(i, c):
        r = pl.multiple_of(i * rows, rows)
        wbf_ref[pl.ds(r, rows), :] = w_ref[pl.ds(r, rows), :].astype(BF16)
        return c
    lax.fori_loop(0, w_ref.shape[0] // rows, body, 0)


def _inproj_kernel(x_ref, w_ref, o_ref, wbf_ref):
    @pl.when(pl.program_id(1) == 0)
    def _():
        _cast_weight(w_ref, wbf_ref)

    acc = jnp.dot(x_ref[...], wbf_ref[...], preferred_element_type=F32)
    for j in range(o_ref.shape[0]):
        o_ref[j] = acc[:, j * LANES:(j + 1) * LANES].astype(o_ref.dtype)


def _input_projection(xn, w_in, tm=1024, tn=1024):
    m, d = xn.shape
    n = w_in.shape[1]
    return pl.pallas_call(
        _inproj_kernel,
        out_shape=jax.ShapeDtypeStruct((n // LANES, m, LANES), BF16),
        grid=(n // tn, m // tm),
        in_specs=[pl.BlockSpec((tm, d), lambda j, i: (i, 0)),
                  pl.BlockSpec((d, tn), lambda j, i: (0, j))],
        out_specs=pl.BlockSpec((tn // LANES, tm, LANES), lambda j, i: (j, i, 0)),
        scratch_shapes=[pltpu.VMEM((d, tn), BF16)],
        compiler_params=_params(2),
        name="input_projection",
    )(xn, w_in)


def _rotary(x, cos, sin_signed):
    return x * cos + pltpu.roll(x, x.shape[-1] // 2, axis=x.ndim - 1) * sin_signed


def _silu(z):
    return z * jax.nn.sigmoid(z)


def _retention_kernel(lg_ref, q_ref, k_ref, v_ref, z_ref, cos_ref, sin_ref, o_ref,
                      qr_ref, qx_ref, kr_ref, kz_ref, vc_ref):
    s_len = q_ref.shape[0]
    c = RET_CHUNK
    n_chunks = s_len // c
    lg = lg_ref[pl.program_id(1)]

    rowq = lax.broadcasted_iota(jnp.int32, (c, RET_QK_DIM), 0).astype(F32)
    zeta = jnp.exp(lg * (c - 1.0 - rowq))
    xi = jnp.exp(lg * (rowq + 1.0))
    for n in range(n_chunks):
        sl = slice(n * c, (n + 1) * c)
        cos = cos_ref[sl, :]
        sin = sin_ref[sl, :]
        qf = _rotary(q_ref[sl, :].astype(F32), cos, sin)
        kf = _rotary(k_ref[sl, :].astype(F32) * (RET_QK_DIM ** -0.5), cos, sin)
        qr_ref[sl, :] = qf.astype(BF16)
        qx_ref[sl, :] = (qf * xi).astype(BF16)
        kr_ref[sl, :] = kf.astype(BF16)
        kz_ref[sl, :] = (kf * zeta).astype(BF16)
    vc_ref[...] = jnp.concatenate([v_ref[0], v_ref[1]], axis=-1)

    row = lax.broadcasted_iota(jnp.int32, (c, c), 0).astype(F32)
    col = lax.broadcasted_iota(jnp.int32, (c, c), 1).astype(F32)
    rel = row - col
    decay_mask = jnp.where(rel >= 0.0, jnp.exp(lg * jnp.maximum(rel, 0.0)), 0.0)
    chunk_decay = jnp.exp(jnp.zeros((1, RET_V_DIM), F32) + lg * c)

    state = jnp.zeros((RET_QK_DIM, RET_V_DIM), F32)
    for n in range(n_chunks):
        sl = slice(n * c, (n + 1) * c)
        vc = vc_ref[sl, :]
        scores = lax.dot_general(qr_ref[sl, :], kr_ref[sl, :], (((1,), (1,)), ((), ())),
                                 preferred_element_type=F32) * decay_mask
        o = jnp.dot(scores.astype(BF16), vc, preferred_element_type=F32)
        o = o + jnp.dot(qx_ref[sl, :], state.astype(BF16), preferred_element_type=F32)
        if n + 1 < n_chunks:
            kv = lax.dot_general(kz_ref[sl, :], vc, (((0,), (0,)), ((), ())),
                                 preferred_element_type=F32)
            state = state * chunk_decay + kv

        ms = jnp.mean(o * o, axis=-1, keepdims=True)
        o = o * lax.rsqrt(ms + NORM_EPS)
        z = jnp.concatenate([z_ref[0, sl, :], z_ref[1, sl, :]], axis=-1).astype(F32)
        o_ref[sl, :] = (o * _silu(z)).astype(o_ref.dtype)


def _retention(proj4, log_gamma, cos, sin_signed):
    _, b, s, _ = proj4.shape
    qk_spec = lambda base: pl.BlockSpec((None, None, s, LANES),
                                        lambda bi, h, base=base: (base + h, bi, 0, 0))
    pair_spec = lambda base: pl.BlockSpec((2, None, s, LANES),
                                          lambda bi, h, base=base: (base // 2 + h, bi, 0, 0))
    table_spec = pl.BlockSpec((s, LANES), lambda bi, h: (0, 0))
    return pl.pallas_call(
        _retention_kernel,
        out_shape=jax.ShapeDtypeStruct((RET_HEADS, b, s, RET_V_DIM), BF16),
        grid=(b, RET_HEADS),
        in_specs=[pl.BlockSpec(memory_space=pltpu.SMEM),
                  qk_spec(SLAB_RQ), qk_spec(SLAB_RK), pair_spec(SLAB_RV), pair_spec(SLAB_RZ),
                  table_spec, table_spec],
        out_specs=pl.BlockSpec((None, None, s, RET_V_DIM), lambda bi, h: (h, bi, 0, 0)),
        scratch_shapes=[pltpu.VMEM((s, RET_QK_DIM), BF16)] * 4
                       + [pltpu.VMEM((s, RET_V_DIM), BF16)],
        compiler_params=_params(2),
        name="retention",
    )(log_gamma, proj4, proj4, proj4, proj4, cos, sin_signed)


def _diff_attn_kernel(q_ref, k_ref, v_ref, z_ref, cos_ref, sin_ref,
                      lq1_ref, lk1_ref, lq2_ref, lk2_ref, subln_ref, o_ref,
                      qr_ref, kr_ref, vc_ref):
    t = ATT_BLOCK
    s_len = vc_ref.shape[0]
    cos = cos_ref[...]
    sin = sin_ref[...]
    q_scale = (DIFF_HEAD_DIM ** -0.5) * math.log2(math.e)
    for c in range(2):
        kr_ref[c] = _rotary(k_ref[c].astype(F32), cos, sin).astype(BF16)
        qr_ref[c] = _rotary(q_ref[c].astype(F32) * q_scale, cos, sin).astype(BF16)
    vc_ref[...] = jnp.concatenate([v_ref[0], v_ref[1]], axis=-1)

    lam = (jnp.exp(jnp.sum(lq1_ref[...] * lk1_ref[...], axis=-1, keepdims=True))
           - jnp.exp(jnp.sum(lq2_ref[...] * lk2_ref[...], axis=-1, keepdims=True))
           + LAMBDA_INIT)
    causal = (lax.broadcasted_iota(jnp.int32, (t, t), 1)
              <= lax.broadcasted_iota(jnp.int32, (t, t), 0))
    nt_dims = (((1,), (1,)), ((), ()))

    units = [(n, c) for n in reversed(range(s_len // t)) for c in range(2)]

    def scores(u):
        n, c = units[u]
        r0 = n * t
        q = qr_ref[c, r0:r0 + t, :]
        s_diag = jnp.where(causal,
                           lax.dot_general(q, kr_ref[c, r0:r0 + t, :], nt_dims,
                                           preferred_element_type=F32), -jnp.inf)
        m = jnp.max(s_diag, axis=-1, keepdims=True)
        s_off = None
        if n > 0:
            s_off = lax.dot_general(q, kr_ref[c, 0:r0, :], nt_dims, preferred_element_type=F32)
            m = jnp.maximum(m, jnp.max(s_off, axis=-1, keepdims=True))
        return s_diag, s_off, m

    def probs(sc):
        s_diag, s_off, m = sc
        p_diag = jnp.exp2(s_diag - m)
        l = jnp.sum(p_diag, axis=-1, keepdims=True)
        p_off = None
        if s_off is not None:
            p_off = jnp.exp2(s_off - m)
            l = l + jnp.sum(p_off, axis=-1, keepdims=True)
            p_off = p_off.astype(BF16)
        return p_diag.astype(BF16), p_off, l

    def weighted(u, pr):
        n, _ = units[u]
        r0 = n * t
        p_diag, p_off, l = pr
        acc = jnp.dot(p_diag, vc_ref[r0:r0 + t, :], preferred_element_type=F32)
        if p_off is not None:
            acc = acc + jnp.dot(p_off, vc_ref[0:r0, :], preferred_element_type=F32)
        return acc, l

    def finish(n, res):
        r0 = n * t
        o = res[0][0] * (1.0 / res[0][1]) - res[1][0] * (lam / res[1][1])
        ms = jnp.mean(o * o, axis=-1, keepdims=True)
        o = o * lax.rsqrt(ms + SUBLN_EPS) * subln_ref[...] * (1.0 - LAMBDA_INIT)
        z = jnp.concatenate([z_ref[0, r0:r0 + t, :], z_ref[1, r0:r0 + t, :]], axis=-1).astype(F32)
        o_ref[r0:r0 + t, :] = (o * _silu(z)).astype(o_ref.dtype)

    skew = ATT_SKEW
    sc, pr, res = {}, {}, {}
    for step in range(len(units) + 2 * skew):
        u_sc, u_pr, u_pv = step, step - skew, step - 2 * skew
        if 0 <= u_sc < len(units):
            sc[u_sc] = scores(u_sc)
        if 0 <= u_pr < len(units):
            pr[u_pr] = probs(sc.pop(u_pr))
        if 0 <= u_pv < len(units):
            res[u_pv] = weighted(u_pv, pr.pop(u_pv))
            if units[u_pv][1] == 1:
                finish(units[u_pv][0], (res.pop(u_pv - 1), res.pop(u_pv)))


def _diff_attention(proj4, cos, sin_signed, lq1, lk1, lq2, lk2, subln_w):
    _, b, s, _ = proj4.shape
    full = lambda base: pl.BlockSpec((2, None, s, LANES),
                                     lambda bi, h, base=base: (base // 2 + h, bi, 0, 0))
    table_spec = pl.BlockSpec((s, LANES), lambda bi, h: (0, 0))
    vec = lambda n: pl.BlockSpec((1, n), lambda bi, h: (0, 0))
    return pl.pallas_call(
        _diff_attn_kernel,
        out_shape=jax.ShapeDtypeStruct((DIFF_HEADS, b, s, DIFF_V_DIM), BF16),
        grid=(b, DIFF_HEADS),
        in_specs=[full(SLAB_DQ), full(SLAB_DK), full(SLAB_DV), full(SLAB_DZ),
                  table_spec, table_spec,
                  vec(DIFF_HEAD_DIM), vec(DIFF_HEAD_DIM), vec(DIFF_HEAD_DIM), vec(DIFF_HEAD_DIM),
                  vec(DIFF_V_DIM)],
        out_specs=pl.BlockSpec((None, None, s, DIFF_V_DIM), lambda bi, h: (h, bi, 0, 0)),
        scratch_shapes=[pltpu.VMEM((2, s, DIFF_HEAD_DIM), BF16),
                        pltpu.VMEM((2, s, DIFF_HEAD_DIM), BF16),
                        pltpu.VMEM((s, DIFF_V_DIM), BF16)],
        compiler_params=_params(2),
        name="diff_attention",
    )(proj4, proj4, proj4, proj4, cos, sin_signed, lq1, lk1, lq2, lk2, subln_w)


def _merge_kernel(a_ref, b_ref, w1_ref, w2_ref, g1_ref, g2_ref, o_ref, w1bf_ref, w2bf_ref):
    @pl.when(pl.program_id(1) == 0)
    def _():
        _cast_weight(w1_ref, w1bf_ref)
        _cast_weight(w2_ref, w2bf_ref)

    kw = a_ref.shape[2]

    def up(x_ref, wbf_ref):
        acc = None
        for k in range(x_ref.shape[0]):
            part = jnp.dot(x_ref[k], wbf_ref[k * kw:(k + 1) * kw, :], preferred_element_type=F32)
            acc = part if acc is None else acc + part
        return acc

    up1 = up(a_ref, w1bf_ref)
    up2 = up(b_ref, w2bf_ref)
    for j in range(g1_ref.shape[0]):
        sl = slice(j * LANES, (j + 1) * LANES)
        o_ref[:, sl] = (jax.nn.sigmoid(g1_ref[j].astype(F32)) * up1[:, sl]
                        + jax.nn.sigmoid(g2_ref[j].astype(F32)) * up2[:, sl]).astype(o_ref.dtype)


def _gated_merge(o_ret, o_diff, w_ret_up, w_diff_up, proj, tm=512, tn=512):
    nk, m, kw = o_ret.shape
    d = w_ret_up.shape[1]
    gs = tn // LANES
    x_spec = pl.BlockSpec((nk, tm, kw), lambda j, i: (0, i, 0))
    w_spec = pl.BlockSpec((nk * kw, tn), lambda j, i: (0, j))
    g_spec = lambda base: pl.BlockSpec((gs, tm, LANES),
                                       lambda j, i, base=base: (base // gs + j, i, 0))
    return pl.pallas_call(
        _merge_kernel,
        out_shape=jax.ShapeDtypeStruct((m, d), BF16),
        grid=(d // tn, m // tm),
        in_specs=[x_spec, x_spec, w_spec, w_spec, g_spec(SLAB_GR), g_spec(SLAB_GD)],
        out_specs=pl.BlockSpec((tm, tn), lambda j, i: (i, j)),
        scratch_shapes=[pltpu.VMEM((nk * kw, tn), BF16), pltpu.VMEM((nk * kw, tn), BF16)],
        compiler_params=_params(2),
        name="gated_merge",
    )(o_ret, o_diff, w_ret_up, w_diff_up, proj, proj)


def _outproj_kernel(mx_ref, w_ref, x_ref, fw_ref, o_ref, wbf_ref):
    @pl.when(pl.program_id(0) == 0)
    def _():
        _cast_weight(w_ref, wbf_ref)

    h = x_ref[...] + jnp.dot(mx_ref[...], wbf_ref[...], preferred_element_type=F32)
    ms = jnp.mean(h * h, axis=-1, keepdims=True)
    o_ref[...] = h * lax.rsqrt(ms + NORM_EPS) * fw_ref[...]


def _output_projection(mixed, w_out, x2, final_w, tm=256):
    m, d = x2.shape
    return pl.pallas_call(
        _outproj_kernel,
        out_shape=jax.ShapeDtypeStruct((m, d), F32),
        grid=(m // tm,),
        in_specs=[pl.BlockSpec((tm, d), lambda i: (i, 0)),
                  pl.BlockSpec((d, d), lambda i: (0, 0), pipeline_mode=pl.Buffered(1)),
                  pl.BlockSpec((tm, d), lambda i: (i, 0)),
                  pl.BlockSpec((1, d), lambda i: (0, 0))],
        out_specs=pl.BlockSpec((tm, d), lambda i: (i, 0)),
        scratch_shapes=[pltpu.VMEM((d, d), BF16)],
        compiler_params=_params(1),
        name="output_projection",
    )(mixed, w_out, x2, final_w)


def _rotary_tables(s, inv_freq):
    pos = jnp.arange(s, dtype=F32)
    ang = pos[:, None] * inv_freq[None, :]
    emb = jnp.concatenate([ang, ang], axis=-1)
    half = emb.shape[-1] // 2
    sign = jnp.concatenate([-jnp.ones((half,), F32), jnp.ones((half,), F32)])
    return jnp.cos(emb), jnp.sin(emb) * sign[None, :]


def kernel(x, norm_w, w_in, w_ret_up, w_diff_up, w_out, lambda_q1, lambda_k1, lambda_q2,
           lambda_k2, subln_w, final_norm_w):
    b, s, d = x.shape
    assert d == D_MODEL and w_in.shape == (1, D_MODEL, IN_WIDTH)
    x2 = x.reshape(b * s, d)

    ret_inv_freq = jnp.exp(-math.log(10000.0) * jnp.linspace(0.0, 1.0, RET_QK_DIM // 2, dtype=F32))
    rope_inv_freq = ROPE_THETA ** (-jnp.arange(0, DIFF_HEAD_DIM, 2, dtype=F32) / DIFF_HEAD_DIM)
    ret_cos, ret_sin = _rotary_tables(s, ret_inv_freq)
    rope_cos, rope_sin = _rotary_tables(s, rope_inv_freq)
    log_gamma = jnp.log1p(-jnp.exp2(-5.0 - jnp.arange(RET_HEADS, dtype=F32)))

    xn = _input_rmsnorm(x2, norm_w[0][None, :])
    proj = _input_projection(xn, w_in[0])
    proj4 = proj.reshape(N_SLABS, b, s, LANES)

    o_ret = _retention(proj4, log_gamma, ret_cos, ret_sin)
    o_diff = _diff_attention(proj4, rope_cos, rope_sin, lambda_q1, lambda_k1, lambda_q2,
                             lambda_k2, subln_w)
    o_ret = o_ret.reshape(RET_HEADS, b * s, RET_V_DIM)
    o_diff = o_diff.reshape(DIFF_HEADS, b * s, DIFF_V_DIM)

    mixed = _gated_merge(o_ret, o_diff, w_ret_up[0], w_diff_up[0], proj)
    out = _output_projection(mixed, w_out[0], x2, final_norm_w[None, :])
    return out.reshape(b, s, d)
```

```python
import functools
import math

import jax
import jax.numpy as jnp
from jax import lax
from jax.experimental import pallas as pl
from jax.experimental.pallas import tpu as pltpu

F32 = jnp.float32
BF16 = jnp.bfloat16

D_MODEL = 2048
RET_QK_DIM = 128
RET_V_DIM = 256
RET_HEADS = D_MODEL // RET_V_DIM
DIFF_HEAD_DIM = 128
DIFF_V_DIM = 2 * DIFF_HEAD_DIM
DIFF_HEADS = D_MODEL // DIFF_V_DIM
ROPE_THETA = 10000.0
NORM_EPS = 1e-6
SUBLN_EPS = 1e-5
LAMBDA_INIT = 0.8 - 0.6 * math.exp(-0.3 * 0)

RET_QK_WIDTH = RET_HEADS * RET_QK_DIM
RET_WIDTH = RET_HEADS * RET_V_DIM
DIFF_QK_WIDTH = DIFF_HEADS * 2 * DIFF_HEAD_DIM
DIFF_WIDTH = DIFF_HEADS * DIFF_V_DIM
IN_WIDTH = 2 * RET_QK_WIDTH + 2 * RET_WIDTH + 2 * DIFF_QK_WIDTH + 2 * DIFF_WIDTH + 2 * D_MODEL

LANES = 128
SLAB_RQ = 0
SLAB_RK = SLAB_RQ + RET_QK_WIDTH // LANES
SLAB_RV = SLAB_RK + RET_QK_WIDTH // LANES
SLAB_RZ = SLAB_RV + RET_WIDTH // LANES
SLAB_DQ = SLAB_RZ + RET_WIDTH // LANES
SLAB_DK = SLAB_DQ + DIFF_QK_WIDTH // LANES
SLAB_DV = SLAB_DK + DIFF_QK_WIDTH // LANES
SLAB_DZ = SLAB_DV + DIFF_WIDTH // LANES
SLAB_GR = SLAB_DZ + DIFF_WIDTH // LANES
SLAB_GD = SLAB_GR + D_MODEL // LANES
N_SLABS = IN_WIDTH // LANES

VMEM_LIMIT = 56 * 1024 * 1024

RET_CHUNK = 256
ATT_BLOCK = 256
ATT_SKEW = 1


def _params(n_axes, vmem=VMEM_LIMIT, flags=None):
    return pltpu.CompilerParams(dimension_semantics=("arbitrary",) * n_axes,
                                vmem_limit_bytes=vmem, flags=flags)


def _rmsnorm_kernel(x_ref, w_ref, o_ref):
    x = x_ref[...]
    ms = jnp.mean(x * x, axis=-1, keepdims=True)
    o_ref[...] = (x * lax.rsqrt(ms + NORM_EPS) * w_ref[...]).astype(o_ref.dtype)


def _input_rmsnorm(x2, w, tm=512):
    m, d = x2.shape
    return pl.pallas_call(
        _rmsnorm_kernel,
        out_shape=jax.ShapeDtypeStruct((m, d), BF16),
        grid=(m // tm,),
        in_specs=[pl.BlockSpec((tm, d), lambda i: (i, 0)),
                  pl.BlockSpec((1, d), lambda i: (0, 0))],
        out_specs=pl.BlockSpec((tm, d), lambda i: (i, 0)),
        compiler_params=_params(1),
        name="input_rmsnorm",
    )(x2, w)


def _cast_weight(w_ref, wbf_ref, rows=128):
    def body(i, c):
        r = pl.multiple_of(i * rows, rows)
        wbf_ref[pl.ds(r, rows), :] = w_ref[pl.ds(r, rows), :].astype(BF16)
        return c
    lax.fori_loop(0, w_ref.shape[0] // rows, body, 0)


def _rotary(x, cos, sin_signed):
    return x * cos + pltpu.roll(x, x.shape[-1] // 2, axis=x.ndim - 1) * sin_signed


def _sigmoid(g):
    return 0.5 * jnp.tanh(0.5 * g) + 0.5


def _silu(z):
    h = 0.5 * z
    return h * jnp.tanh(h) + h


_SECTIONS = ((RET_QK_WIDTH, "rotary", 0), (RET_QK_WIDTH, "rotary", 1),
             (RET_WIDTH, "plain", 0), (RET_WIDTH, "silu", 0),
             (DIFF_QK_WIDTH, "rotary", 2), (DIFF_QK_WIDTH, "rotary", 3),
             (DIFF_WIDTH, "plain", 0), (DIFF_WIDTH, "silu", 0),
             (D_MODEL, "sigmoid", 0), (D_MODEL, "sigmoid", 0))


def _tile_kinds(tn):
    kinds = []
    for width, kind, table in _SECTIONS:
        assert width % tn == 0
        kinds.extend([(kind, table)] * (width // tn))
    return kinds


def _any_of(j, tiles):
    return functools.reduce(jnp.logical_or, [j == t for t in tiles])


def _inproj_kernel(x_ref, w_ref, cos_ref, sin_ref, o_ref, *, kinds):
    epilogues = {
        "plain": lambda a: a,
        "rotary": lambda a: _rotary(a, cos_ref[...], sin_ref[...]),
        "silu": _silu,
        "sigmoid": _sigmoid,
    }
    j = pl.program_id(0)
    for kind, fn in epilogues.items():
        tiles = [t for t, (k, _) in enumerate(kinds) if k == kind]

        @pl.when(_any_of(j, tiles))
        def _(fn=fn):
            acc = jnp.dot(x_ref[...], w_ref[...].astype(BF16), preferred_element_type=F32)
            for s in range(o_ref.shape[0]):
                o_ref[s] = fn(acc[:, s * LANES:(s + 1) * LANES]).astype(o_ref.dtype)


def _input_projection(xn, w_in, cos_tables, sin_tables, tm=1024, tn=1024):
    m, d = xn.shape
    n = w_in.shape[1]
    s_len = cos_tables.shape[1]
    kinds = _tile_kinds(tn)

    def table_map(j, i):
        tab = sum(jnp.where(j == t, table, 0) for t, (_, table) in enumerate(kinds) if table)
        return (tab, i % (s_len // tm), 0)

    return pl.pallas_call(
        functools.partial(_inproj_kernel, kinds=kinds),
        out_shape=jax.ShapeDtypeStruct((n // LANES, m, LANES), BF16),
        grid=(n // tn, m // tm),
        in_specs=[pl.BlockSpec((tm, d), lambda j, i: (i, 0)),
                  pl.BlockSpec((d, tn), lambda j, i: (0, j)),
                  pl.BlockSpec((None, tm, LANES), table_map),
                  pl.BlockSpec((None, tm, LANES), table_map)],
        out_specs=pl.BlockSpec((tn // LANES, tm, LANES), lambda j, i: (j, i, 0)),
        compiler_params=_params(2),
        name="input_projection",
    )(xn, w_in, cos_tables, sin_tables)


def _retention_kernel(lg_ref, q_ref, k_ref, v_ref, z_ref, o_ref, qx_ref, kz_ref, vc_ref):
    s_len = q_ref.shape[0]
    c = RET_CHUNK
    n_chunks = s_len // c
    lg = lg_ref[pl.program_id(1)]

    rowq = lax.broadcasted_iota(jnp.int32, (c, RET_QK_DIM), 0).astype(F32)
    zeta = jnp.exp(lg * (c - 1.0 - rowq))
    xi = jnp.exp(lg * (rowq + 1.0))
    for n in range(n_chunks):
        sl = slice(n * c, (n + 1) * c)
        qx_ref[sl, :] = (q_ref[sl, :].astype(F32) * xi).astype(BF16)
        kz_ref[sl, :] = (k_ref[sl, :].astype(F32) * zeta).astype(BF16)
    vc_ref[...] = jnp.concatenate([v_ref[0], v_ref[1]], axis=-1)

    row = lax.broadcasted_iota(jnp.int32, (c, c), 0).astype(F32)
    col = lax.broadcasted_iota(jnp.int32, (c, c), 1).astype(F32)
    rel = row - col
    decay_mask = jnp.where(rel >= 0.0, jnp.exp(lg * jnp.maximum(rel, 0.0)), 0.0)
    chunk_decay = jnp.exp(jnp.zeros((1, RET_V_DIM), F32) + lg * c)

    state = jnp.zeros((RET_QK_DIM, RET_V_DIM), F32)
    for n in range(n_chunks):
        sl = slice(n * c, (n + 1) * c)
        vc = vc_ref[sl, :]
        scores = lax.dot_general(q_ref[sl, :], k_ref[sl, :], (((1,), (1,)), ((), ())),
                                 preferred_element_type=F32) * decay_mask
        o = jnp.dot(scores.astype(BF16), vc, preferred_element_type=F32)
        o = o + jnp.dot(qx_ref[sl, :], state.astype(BF16), preferred_element_type=F32)
        if n + 1 < n_chunks:
            kv = lax.dot_general(kz_ref[sl, :], vc, (((0,), (0,)), ((), ())),
                                 preferred_element_type=F32)
            state = state * chunk_decay + kv

        ms = jnp.mean(o * o, axis=-1, keepdims=True)
        o = o * lax.rsqrt(ms + NORM_EPS)
        silu_z = jnp.concatenate([z_ref[0, sl, :], z_ref[1, sl, :]], axis=-1).astype(F32)
        o_ref[sl, :] = (o * silu_z).astype(o_ref.dtype)


def _retention(proj4, log_gamma):
    _, b, s, _ = proj4.shape
    qk_spec = lambda base: pl.BlockSpec((None, None, s, LANES),
                                        lambda bi, h, base=base: (base + h, bi, 0, 0))
    pair_spec = lambda base: pl.BlockSpec((2, None, s, LANES),
                                          lambda bi, h, base=base: (base // 2 + h, bi, 0, 0))
    return pl.pallas_call(
        _retention_kernel,
        out_shape=jax.ShapeDtypeStruct((RET_HEADS, b, s, RET_V_DIM), BF16),
        grid=(b, RET_HEADS),
        in_specs=[pl.BlockSpec(memory_space=pltpu.SMEM),
                  qk_spec(SLAB_RQ), qk_spec(SLAB_RK), pair_spec(SLAB_RV), pair_spec(SLAB_RZ)],
        out_specs=pl.BlockSpec((None, None, s, RET_V_DIM), lambda bi, h: (h, bi, 0, 0)),
        scratch_shapes=[pltpu.VMEM((s, RET_QK_DIM), BF16)] * 2
                       + [pltpu.VMEM((s, RET_V_DIM), BF16)],
        compiler_params=_params(2),
        name="retention",
    )(log_gamma, proj4, proj4, proj4, proj4)


def _diff_attn_kernel(q_ref, k_ref, v_ref, z_ref,
                      lq1_ref, lk1_ref, lq2_ref, lk2_ref, subln_ref, o_ref, vc_ref):
    t = ATT_BLOCK
    s_len = vc_ref.shape[0]
    vc_ref[...] = jnp.concatenate([v_ref[0], v_ref[1]], axis=-1)

    lam = (jnp.exp(jnp.sum(lq1_ref[...] * lk1_ref[...], axis=-1, keepdims=True))
           - jnp.exp(jnp.sum(lq2_ref[...] * lk2_ref[...], axis=-1, keepdims=True))
           + LAMBDA_INIT)
    causal = (lax.broadcasted_iota(jnp.int32, (t, t), 1)
              <= lax.broadcasted_iota(jnp.int32, (t, t), 0))
    nt_dims = (((1,), (1,)), ((), ()))

    units = [(n, c) for n in reversed(range(s_len // t)) for c in range(2)]

    def scores(u):
        n, c = units[u]
        r0 = n * t
        q = q_ref[c, r0:r0 + t, :]
        s_diag = jnp.where(causal,
                           lax.dot_general(q, k_ref[c, r0:r0 + t, :], nt_dims,
                                           preferred_element_type=F32), -jnp.inf)
        m = jnp.max(s_diag, axis=-1, keepdims=True)
        s_off = None
        if n > 0:
            s_off = lax.dot_general(q, k_ref[c, 0:r0, :], nt_dims, preferred_element_type=F32)
            m = jnp.maximum(m, jnp.max(s_off, axis=-1, keepdims=True))
        return s_diag, s_off, m

    def probs(sc):
        s_diag, s_off, m = sc
        p_diag = jnp.exp2(s_diag - m)
        l = jnp.sum(p_diag, axis=-1, keepdims=True)
        p_off = None
        if s_off is not None:
            p_off = jnp.exp2(s_off - m)
            l = l + jnp.sum(p_off, axis=-1, keepdims=True)
            p_off = p_off.astype(BF16)
        return p_diag.astype(BF16), p_off, l

    def weighted(u, pr):
        n, _ = units[u]
        r0 = n * t
        p_diag, p_off, l = pr
        acc = jnp.dot(p_diag, vc_ref[r0:r0 + t, :], preferred_element_type=F32)
        if p_off is not None:
            acc = acc + jnp.dot(p_off, vc_ref[0:r0, :], preferred_element_type=F32)
        return acc, l

    def finish(n, res):
        r0 = n * t
        o = res[0][0] * (1.0 / res[0][1]) - res[1][0] * (lam / res[1][1])
        ms = jnp.mean(o * o, axis=-1, keepdims=True)
        o = o * lax.rsqrt(ms + SUBLN_EPS) * subln_ref[...] * (1.0 - LAMBDA_INIT)
        silu_z = jnp.concatenate([z_ref[0, r0:r0 + t, :], z_ref[1, r0:r0 + t, :]],
                                 axis=-1).astype(F32)
        o_ref[r0:r0 + t, :] = (o * silu_z).astype(o_ref.dtype)

    skew = ATT_SKEW
    sc, pr, res = {}, {}, {}
    for step in range(len(units) + 2 * skew):
        u_sc, u_pr, u_pv = step, step - skew, step - 2 * skew
        if 0 <= u_sc < len(units):
            sc[u_sc] = scores(u_sc)
        if 0 <= u_pr < len(units):
            pr[u_pr] = probs(sc.pop(u_pr))
        if 0 <= u_pv < len(units):
            res[u_pv] = weighted(u_pv, pr.pop(u_pv))
            if units[u_pv][1] == 1:
                finish(units[u_pv][0], (res.pop(u_pv - 1), res.pop(u_pv)))


def _diff_attention(proj4, lq1, lk1, lq2, lk2, subln_w):
    _, b, s, _ = proj4.shape
    full = lambda base: pl.BlockSpec((2, None, s, LANES),
                                     lambda bi, h, base=base: (base // 2 + h, bi, 0, 0))
    vec = lambda n: pl.BlockSpec((1, n), lambda bi, h: (0, 0))
    return pl.pallas_call(
        _diff_attn_kernel,
        out_shape=jax.ShapeDtypeStruct((DIFF_HEADS, b, s, DIFF_V_DIM), BF16),
        grid=(b, DIFF_HEADS),
        in_specs=[full(SLAB_DQ), full(SLAB_DK), full(SLAB_DV), full(SLAB_DZ),
                  vec(DIFF_HEAD_DIM), vec(DIFF_HEAD_DIM), vec(DIFF_HEAD_DIM), vec(DIFF_HEAD_DIM),
                  vec(DIFF_V_DIM)],
        out_specs=pl.BlockSpec((None, None, s, DIFF_V_DIM), lambda bi, h: (h, bi, 0, 0)),
        scratch_shapes=[pltpu.VMEM((s, DIFF_V_DIM), BF16)],
        compiler_params=_params(2),
        name="diff_attention",
    )(proj4, proj4, proj4, proj4, lq1, lk1, lq2, lk2, subln_w)


def _merge_kernel(a_ref, b_ref, w1_ref, w2_ref, g1_ref, g2_ref, o_ref, w1bf_ref, w2bf_ref):
    @pl.when(pl.program_id(1) == 0)
    def _():
        _cast_weight(w1_ref, w1bf_ref)
        _cast_weight(w2_ref, w2bf_ref)

    kw = a_ref.shape[2]

    def up(x_ref, wbf_ref):
        acc = None
        for k in range(x_ref.shape[0]):
            part = jnp.dot(x_ref[k], wbf_ref[k * kw:(k + 1) * kw, :], preferred_element_type=F32)
            acc = part if acc is None else acc + part
        return acc

    up1 = up(a_ref, w1bf_ref)
    up2 = up(b_ref, w2bf_ref)
    for j in range(g1_ref.shape[0]):
        sl = slice(j * LANES, (j + 1) * LANES)
        o_ref[:, sl] = (g1_ref[j].astype(F32) * up1[:, sl]
                        + g2_ref[j].astype(F32) * up2[:, sl]).astype(o_ref.dtype)


def _gated_merge(o_ret, o_diff, w_ret_up, w_diff_up, proj, tm=512, tn=512):
    nk, m, kw = o_ret.shape
    d = w_ret_up.shape[1]
    gs = tn // LANES
    x_spec = pl.BlockSpec((nk, tm, kw), lambda j, i: (0, i, 0))
    w_spec = pl.BlockSpec((nk * kw, tn), lambda j, i: (0, j))
    g_spec = lambda base: pl.BlockSpec((gs, tm, LANES),
                                       lambda j, i, base=base: (base // gs + j, i, 0))
    return pl.pallas_call(
        _merge_kernel,
        out_shape=jax.ShapeDtypeStruct((m, d), BF16),
        grid=(d // tn, m // tm),
        in_specs=[x_spec, x_spec, w_spec, w_spec, g_spec(SLAB_GR), g_spec(SLAB_GD)],
        out_specs=pl.BlockSpec((tm, tn), lambda j, i: (i, j)),
        scratch_shapes=[pltpu.VMEM((nk * kw, tn), BF16), pltpu.VMEM((nk * kw, tn), BF16)],
        compiler_params=_params(2),
        name="gated_merge",
    )(o_ret, o_diff, w_ret_up, w_diff_up, proj, proj)


def _outproj_kernel(mx_ref, w_ref, x_ref, fw_ref, o_ref, wbf_ref):
    @pl.when(pl.program_id(0) == 0)
    def _():
        _cast_weight(w_ref, wbf_ref)

    h = x_ref[...] + jnp.dot(mx_ref[...], wbf_ref[...], preferred_element_type=F32)
    ms = jnp.mean(h * h, axis=-1, keepdims=True)
    o_ref[...] = h * lax.rsqrt(ms + NORM_EPS) * fw_ref[...]


def _output_projection(mixed, w_out, x2, final_w, tm=256):
    m, d = x2.shape
    return pl.pallas_call(
        _outproj_kernel,
        out_shape=jax.ShapeDtypeStruct((m, d), F32),
        grid=(m // tm,),
        in_specs=[pl.BlockSpec((tm, d), lambda i: (i, 0)),
                  pl.BlockSpec((d, d), lambda i: (0, 0), pipeline_mode=pl.Buffered(1)),
                  pl.BlockSpec((tm, d), lambda i: (i, 0)),
                  pl.BlockSpec((1, d), lambda i: (0, 0))],
        out_specs=pl.BlockSpec((tm, d), lambda i: (i, 0)),
        scratch_shapes=[pltpu.VMEM((d, d), BF16)],
        compiler_params=_params(1),
        name="output_projection",
    )(mixed, w_out, x2, final_w)


def _rotary_tables(s, inv_freq):
    pos = jnp.arange(s, dtype=F32)
    ang = pos[:, None] * inv_freq[None, :]
    emb = jnp.concatenate([ang, ang], axis=-1)
    half = emb.shape[-1] // 2
    sign = jnp.concatenate([-jnp.ones((half,), F32), jnp.ones((half,), F32)])
    return jnp.cos(emb), jnp.sin(emb) * sign[None, :]


def kernel(x, norm_w, w_in, w_ret_up, w_diff_up, w_out, lambda_q1, lambda_k1, lambda_q2,
           lambda_k2, subln_w, final_norm_w):
    b, s, d = x.shape
    assert d == D_MODEL and w_in.shape == (1, D_MODEL, IN_WIDTH)
    x2 = x.reshape(b * s, d)

    ret_inv_freq = jnp.exp(-math.log(10000.0) * jnp.linspace(0.0, 1.0, RET_QK_DIM // 2, dtype=F32))
    rope_inv_freq = ROPE_THETA ** (-jnp.arange(0, DIFF_HEAD_DIM, 2, dtype=F32) / DIFF_HEAD_DIM)
    ret_cos, ret_sin = _rotary_tables(s, ret_inv_freq)
    rope_cos, rope_sin = _rotary_tables(s, rope_inv_freq)
    log_gamma = jnp.log1p(-jnp.exp2(-5.0 - jnp.arange(RET_HEADS, dtype=F32)))

    ret_k_scale = RET_QK_DIM ** -0.5
    diff_q_scale = (DIFF_HEAD_DIM ** -0.5) * math.log2(math.e)
    cos_tables = jnp.stack([ret_cos, ret_cos * ret_k_scale, rope_cos * diff_q_scale, rope_cos])
    sin_tables = jnp.stack([ret_sin, ret_sin * ret_k_scale, rope_sin * diff_q_scale, rope_sin])

    xn = _input_rmsnorm(x2, norm_w[0][None, :])
    proj = _input_projection(xn, w_in[0], cos_tables, sin_tables)
    proj4 = proj.reshape(N_SLABS, b, s, LANES)

    o_ret = _retention(proj4, log_gamma)
    o_diff = _diff_attention(proj4, lambda_q1, lambda_k1, lambda_q2, lambda_k2, subln_w)
    o_ret = o_ret.reshape(RET_HEADS, b * s, RET_V_DIM)
    o_diff = o_diff.reshape(DIFF_HEADS, b * s, DIFF_V_DIM)

    mixed = _gated_merge(o_ret, o_diff, w_ret_up[0], w_diff_up[0], proj)
    out = _output_projection(mixed, w_out[0], x2, final_norm_w[None, :])
    return out.reshape(b, s, d)
```

```python
import functools
import math

import jax
import jax.numpy as jnp
from jax import lax
from jax.experimental import pallas as pl
from jax.experimental.pallas import tpu as pltpu

F32 = jnp.float32
BF16 = jnp.bfloat16

D_MODEL = 2048
RET_QK_DIM = 128
RET_V_DIM = 256
RET_HEADS = D_MODEL // RET_V_DIM
DIFF_HEAD_DIM = 128
DIFF_V_DIM = 2 * DIFF_HEAD_DIM
DIFF_HEADS = D_MODEL // DIFF_V_DIM
ROPE_THETA = 10000.0
NORM_EPS = 1e-6
SUBLN_EPS = 1e-5
LAMBDA_INIT = 0.8 - 0.6 * math.exp(-0.3 * 0)

RET_QK_WIDTH = RET_HEADS * RET_QK_DIM
RET_WIDTH = RET_HEADS * RET_V_DIM
DIFF_QK_WIDTH = DIFF_HEADS * 2 * DIFF_HEAD_DIM
DIFF_WIDTH = DIFF_HEADS * DIFF_V_DIM
IN_WIDTH = 2 * RET_QK_WIDTH + 2 * RET_WIDTH + 2 * DIFF_QK_WIDTH + 2 * DIFF_WIDTH + 2 * D_MODEL

LANES = 128
SLAB_RQ = 0
SLAB_RK = SLAB_RQ + RET_QK_WIDTH // LANES
SLAB_RV = SLAB_RK + RET_QK_WIDTH // LANES
SLAB_RZ = SLAB_RV + RET_WIDTH // LANES
SLAB_DQ = SLAB_RZ + RET_WIDTH // LANES
SLAB_DK = SLAB_DQ + DIFF_QK_WIDTH // LANES
SLAB_DV = SLAB_DK + DIFF_QK_WIDTH // LANES
SLAB_DZ = SLAB_DV + DIFF_WIDTH // LANES
SLAB_GR = SLAB_DZ + DIFF_WIDTH // LANES
SLAB_GD = SLAB_GR + D_MODEL // LANES
N_SLABS = IN_WIDTH // LANES

VMEM_LIMIT = 56 * 1024 * 1024

RET_CHUNK = 256
ATT_BLOCK = 256
ATT_SKEW = 1


def _params(n_axes, vmem=VMEM_LIMIT, flags=None):
    return pltpu.CompilerParams(dimension_semantics=("arbitrary",) * n_axes,
                                vmem_limit_bytes=vmem, flags=flags)


def _rmsnorm_kernel(x_ref, w_ref, o_ref):
    x = x_ref[...]
    ms = jnp.mean(x * x, axis=-1, keepdims=True)
    o_ref[...] = (x * lax.rsqrt(ms + NORM_EPS) * w_ref[...]).astype(o_ref.dtype)


def _input_rmsnorm(x2, w, tm=512):
    m, d = x2.shape
    return pl.pallas_call(
        _rmsnorm_kernel,
        out_shape=jax.ShapeDtypeStruct((m, d), BF16),
        grid=(m // tm,),
        in_specs=[pl.BlockSpec((tm, d), lambda i: (i, 0)),
                  pl.BlockSpec((1, d), lambda i: (0, 0))],
        out_specs=pl.BlockSpec((tm, d), lambda i: (i, 0)),
        compiler_params=_params(1),
        name="input_rmsnorm",
    )(x2, w)


def _cast_weight(w_ref, wbf_ref, rows=128):
    def body(i, c):
        r = pl.multiple_of(i * rows, rows)
        wbf_ref[pl.ds(r, rows), :] = w_ref[pl.ds(r, rows), :].astype(BF16)
        return c
    lax.fori_loop(0, w_ref.shape[0] // rows, body, 0)


def _rotary(x, cos, sin_signed):
    return x * cos + pltpu.roll(x, x.shape[-1] // 2, axis=x.ndim - 1) * sin_signed


def _sigmoid(g):
    return 0.5 * jnp.tanh(0.5 * g) + 0.5


def _silu(z):
    h = 0.5 * z
    return h * jnp.tanh(h) + h


_SECTIONS = ((RET_QK_WIDTH, "rotary", 0), (RET_QK_WIDTH, "rotary", 1),
             (RET_WIDTH, "plain", 0), (RET_WIDTH, "silu", 0),
             (DIFF_QK_WIDTH, "rotary", 2), (DIFF_QK_WIDTH, "rotary", 3),
             (DIFF_WIDTH, "plain", 0), (DIFF_WIDTH, "silu", 0),
             (D_MODEL, "sigmoid", 0), (D_MODEL, "sigmoid", 0))


def _tile_kinds(tn):
    kinds = []
    for width, kind, table in _SECTIONS:
        assert width % tn == 0
        kinds.extend([(kind, table)] * (width // tn))
    return kinds


def _any_of(j, tiles):
    return functools.reduce(jnp.logical_or, [j == t for t in tiles])


def _inproj_kernel(x_ref, w_ref, cos_ref, sin_ref, o_ref, wbf_ref, *, kinds):
    @pl.when(pl.program_id(1) == 0)
    def _():
        _cast_weight(w_ref, wbf_ref)

    epilogues = {
        "plain": lambda a: a,
        "rotary": lambda a: _rotary(a, cos_ref[...], sin_ref[...]),
        "silu": _silu,
        "sigmoid": _sigmoid,
    }
    j = pl.program_id(0)
    for kind, fn in epilogues.items():
        tiles = [t for t, (k, _) in enumerate(kinds) if k == kind]

        @pl.when(_any_of(j, tiles))
        def _(fn=fn):
            acc = jnp.dot(x_ref[...], wbf_ref[...], preferred_element_type=F32)
            for s in range(o_ref.shape[0]):
                o_ref[s] = fn(acc[:, s * LANES:(s + 1) * LANES]).astype(o_ref.dtype)


def _input_projection(xn, w_in, cos_tables, sin_tables, tm=1024, tn=1024):
    m, d = xn.shape
    n = w_in.shape[1]
    s_len = cos_tables.shape[1]
    kinds = _tile_kinds(tn)

    def table_map(j, i):
        tab = sum(jnp.where(j == t, table, 0) for t, (_, table) in enumerate(kinds) if table)
        return (tab, i % (s_len // tm), 0)

    return pl.pallas_call(
        functools.partial(_inproj_kernel, kinds=kinds),
        out_shape=jax.ShapeDtypeStruct((n // LANES, m, LANES), BF16),
        grid=(n // tn, m // tm),
        in_specs=[pl.BlockSpec((tm, d), lambda j, i: (i, 0)),
                  pl.BlockSpec((d, tn), lambda j, i: (0, j)),
                  pl.BlockSpec((None, tm, LANES), table_map),
                  pl.BlockSpec((None, tm, LANES), table_map)],
        out_specs=pl.BlockSpec((tn // LANES, tm, LANES), lambda j, i: (j, i, 0)),
        scratch_shapes=[pltpu.VMEM((d, tn), BF16)],
        compiler_params=_params(2),
        name="input_projection",
    )(xn, w_in, cos_tables, sin_tables)


def _retention_kernel(lg_ref, q_ref, k_ref, v_ref, z_ref, o_ref, qx_ref, kz_ref, vc_ref):
    s_len = q_ref.shape[0]
    c = RET_CHUNK
    n_chunks = s_len // c
    lg = lg_ref[pl.program_id(1)]

    rowq = lax.broadcasted_iota(jnp.int32, (c, RET_QK_DIM), 0).astype(F32)
    zeta = jnp.exp(lg * (c - 1.0 - rowq))
    xi = jnp.exp(lg * (rowq + 1.0))
    for n in range(n_chunks):
        sl = slice(n * c, (n + 1) * c)
        qx_ref[sl, :] = (q_ref[sl, :].astype(F32) * xi).astype(BF16)
        kz_ref[sl, :] = (k_ref[sl, :].astype(F32) * zeta).astype(BF16)
    vc_ref[...] = jnp.concatenate([v_ref[0], v_ref[1]], axis=-1)

    row = lax.broadcasted_iota(jnp.int32, (c, c), 0).astype(F32)
    col = lax.broadcasted_iota(jnp.int32, (c, c), 1).astype(F32)
    rel = row - col
    decay_mask = jnp.where(rel >= 0.0, jnp.exp(lg * jnp.maximum(rel, 0.0)), 0.0)
    chunk_decay = jnp.exp(jnp.zeros((1, RET_V_DIM), F32) + lg * c)

    state = jnp.zeros((RET_QK_DIM, RET_V_DIM), F32)
    for n in range(n_chunks):
        sl = slice(n * c, (n + 1) * c)
        vc = vc_ref[sl, :]
        scores = lax.dot_general(q_ref[sl, :], k_ref[sl, :], (((1,), (1,)), ((), ())),
                                 preferred_element_type=F32) * decay_mask
        o = jnp.dot(scores.astype(BF16), vc, preferred_element_type=F32)
        o = o + jnp.dot(qx_ref[sl, :], state.astype(BF16), preferred_element_type=F32)
        if n + 1 < n_chunks:
            kv = lax.dot_general(kz_ref[sl, :], vc, (((0,), (0,)), ((), ())),
                                 preferred_element_type=F32)
            state = state * chunk_decay + kv

        ms = jnp.mean(o * o, axis=-1, keepdims=True)
        o = o * lax.rsqrt(ms + NORM_EPS)
        silu_z = jnp.concatenate([z_ref[0, sl, :], z_ref[1, sl, :]], axis=-1).astype(F32)
        o_ref[sl, :] = (o * silu_z).astype(o_ref.dtype)


def _retention(proj4, log_gamma):
    _, b, s, _ = proj4.shape
    qk_spec = lambda base: pl.BlockSpec((None, None, s, LANES),
                                        lambda bi, h, base=base: (base + h, bi, 0, 0))
    pair_spec = lambda base: pl.BlockSpec((2, None, s, LANES),
                                          lambda bi, h, base=base: (base // 2 + h, bi, 0, 0))
    return pl.pallas_call(
        _retention_kernel,
        out_shape=jax.ShapeDtypeStruct((RET_HEADS, b, s, RET_V_DIM), BF16),
        grid=(b, RET_HEADS),
        in_specs=[pl.BlockSpec(memory_space=pltpu.SMEM),
                  qk_spec(SLAB_RQ), qk_spec(SLAB_RK), pair_spec(SLAB_RV), pair_spec(SLAB_RZ)],
        out_specs=pl.BlockSpec((None, None, s, RET_V_DIM), lambda bi, h: (h, bi, 0, 0)),
        scratch_shapes=[pltpu.VMEM((s, RET_QK_DIM), BF16)] * 2
                       + [pltpu.VMEM((s, RET_V_DIM), BF16)],
        compiler_params=_params(2),
        name="retention",
    )(log_gamma, proj4, proj4, proj4, proj4)


def _diff_attn_kernel(q_ref, k_ref, v_ref, z_ref,
                      lq1_ref, lk1_ref, lq2_ref, lk2_ref, subln_ref, o_ref, vc_ref):
    t = ATT_BLOCK
    s_len = vc_ref.shape[0]
    vc_ref[...] = jnp.concatenate([v_ref[0], v_ref[1]], axis=-1)

    lam = (jnp.exp(jnp.sum(lq1_ref[...] * lk1_ref[...], axis=-1, keepdims=True))
           - jnp.exp(jnp.sum(lq2_ref[...] * lk2_ref[...], axis=-1, keepdims=True))
           + LAMBDA_INIT)
    causal = (lax.broadcasted_iota(jnp.int32, (t, t), 1)
              <= lax.broadcasted_iota(jnp.int32, (t, t), 0))
    nt_dims = (((1,), (1,)), ((), ()))

    units = [(n, c) for n in reversed(range(s_len // t)) for c in range(2)]

    def scores(u):
        n, c = units[u]
        r0 = n * t
        q = q_ref[c, r0:r0 + t, :]
        s_diag = jnp.where(causal,
                           lax.dot_general(q, k_ref[c, r0:r0 + t, :], nt_dims,
                                           preferred_element_type=F32), -jnp.inf)
        m = jnp.max(s_diag, axis=-1, keepdims=True)
        s_off = None
        if n > 0:
            s_off = lax.dot_general(q, k_ref[c, 0:r0, :], nt_dims, preferred_element_type=F32)
            m = jnp.maximum(m, jnp.max(s_off, axis=-1, keepdims=True))
        return s_diag, s_off, m

    def probs(sc):
        s_diag, s_off, m = sc
        p_diag = jnp.exp2(s_diag - m)
        l = jnp.sum(p_diag, axis=-1, keepdims=True)
        p_off = None
        if s_off is not None:
            p_off = jnp.exp2(s_off - m)
            l = l + jnp.sum(p_off, axis=-1, keepdims=True)
            p_off = p_off.astype(BF16)
        return p_diag.astype(BF16), p_off, l

    def weighted(u, pr):
        n, _ = units[u]
        r0 = n * t
        p_diag, p_off, l = pr
        acc = jnp.dot(p_diag, vc_ref[r0:r0 + t, :], preferred_element_type=F32)
        if p_off is not None:
            acc = acc + jnp.dot(p_off, vc_ref[0:r0, :], preferred_element_type=F32)
        return acc, l

    def finish(n, res):
        r0 = n * t
        o = res[0][0] * (1.0 / res[0][1]) - res[1][0] * (lam / res[1][1])
        ms = jnp.mean(o * o, axis=-1, keepdims=True)
        o = o * lax.rsqrt(ms + SUBLN_EPS) * subln_ref[...] * (1.0 - LAMBDA_INIT)
        silu_z = jnp.concatenate([z_ref[0, r0:r0 + t, :], z_ref[1, r0:r0 + t, :]],
                                 axis=-1).astype(F32)
        o_ref[r0:r0 + t, :] = (o * silu_z).astype(o_ref.dtype)

    skew = ATT_SKEW
    sc, pr, res = {}, {}, {}
    for step in range(len(units) + 2 * skew):
        u_sc, u_pr, u_pv = step, step - skew, step - 2 * skew
        if 0 <= u_sc < len(units):
            sc[u_sc] = scores(u_sc)
        if 0 <= u_pr < len(units):
            pr[u_pr] = probs(sc.pop(u_pr))
        if 0 <= u_pv < len(units):
            res[u_pv] = weighted(u_pv, pr.pop(u_pv))
            if units[u_pv][1] == 1:
                finish(units[u_pv][0], (res.pop(u_pv - 1), res.pop(u_pv)))


def _diff_attention(proj4, lq1, lk1, lq2, lk2, subln_w):
    _, b, s, _ = proj4.shape
    full = lambda base: pl.BlockSpec((2, None, s, LANES),
                                     lambda bi, h, base=base: (base // 2 + h, bi, 0, 0))
    vec = lambda n: pl.BlockSpec((1, n), lambda bi, h: (0, 0))
    return pl.pallas_call(
        _diff_attn_kernel,
        out_shape=jax.ShapeDtypeStruct((DIFF_HEADS, b, s, DIFF_V_DIM), BF16),
        grid=(b, DIFF_HEADS),
        in_specs=[full(SLAB_DQ), full(SLAB_DK), full(SLAB_DV), full(SLAB_DZ),
                  vec(DIFF_HEAD_DIM), vec(DIFF_HEAD_DIM), vec(DIFF_HEAD_DIM), vec(DIFF_HEAD_DIM),
                  vec(DIFF_V_DIM)],
        out_specs=pl.BlockSpec((None, None, s, DIFF_V_DIM), lambda bi, h: (h, bi, 0, 0)),
        scratch_shapes=[pltpu.VMEM((s, DIFF_V_DIM), BF16)],
        compiler_params=_params(2),
        name="diff_attention",
    )(proj4, proj4, proj4, proj4, lq1, lk1, lq2, lk2, subln_w)


def _merge_kernel(a_ref, b_ref, w1_ref, w2_ref, g1_ref, g2_ref, o_ref, w1bf_ref, w2bf_ref):
    @pl.when(pl.program_id(1) == 0)
    def _():
        _cast_weight(w1_ref, w1bf_ref)
        _cast_weight(w2_ref, w2bf_ref)

    kw = a_ref.shape[2]

    def up(x_ref, wbf_ref):
        acc = None
        for k in range(x_ref.shape[0]):
            part = jnp.dot(x_ref[k], wbf_ref[k * kw:(k + 1) * kw, :], preferred_element_type=F32)
            acc = part if acc is None else acc + part
        return acc

    up1 = up(a_ref, w1bf_ref)
    up2 = up(b_ref, w2bf_ref)
    for j in range(g1_ref.shape[0]):
        sl = slice(j * LANES, (j + 1) * LANES)
        o_ref[:, sl] = (g1_ref[j].astype(F32) * up1[:, sl]
                        + g2_ref[j].astype(F32) * up2[:, sl]).astype(o_ref.dtype)


def _gated_merge(o_ret, o_diff, w_ret_up, w_diff_up, proj, tm=1024, tn=512):
    nk, m, kw = o_ret.shape
    d = w_ret_up.shape[1]
    gs = tn // LANES
    x_spec = pl.BlockSpec((nk, tm, kw), lambda j, i: (0, i, 0))
    w_spec = pl.BlockSpec((nk * kw, tn), lambda j, i: (0, j))
    g_spec = lambda base: pl.BlockSpec((gs, tm, LANES),
                                       lambda j, i, base=base: (base // gs + j, i, 0))
    return pl.pallas_call(
        _merge_kernel,
        out_shape=jax.ShapeDtypeStruct((m, d), BF16),
        grid=(d // tn, m // tm),
        in_specs=[x_spec, x_spec, w_spec, w_spec, g_spec(SLAB_GR), g_spec(SLAB_GD)],
        out_specs=pl.BlockSpec((tm, tn), lambda j, i: (i, j)),
        scratch_shapes=[pltpu.VMEM((nk * kw, tn), BF16), pltpu.VMEM((nk * kw, tn), BF16)],
        compiler_params=_params(2),
        name="gated_merge",
    )(o_ret, o_diff, w_ret_up, w_diff_up, proj, proj)


def _outproj_kernel(mx_ref, w_ref, x_ref, fw_ref, o_ref, wbf_ref):
    @pl.when(pl.program_id(0) == 0)
    def _():
        _cast_weight(w_ref, wbf_ref)

    h = x_ref[...] + jnp.dot(mx_ref[...], wbf_ref[...], preferred_element_type=F32)
    ms = jnp.mean(h * h, axis=-1, keepdims=True)
    o_ref[...] = h * lax.rsqrt(ms + NORM_EPS) * fw_ref[...]


def _output_projection(mixed, w_out, x2, final_w, tm=512):
    m, d = x2.shape
    return pl.pallas_call(
        _outproj_kernel,
        out_shape=jax.ShapeDtypeStruct((m, d), F32),
        grid=(m // tm,),
        in_specs=[pl.BlockSpec((tm, d), lambda i: (i, 0)),
                  pl.BlockSpec((d, d), lambda i: (0, 0), pipeline_mode=pl.Buffered(1)),
                  pl.BlockSpec((tm, d), lambda i: (i, 0)),
                  pl.BlockSpec((1, d), lambda i: (0, 0))],
        out_specs=pl.BlockSpec((tm, d), lambda i: (i, 0)),
        scratch_shapes=[pltpu.VMEM((d, d), BF16)],
        compiler_params=_params(1),
        name="output_projection",
    )(mixed, w_out, x2, final_w)


def _rotary_tables(s, inv_freq):
    pos = jnp.arange(s, dtype=F32)
    ang = pos[:, None] * inv_freq[None, :]
    emb = jnp.concatenate([ang, ang], axis=-1)
    half = emb.shape[-1] // 2
    sign = jnp.concatenate([-jnp.ones((half,), F32), jnp.ones((half,), F32)])
    return jnp.cos(emb), jnp.sin(emb) * sign[None, :]


def kernel(x, norm_w, w_in, w_ret_up, w_diff_up, w_out, lambda_q1, lambda_k1, lambda_q2,
           lambda_k2, subln_w, final_norm_w):
    b, s, d = x.shape
    assert d == D_MODEL and w_in.shape == (1, D_MODEL, IN_WIDTH)
    x2 = x.reshape(b * s, d)

    ret_inv_freq = jnp.exp(-math.log(10000.0) * jnp.linspace(0.0, 1.0, RET_QK_DIM // 2, dtype=F32))
    rope_inv_freq = ROPE_THETA ** (-jnp.arange(0, DIFF_HEAD_DIM, 2, dtype=F32) / DIFF_HEAD_DIM)
    ret_cos, ret_sin = _rotary_tables(s, ret_inv_freq)
    rope_cos, rope_sin = _rotary_tables(s, rope_inv_freq)
    log_gamma = jnp.log1p(-jnp.exp2(-5.0 - jnp.arange(RET_HEADS, dtype=F32)))

    ret_k_scale = RET_QK_DIM ** -0.5
    diff_q_scale = (DIFF_HEAD_DIM ** -0.5) * math.log2(math.e)
    cos_tables = jnp.stack([ret_cos, ret_cos * ret_k_scale, rope_cos * diff_q_scale, rope_cos])
    sin_tables = jnp.stack([ret_sin, ret_sin * ret_k_scale, rope_sin * diff_q_scale, rope_sin])

    xn = _input_rmsnorm(x2, norm_w[0][None, :])
    proj = _input_projection(xn, w_in[0], cos_tables, sin_tables)
    proj4 = proj.reshape(N_SLABS, b, s, LANES)

    o_ret = _retention(proj4, log_gamma)
    o_diff = _diff_attention(proj4, lambda_q1, lambda_k1, lambda_q2, lambda_k2, subln_w)
    o_ret = o_ret.reshape(RET_HEADS, b * s, RET_V_DIM)
    o_diff = o_diff.reshape(DIFF_HEADS, b * s, DIFF_V_DIM)

    mixed = _gated_merge(o_ret, o_diff, w_ret_up[0], w_diff_up[0], proj)
    out = _output_projection(mixed, w_out[0], x2, final_norm_w[None, :])
    return out.reshape(b, s, d)
```

```python
import functools
import math

import jax
import jax.numpy as jnp
from jax import lax
from jax.experimental import pallas as pl
from jax.experimental.pallas import tpu as pltpu

F32 = jnp.float32
BF16 = jnp.bfloat16

D_MODEL = 2048
RET_QK_DIM = 128
RET_V_DIM = 256
RET_HEADS = D_MODEL // RET_V_DIM
DIFF_HEAD_DIM = 128
DIFF_V_DIM = 2 * DIFF_HEAD_DIM
DIFF_HEADS = D_MODEL // DIFF_V_DIM
ROPE_THETA = 10000.0
NORM_EPS = 1e-6
SUBLN_EPS = 1e-5
LAMBDA_INIT = 0.8 - 0.6 * math.exp(-0.3 * 0)

RET_QK_WIDTH = RET_HEADS * RET_QK_DIM
RET_WIDTH = RET_HEADS * RET_V_DIM
DIFF_QK_WIDTH = DIFF_HEADS * 2 * DIFF_HEAD_DIM
DIFF_WIDTH = DIFF_HEADS * DIFF_V_DIM
IN_WIDTH = 2 * RET_QK_WIDTH + 2 * RET_WIDTH + 2 * DIFF_QK_WIDTH + 2 * DIFF_WIDTH + 2 * D_MODEL

LANES = 128
SLAB_RQ = 0
SLAB_RK = SLAB_RQ + RET_QK_WIDTH // LANES
SLAB_RV = SLAB_RK + RET_QK_WIDTH // LANES
SLAB_RZ = SLAB_RV + RET_WIDTH // LANES
SLAB_DQ = SLAB_RZ + RET_WIDTH // LANES
SLAB_DK = SLAB_DQ + DIFF_QK_WIDTH // LANES
SLAB_DV = SLAB_DK + DIFF_QK_WIDTH // LANES
SLAB_DZ = SLAB_DV + DIFF_WIDTH // LANES
SLAB_GR = SLAB_DZ + DIFF_WIDTH // LANES
SLAB_GD = SLAB_GR + D_MODEL // LANES
N_SLABS = IN_WIDTH // LANES

VMEM_LIMIT = 56 * 1024 * 1024

RET_CHUNK = 256
ATT_BLOCK = 256
ATT_SKEW = 1


def _params(n_axes, vmem=VMEM_LIMIT, flags=None):
    return pltpu.CompilerParams(dimension_semantics=("arbitrary",) * n_axes,
                                vmem_limit_bytes=vmem, flags=flags)


def _rmsnorm_kernel(x_ref, w_ref, o_ref):
    x = x_ref[...]
    ms = jnp.mean(x * x, axis=-1, keepdims=True)
    o_ref[...] = (x * lax.rsqrt(ms + NORM_EPS) * w_ref[...]).astype(o_ref.dtype)


def _input_rmsnorm(x2, w, tm=512):
    m, d = x2.shape
    return pl.pallas_call(
        _rmsnorm_kernel,
        out_shape=jax.ShapeDtypeStruct((m, d), BF16),
        grid=(m // tm,),
        in_specs=[pl.BlockSpec((tm, d), lambda i: (i, 0)),
                  pl.BlockSpec((1, d), lambda i: (0, 0))],
        out_specs=pl.BlockSpec((tm, d), lambda i: (i, 0)),
        compiler_params=_params(1),
        name="input_rmsnorm",
    )(x2, w)


def _rotary(x, cos, sin_signed):
    return x * cos + pltpu.roll(x, x.shape[-1] // 2, axis=x.ndim - 1) * sin_signed


def _sigmoid(g):
    return 0.5 * jnp.tanh(0.5 * g) + 0.5


def _silu(z):
    h = 0.5 * z
    return h * jnp.tanh(h) + h


_SECTIONS = ((RET_QK_WIDTH, "rotary", 0), (RET_QK_WIDTH, "rotary", 1),
             (RET_WIDTH, "plain", 0), (RET_WIDTH, "silu", 0),
             (DIFF_QK_WIDTH, "rotary", 2), (DIFF_QK_WIDTH, "rotary", 3),
             (DIFF_WIDTH, "plain", 0), (DIFF_WIDTH, "silu", 0),
             (D_MODEL, "sigmoid", 0), (D_MODEL, "sigmoid", 0))


def _tile_kinds(tn):
    kinds = []
    for width, kind, table in _SECTIONS:
        assert width % tn == 0
        kinds.extend([(kind, table)] * (width // tn))
    return kinds


def _any_of(j, tiles):
    return functools.reduce(jnp.logical_or, [j == t for t in tiles])


def _inproj_kernel(x_ref, w_ref, cos_ref, sin_ref, s1_ref, s2_ref, s3_ref,
                   o_ref, s1bf_ref, s2bf_ref, s3bf_ref, *, kinds):
    @pl.when(pl.program_id(1) == 0)
    def _():
        for src, dst in ((s1_ref, s1bf_ref), (s2_ref, s2bf_ref), (s3_ref, s3bf_ref)):
            dst[...] = src[...].astype(BF16)

    epilogues = {
        "plain": lambda a: a,
        "rotary": lambda a: _rotary(a, cos_ref[...], sin_ref[...]),
        "silu": _silu,
        "sigmoid": _sigmoid,
    }
    j = pl.program_id(0)
    for kind, fn in epilogues.items():
        tiles = [t for t, (k, _) in enumerate(kinds) if k == kind]

        @pl.when(_any_of(j, tiles))
        def _(fn=fn):
            acc = jnp.dot(x_ref[...], w_ref[...].astype(BF16), preferred_element_type=F32)
            for s in range(o_ref.shape[0]):
                o_ref[s] = fn(acc[:, s * LANES:(s + 1) * LANES]).astype(o_ref.dtype)


def _input_projection(xn, w_in, cos_tables, sin_tables, side_weights, tm=1024, tn=1024):
    m, d = xn.shape
    n = w_in.shape[1]
    s_len = cos_tables.shape[1]
    kinds = _tile_kinds(tn)
    rotary_tiles = [t for t, (k, _) in enumerate(kinds) if k == "rotary"]

    def table_map(j, i):
        tab = sum(jnp.where(j == t, table, 0) for t, (_, table) in enumerate(kinds) if table)
        return (tab, jnp.where(_any_of(j, rotary_tiles), i % (s_len // tm), 0), 0)

    n_tiles = n // tn
    side_rows, side_cols = side_weights[0].shape
    band = LANES
    n_bands = side_rows // band
    assert n_bands <= n_tiles and all(w.shape == (side_rows, side_cols) for w in side_weights)
    side_spec = pl.BlockSpec((band, side_cols), lambda j, i: (jnp.minimum(j, n_bands - 1), 0))

    outs = pl.pallas_call(
        functools.partial(_inproj_kernel, kinds=kinds),
        out_shape=[jax.ShapeDtypeStruct((n // LANES, m, LANES), BF16)]
                  + [jax.ShapeDtypeStruct((side_rows, side_cols), BF16)] * len(side_weights),
        grid=(n_tiles, m // tm),
        in_specs=[pl.BlockSpec((tm, d), lambda j, i: (i, 0)),
                  pl.BlockSpec((d, tn), lambda j, i: (0, j)),
                  pl.BlockSpec((None, tm, LANES), table_map),
                  pl.BlockSpec((None, tm, LANES), table_map)]
                 + [side_spec] * len(side_weights),
        out_specs=[pl.BlockSpec((tn // LANES, tm, LANES), lambda j, i: (j, i, 0))]
                  + [side_spec] * len(side_weights),
        compiler_params=_params(2),
        name="input_projection",
    )(xn, w_in, cos_tables, sin_tables, *side_weights)
    return outs[0], outs[1:]


def _retention_kernel(lg_ref, q_ref, k_ref, v_ref, z_ref, o_ref, qx_ref, kz_ref, vc_ref):
    s_len = q_ref.shape[0]
    c = RET_CHUNK
    n_chunks = s_len // c
    lg = lg_ref[pl.program_id(1)]

    rowq = lax.broadcasted_iota(jnp.int32, (c, RET_QK_DIM), 0).astype(F32)
    zeta = jnp.exp(lg * (c - 1.0 - rowq))
    xi = jnp.exp(lg * (rowq + 1.0))
    for n in range(n_chunks):
        sl = slice(n * c, (n + 1) * c)
        qx_ref[sl, :] = (q_ref[sl, :].astype(F32) * xi).astype(BF16)
        kz_ref[sl, :] = (k_ref[sl, :].astype(F32) * zeta).astype(BF16)
    vc_ref[...] = jnp.concatenate([v_ref[0], v_ref[1]], axis=-1)

    row = lax.broadcasted_iota(jnp.int32, (c, c), 0).astype(F32)
    col = lax.broadcasted_iota(jnp.int32, (c, c), 1).astype(F32)
    rel = row - col
    decay_mask = jnp.where(rel >= 0.0, jnp.exp(lg * jnp.maximum(rel, 0.0)), 0.0)
    chunk_decay = jnp.exp(jnp.zeros((1, RET_V_DIM), F32) + lg * c)

    state = jnp.zeros((RET_QK_DIM, RET_V_DIM), F32)
    for n in range(n_chunks):
        sl = slice(n * c, (n + 1) * c)
        vc = vc_ref[sl, :]
        scores = lax.dot_general(q_ref[sl, :], k_ref[sl, :], (((1,), (1,)), ((), ())),
                                 preferred_element_type=F32) * decay_mask
        o = jnp.dot(scores.astype(BF16), vc, preferred_element_type=F32)
        o = o + jnp.dot(qx_ref[sl, :], state.astype(BF16), preferred_element_type=F32)
        if n + 1 < n_chunks:
            kv = lax.dot_general(kz_ref[sl, :], vc, (((0,), (0,)), ((), ())),
                                 preferred_element_type=F32)
            state = state * chunk_decay + kv

        ms = jnp.mean(o * o, axis=-1, keepdims=True)
        o = o * lax.rsqrt(ms + NORM_EPS)
        silu_z = jnp.concatenate([z_ref[0, sl, :], z_ref[1, sl, :]], axis=-1).astype(F32)
        o_ref[sl, :] = (o * silu_z).astype(o_ref.dtype)


def _retention(proj4, log_gamma):
    _, b, s, _ = proj4.shape
    qk_spec = lambda base: pl.BlockSpec((None, None, s, LANES),
                                        lambda bi, h, base=base: (base + h, bi, 0, 0))
    pair_spec = lambda base: pl.BlockSpec((2, None, s, LANES),
                                          lambda bi, h, base=base: (base // 2 + h, bi, 0, 0))
    return pl.pallas_call(
        _retention_kernel,
        out_shape=jax.ShapeDtypeStruct((RET_HEADS, b, s, RET_V_DIM), BF16),
        grid=(b, RET_HEADS),
        in_specs=[pl.BlockSpec(memory_space=pltpu.SMEM),
                  qk_spec(SLAB_RQ), qk_spec(SLAB_RK), pair_spec(SLAB_RV), pair_spec(SLAB_RZ)],
        out_specs=pl.BlockSpec((None, None, s, RET_V_DIM), lambda bi, h: (h, bi, 0, 0)),
        scratch_shapes=[pltpu.VMEM((s, RET_QK_DIM), BF16)] * 2
                       + [pltpu.VMEM((s, RET_V_DIM), BF16)],
        compiler_params=_params(2),
        name="retention",
    )(log_gamma, proj4, proj4, proj4, proj4)


def _diff_attn_kernel(q_ref, k_ref, v_ref, z_ref,
                      lq1_ref, lk1_ref, lq2_ref, lk2_ref, subln_ref, o_ref, vc_ref):
    t = ATT_BLOCK
    s_len = vc_ref.shape[0]
    vc_ref[...] = jnp.concatenate([v_ref[0], v_ref[1]], axis=-1)

    lam = (jnp.exp(jnp.sum(lq1_ref[...] * lk1_ref[...], axis=-1, keepdims=True))
           - jnp.exp(jnp.sum(lq2_ref[...] * lk2_ref[...], axis=-1, keepdims=True))
           + LAMBDA_INIT)
    causal = (lax.broadcasted_iota(jnp.int32, (t, t), 1)
              <= lax.broadcasted_iota(jnp.int32, (t, t), 0))
    nt_dims = (((1,), (1,)), ((), ()))

    units = [(n, c) for n in reversed(range(s_len // t)) for c in range(2)]

    def scores(u):
        n, c = units[u]
        r0 = n * t
        q = q_ref[c, r0:r0 + t, :]
        s_diag = jnp.where(causal,
                           lax.dot_general(q, k_ref[c, r0:r0 + t, :], nt_dims,
                                           preferred_element_type=F32), -jnp.inf)
        m = jnp.max(s_diag, axis=-1, keepdims=True)
        s_off = None
        if n > 0:
            s_off = lax.dot_general(q, k_ref[c, 0:r0, :], nt_dims, preferred_element_type=F32)
            m = jnp.maximum(m, jnp.max(s_off, axis=-1, keepdims=True))
        return s_diag, s_off, m

    def probs(sc):
        s_diag, s_off, m = sc
        p_diag = jnp.exp2(s_diag - m)
        l = jnp.sum(p_diag, axis=-1, keepdims=True)
        p_off = None
        if s_off is not None:
            p_off = jnp.exp2(s_off - m)
            l = l + jnp.sum(p_off, axis=-1, keepdims=True)
            p_off = p_off.astype(BF16)
        return p_diag.astype(BF16), p_off, l

    def weighted(u, pr):
        n, _ = units[u]
        r0 = n * t
        p_diag, p_off, l = pr
        acc = jnp.dot(p_diag, vc_ref[r0:r0 + t, :], preferred_element_type=F32)
        if p_off is not None:
            acc = acc + jnp.dot(p_off, vc_ref[0:r0, :], preferred_element_type=F32)
        return acc, l

    def finish(n, res):
        r0 = n * t
        o = res[0][0] * (1.0 / res[0][1]) - res[1][0] * (lam / res[1][1])
        ms = jnp.mean(o * o, axis=-1, keepdims=True)
        o = o * lax.rsqrt(ms + SUBLN_EPS) * subln_ref[...] * (1.0 - LAMBDA_INIT)
        silu_z = jnp.concatenate([z_ref[0, r0:r0 + t, :], z_ref[1, r0:r0 + t, :]],
                                 axis=-1).astype(F32)
        o_ref[r0:r0 + t, :] = (o * silu_z).astype(o_ref.dtype)

    skew = ATT_SKEW
    sc, pr, res = {}, {}, {}
    for step in range(len(units) + 2 * skew):
        u_sc, u_pr, u_pv = step, step - skew, step - 2 * skew
        if 0 <= u_sc < len(units):
            sc[u_sc] = scores(u_sc)
        if 0 <= u_pr < len(units):
            pr[u_pr] = probs(sc.pop(u_pr))
        if 0 <= u_pv < len(units):
            res[u_pv] = weighted(u_pv, pr.pop(u_pv))
            if units[u_pv][1] == 1:
                finish(units[u_pv][0], (res.pop(u_pv - 1), res.pop(u_pv)))


def _diff_attention(proj4, lq1, lk1, lq2, lk2, subln_w):
    _, b, s, _ = proj4.shape
    full = lambda base: pl.BlockSpec((2, None, s, LANES),
                                     lambda bi, h, base=base: (base // 2 + h, bi, 0, 0))
    vec = lambda n: pl.BlockSpec((1, n), lambda bi, h: (0, 0))
    return pl.pallas_call(
        _diff_attn_kernel,
        out_shape=jax.ShapeDtypeStruct((DIFF_HEADS, b, s, DIFF_V_DIM), BF16),
        grid=(b, DIFF_HEADS),
        in_specs=[full(SLAB_DQ), full(SLAB_DK), full(SLAB_DV), full(SLAB_DZ),
                  vec(DIFF_HEAD_DIM), vec(DIFF_HEAD_DIM), vec(DIFF_HEAD_DIM), vec(DIFF_HEAD_DIM),
                  vec(DIFF_V_DIM)],
        out_specs=pl.BlockSpec((None, None, s, DIFF_V_DIM), lambda bi, h: (h, bi, 0, 0)),
        scratch_shapes=[pltpu.VMEM((s, DIFF_V_DIM), BF16)],
        compiler_params=_params(2),
        name="diff_attention",
    )(proj4, proj4, proj4, proj4, lq1, lk1, lq2, lk2, subln_w)


def _merge_kernel(a_ref, b_ref, w1_ref, w2_ref, g1_ref, g2_ref, o_ref):
    kw = a_ref.shape[2]

    def up(x_ref, w_ref):
        acc = None
        for k in range(x_ref.shape[0]):
            part = jnp.dot(x_ref[k], w_ref[k * kw:(k + 1) * kw, :], preferred_element_type=F32)
            acc = part if acc is None else acc + part
        return acc

    up1 = up(a_ref, w1_ref)
    up2 = up(b_ref, w2_ref)
    for j in range(g1_ref.shape[0]):
        sl = slice(j * LANES, (j + 1) * LANES)
        o_ref[:, sl] = (g1_ref[j].astype(F32) * up1[:, sl]
                        + g2_ref[j].astype(F32) * up2[:, sl]).astype(o_ref.dtype)


def _gated_merge(o_ret, o_diff, w_ret_up_bf, w_diff_up_bf, proj, tm=512):
    nk, m, kw = o_ret.shape
    d = w_ret_up_bf.shape[1]
    gs = d // LANES
    x_spec = pl.BlockSpec((nk, tm, kw), lambda i: (0, i, 0))
    w_spec = pl.BlockSpec((nk * kw, d), lambda i: (0, 0), pipeline_mode=pl.Buffered(1))
    g_spec = lambda base: pl.BlockSpec((gs, tm, LANES), lambda i, base=base: (base // gs, i, 0))
    return pl.pallas_call(
        _merge_kernel,
        out_shape=jax.ShapeDtypeStruct((m, d), BF16),
        grid=(m // tm,),
        in_specs=[x_spec, x_spec, w_spec, w_spec, g_spec(SLAB_GR), g_spec(SLAB_GD)],
        out_specs=pl.BlockSpec((tm, d), lambda i: (i, 0)),
        compiler_params=_params(1),
        name="gated_merge",
    )(o_ret, o_diff, w_ret_up_bf, w_diff_up_bf, proj, proj)


def _outproj_kernel(mx_ref, w_ref, x_ref, fw_ref, o_ref):
    half = mx_ref.shape[0] // 2
    for r in range(2):
        rows = slice(r * half, (r + 1) * half)
        h = x_ref[rows, :] + jnp.dot(mx_ref[rows, :], w_ref[...], preferred_element_type=F32)
        ms = jnp.mean(h * h, axis=-1, keepdims=True)
        o_ref[rows, :] = h * lax.rsqrt(ms + NORM_EPS) * fw_ref[...]


def _output_projection(mixed, w_out_bf, x2, final_w, tm=512):
    m, d = x2.shape
    return pl.pallas_call(
        _outproj_kernel,
        out_shape=jax.ShapeDtypeStruct((m, d), F32),
        grid=(m // tm,),
        in_specs=[pl.BlockSpec((tm, d), lambda i: (i, 0)),
                  pl.BlockSpec((d, d), lambda i: (0, 0), pipeline_mode=pl.Buffered(1)),
                  pl.BlockSpec((tm, d), lambda i: (i, 0)),
                  pl.BlockSpec((1, d), lambda i: (0, 0))],
        out_specs=pl.BlockSpec((tm, d), lambda i: (i, 0)),
        compiler_params=_params(1),
        name="output_projection",
    )(mixed, w_out_bf, x2, final_w)


def _rotary_tables(s, inv_freq):
    pos = jnp.arange(s, dtype=F32)
    ang = pos[:, None] * inv_freq[None, :]
    emb = jnp.concatenate([ang, ang], axis=-1)
    half = emb.shape[-1] // 2
    sign = jnp.concatenate([-jnp.ones((half,), F32), jnp.ones((half,), F32)])
    return jnp.cos(emb), jnp.sin(emb) * sign[None, :]


def kernel(x, norm_w, w_in, w_ret_up, w_diff_up, w_out, lambda_q1, lambda_k1, lambda_q2,
           lambda_k2, subln_w, final_norm_w):
    b, s, d = x.shape
    assert d == D_MODEL and w_in.shape == (1, D_MODEL, IN_WIDTH)
    x2 = x.reshape(b * s, d)

    ret_inv_freq = jnp.exp(-math.log(10000.0) * jnp.linspace(0.0, 1.0, RET_QK_DIM // 2, dtype=F32))
    rope_inv_freq = ROPE_THETA ** (-jnp.arange(0, DIFF_HEAD_DIM, 2, dtype=F32) / DIFF_HEAD_DIM)
    ret_cos, ret_sin = _rotary_tables(s, ret_inv_freq)
    rope_cos, rope_sin = _rotary_tables(s, rope_inv_freq)
    log_gamma = jnp.log1p(-jnp.exp2(-5.0 - jnp.arange(RET_HEADS, dtype=F32)))

    ret_k_scale = RET_QK_DIM ** -0.5
    diff_q_scale = (DIFF_HEAD_DIM ** -0.5) * math.log2(math.e)
    cos_tables = jnp.stack([ret_cos, ret_cos * ret_k_scale, rope_cos * diff_q_scale, rope_cos])
    sin_tables = jnp.stack([ret_sin, ret_sin * ret_k_scale, rope_sin * diff_q_scale, rope_sin])

    xn = _input_rmsnorm(x2, norm_w[0][None, :])
    proj, (w_ret_up_bf, w_diff_up_bf, w_out_bf) = _input_projection(
        xn, w_in[0], cos_tables, sin_tables, (w_ret_up[0], w_diff_up[0], w_out[0]))
    proj4 = proj.reshape(N_SLABS, b, s, LANES)

    o_ret = _retention(proj4, log_gamma)
    o_diff = _diff_attention(proj4, lambda_q1, lambda_k1, lambda_q2, lambda_k2, subln_w)
    o_ret = o_ret.reshape(RET_HEADS, b * s, RET_V_DIM)
    o_diff = o_diff.reshape(DIFF_HEADS, b * s, DIFF_V_DIM)

    mixed = _gated_merge(o_ret, o_diff, w_ret_up_bf, w_diff_up_bf, proj)
    out = _output_projection(mixed, w_out_bf, x2, final_norm_w[None, :])
    return out.reshape(b, s, d)
```

```python
import functools
import math

import jax
import jax.numpy as jnp
from jax import lax
from jax.experimental import pallas as pl
from jax.experimental.pallas import tpu as pltpu

F32 = jnp.float32
BF16 = jnp.bfloat16

D_MODEL = 2048
RET_QK_DIM = 128
RET_V_DIM = 256
RET_HEADS = D_MODEL // RET_V_DIM
DIFF_HEAD_DIM = 128
DIFF_V_DIM = 2 * DIFF_HEAD_DIM
DIFF_HEADS = D_MODEL // DIFF_V_DIM
ROPE_THETA = 10000.0
NORM_EPS = 1e-6
SUBLN_EPS = 1e-5
LAMBDA_INIT = 0.8 - 0.6 * math.exp(-0.3 * 0)

RET_QK_WIDTH = RET_HEADS * RET_QK_DIM
RET_WIDTH = RET_HEADS * RET_V_DIM
DIFF_QK_WIDTH = DIFF_HEADS * 2 * DIFF_HEAD_DIM
DIFF_WIDTH = DIFF_HEADS * DIFF_V_DIM
IN_WIDTH = 2 * RET_QK_WIDTH + 2 * RET_WIDTH + 2 * DIFF_QK_WIDTH + 2 * DIFF_WIDTH + 2 * D_MODEL

LANES = 128

_SECTIONS = (("rq", RET_QK_WIDTH, "rotary", 0), ("rk", RET_QK_WIDTH, "rotary", 1),
             ("rv", RET_WIDTH, "plain", 0), ("rz", RET_WIDTH, "silu", 0),
             ("dq", DIFF_QK_WIDTH, "rotary", 2), ("dk", DIFF_QK_WIDTH, "rotary", 3),
             ("dv", DIFF_WIDTH, "plain", 0), ("dz", DIFF_WIDTH, "silu", 0),
             ("gr", D_MODEL, "sigmoid", 0), ("gd", D_MODEL, "sigmoid", 0))
assert sum(sec[1] for sec in _SECTIONS) == IN_WIDTH
_KINDS = ("rotary", "plain", "silu", "sigmoid")


def _slab_base(name):
    kind = next(sec[2] for sec in _SECTIONS if sec[0] == name)
    base = 0
    for sec_name, width, sec_kind, _ in _SECTIONS:
        if sec_name == name:
            return base
        if sec_kind == kind:
            base += width // LANES
    raise KeyError(name)

VMEM_LIMIT = 56 * 1024 * 1024

RET_CHUNK = 256
ATT_BLOCK = 256
ATT_SKEW = 1


def _params(n_axes, vmem=VMEM_LIMIT, flags=None):
    return pltpu.CompilerParams(dimension_semantics=("arbitrary",) * n_axes,
                                vmem_limit_bytes=vmem, flags=flags)


def _rmsnorm_kernel(x_ref, w_ref, o_ref):
    x = x_ref[...]
    ms = jnp.mean(x * x, axis=-1, keepdims=True)
    o_ref[...] = (x * lax.rsqrt(ms + NORM_EPS) * w_ref[...]).astype(o_ref.dtype)


def _input_rmsnorm(x2, w, tm=512):
    m, d = x2.shape
    return pl.pallas_call(
        _rmsnorm_kernel,
        out_shape=jax.ShapeDtypeStruct((m, d), BF16),
        grid=(m // tm,),
        in_specs=[pl.BlockSpec((tm, d), lambda i: (i, 0)),
                  pl.BlockSpec((1, d), lambda i: (0, 0))],
        out_specs=pl.BlockSpec((tm, d), lambda i: (i, 0)),
        compiler_params=_params(1),
        name="input_rmsnorm",
    )(x2, w)


def _rotary(x, cos, sin_signed):
    return x * cos + pltpu.roll(x, x.shape[-1] // 2, axis=x.ndim - 1) * sin_signed


def _sigmoid(g):
    return 0.5 * jnp.tanh(0.5 * g) + 0.5


def _silu(z):
    h = 0.5 * z
    return h * jnp.tanh(h) + h


def _select(j, values):
    return sum(jnp.where(j == t, v, 0) for t, v in enumerate(values) if v)


def _projection_kernel(*refs, epilogue):
    if epilogue == "rotary":
        x_ref, w_ref, cos_ref, sin_ref, o_ref = refs
        fn = lambda a: _rotary(a, cos_ref[...], sin_ref[...])
    else:
        x_ref, w_ref, o_ref = refs
        fn = {"plain": lambda a: a, "silu": _silu, "sigmoid": _sigmoid}[epilogue]
    acc = jnp.dot(x_ref[...], w_ref[...].astype(BF16), preferred_element_type=F32)
    for s in range(o_ref.shape[0]):
        o_ref[s] = fn(acc[:, s * LANES:(s + 1) * LANES]).astype(o_ref.dtype)


def _input_projection(xn, w_in, cos_tables, sin_tables, epilogue, tm=1024, tn=1024):
    m, d = xn.shape
    s_len = cos_tables.shape[1]
    col_tiles, tables = [], []
    col = 0
    for _, width, kind, table in _SECTIONS:
        assert width % tn == 0
        if kind == epilogue:
            col_tiles += [col // tn + t for t in range(width // tn)]
            tables += [table] * (width // tn)
        col += width

    in_specs = [pl.BlockSpec((tm, d), lambda j, i: (i, 0)),
                pl.BlockSpec((d, tn), lambda j, i: (0, _select(j, col_tiles)))]
    operands = [xn, w_in]
    if epilogue == "rotary":
        table_spec = pl.BlockSpec((None, tm, LANES),
                                  lambda j, i: (_select(j, tables), i % (s_len // tm), 0))
        in_specs += [table_spec, table_spec]
        operands += [cos_tables, sin_tables]
    return pl.pallas_call(
        functools.partial(_projection_kernel, epilogue=epilogue),
        out_shape=jax.ShapeDtypeStruct((len(col_tiles) * tn // LANES, m, LANES), BF16),
        grid=(len(col_tiles), m // tm),
        in_specs=in_specs,
        out_specs=pl.BlockSpec((tn // LANES, tm, LANES), lambda j, i: (j, i, 0)),
        compiler_params=_params(2),
        name="projection_" + epilogue,
    )(*operands)


def _retention_kernel(lg_ref, q_ref, k_ref, v_ref, z_ref, o_ref, qx_ref, kz_ref, vc_ref):
    s_len = q_ref.shape[0]
    c = RET_CHUNK
    n_chunks = s_len // c
    lg = lg_ref[pl.program_id(1)]

    rowq = lax.broadcasted_iota(jnp.int32, (c, RET_QK_DIM), 0).astype(F32)
    zeta = jnp.exp(lg * (c - 1.0 - rowq))
    xi = jnp.exp(lg * (rowq + 1.0))
    for n in range(n_chunks):
        sl = slice(n * c, (n + 1) * c)
        qx_ref[sl, :] = (q_ref[sl, :].astype(F32) * xi).astype(BF16)
        kz_ref[sl, :] = (k_ref[sl, :].astype(F32) * zeta).astype(BF16)
    vc_ref[...] = jnp.concatenate([v_ref[0], v_ref[1]], axis=-1)

    row = lax.broadcasted_iota(jnp.int32, (c, c), 0).astype(F32)
    col = lax.broadcasted_iota(jnp.int32, (c, c), 1).astype(F32)
    rel = row - col
    decay_mask = jnp.where(rel >= 0.0, jnp.exp(lg * jnp.maximum(rel, 0.0)), 0.0)
    chunk_decay = jnp.exp(jnp.zeros((1, RET_V_DIM), F32) + lg * c)

    state = jnp.zeros((RET_QK_DIM, RET_V_DIM), F32)
    for n in range(n_chunks):
        sl = slice(n * c, (n + 1) * c)
        vc = vc_ref[sl, :]
        scores = lax.dot_general(q_ref[sl, :], k_ref[sl, :], (((1,), (1,)), ((), ())),
                                 preferred_element_type=F32) * decay_mask
        o = jnp.dot(scores.astype(BF16), vc, preferred_element_type=F32)
        o = o + jnp.dot(qx_ref[sl, :], state.astype(BF16), preferred_element_type=F32)
        if n + 1 < n_chunks:
            kv = lax.dot_general(kz_ref[sl, :], vc, (((0,), (0,)), ((), ())),
                                 preferred_element_type=F32)
            state = state * chunk_decay + kv

        ms = jnp.mean(o * o, axis=-1, keepdims=True)
        o = o * lax.rsqrt(ms + NORM_EPS)
        silu_z = jnp.concatenate([z_ref[0, sl, :], z_ref[1, sl, :]], axis=-1).astype(F32)
        o_ref[sl, :] = (o * silu_z).astype(o_ref.dtype)


def _retention(rot4, plain4, silu4, log_gamma):
    _, b, s, _ = rot4.shape
    qk_spec = lambda name: pl.BlockSpec(
        (None, None, s, LANES), lambda bi, h, base=_slab_base(name): (base + h, bi, 0, 0))
    pair_spec = lambda name: pl.BlockSpec(
        (2, None, s, LANES), lambda bi, h, base=_slab_base(name): (base // 2 + h, bi, 0, 0))
    return pl.pallas_call(
        _retention_kernel,
        out_shape=jax.ShapeDtypeStruct((RET_HEADS, b, s, RET_V_DIM), BF16),
        grid=(b, RET_HEADS),
        in_specs=[pl.BlockSpec(memory_space=pltpu.SMEM),
                  qk_spec("rq"), qk_spec("rk"), pair_spec("rv"), pair_spec("rz")],
        out_specs=pl.BlockSpec((None, None, s, RET_V_DIM), lambda bi, h: (h, bi, 0, 0)),
        scratch_shapes=[pltpu.VMEM((s, RET_QK_DIM), BF16)] * 2
                       + [pltpu.VMEM((s, RET_V_DIM), BF16)],
        compiler_params=_params(2),
        name="retention",
    )(log_gamma, rot4, rot4, plain4, silu4)


def _diff_attn_kernel(q_ref, k_ref, v_ref, z_ref,
                      lq1_ref, lk1_ref, lq2_ref, lk2_ref, subln_ref, s1_ref, s2_ref, s3_ref,
                      o_ref, s1bf_ref, s2bf_ref, s3bf_ref, vc_ref):
    for src, dst in ((s1_ref, s1bf_ref), (s2_ref, s2bf_ref), (s3_ref, s3bf_ref)):
        dst[...] = src[...].astype(BF16)

    t = ATT_BLOCK
    s_len = vc_ref.shape[0]
    vc_ref[...] = jnp.concatenate([v_ref[0], v_ref[1]], axis=-1)

    lam = (jnp.exp(jnp.sum(lq1_ref[...] * lk1_ref[...], axis=-1, keepdims=True))
           - jnp.exp(jnp.sum(lq2_ref[...] * lk2_ref[...], axis=-1, keepdims=True))
           + LAMBDA_INIT)
    causal = (lax.broadcasted_iota(jnp.int32, (t, t), 1)
              <= lax.broadcasted_iota(jnp.int32, (t, t), 0))
    nt_dims = (((1,), (1,)), ((), ()))

    units = [(n, c) for n in reversed(range(s_len // t)) for c in range(2)]

    def scores(u):
        n, c = units[u]
        r0 = n * t
        q = q_ref[c, r0:r0 + t, :]
        s_diag = jnp.where(causal,
                           lax.dot_general(q, k_ref[c, r0:r0 + t, :], nt_dims,
                                           preferred_element_type=F32), -jnp.inf)
        m = jnp.max(s_diag, axis=-1, keepdims=True)
        s_off = None
        if n > 0:
            s_off = lax.dot_general(q, k_ref[c, 0:r0, :], nt_dims, preferred_element_type=F32)
            m = jnp.maximum(m, jnp.max(s_off, axis=-1, keepdims=True))
        return s_diag, s_off, m

    def probs(sc):
        s_diag, s_off, m = sc
        p_diag = jnp.exp2(s_diag - m)
        l = jnp.sum(p_diag, axis=-1, keepdims=True)
        p_off = None
        if s_off is not None:
            p_off = jnp.exp2(s_off - m)
            l = l + jnp.sum(p_off, axis=-1, keepdims=True)
            p_off = p_off.astype(BF16)
        return p_diag.astype(BF16), p_off, l

    def weighted(u, pr):
        n, _ = units[u]
        r0 = n * t
        p_diag, p_off, l = pr
        acc = jnp.dot(p_diag, vc_ref[r0:r0 + t, :], preferred_element_type=F32)
        if p_off is not None:
            acc = acc + jnp.dot(p_off, vc_ref[0:r0, :], preferred_element_type=F32)
        return acc, l

    def finish(n, res):
        r0 = n * t
        o = res[0][0] * (1.0 / res[0][1]) - res[1][0] * (lam / res[1][1])
        ms = jnp.mean(o * o, axis=-1, keepdims=True)
        o = o * lax.rsqrt(ms + SUBLN_EPS) * subln_ref[...] * (1.0 - LAMBDA_INIT)
        silu_z = jnp.concatenate([z_ref[0, r0:r0 + t, :], z_ref[1, r0:r0 + t, :]],
                                 axis=-1).astype(F32)
        o_ref[r0:r0 + t, :] = (o * silu_z).astype(o_ref.dtype)

    skew = ATT_SKEW
    sc, pr, res = {}, {}, {}
    for step in range(len(units) + 2 * skew):
        u_sc, u_pr, u_pv = step, step - skew, step - 2 * skew
        if 0 <= u_sc < len(units):
            sc[u_sc] = scores(u_sc)
        if 0 <= u_pr < len(units):
            pr[u_pr] = probs(sc.pop(u_pr))
        if 0 <= u_pv < len(units):
            res[u_pv] = weighted(u_pv, pr.pop(u_pv))
            if units[u_pv][1] == 1:
                finish(units[u_pv][0], (res.pop(u_pv - 1), res.pop(u_pv)))


def _diff_attention(rot4, plain4, silu4, lq1, lk1, lq2, lk2, subln_w, side_weights):
    _, b, s, _ = rot4.shape
    full = lambda name: pl.BlockSpec(
        (2, None, s, LANES), lambda bi, h, base=_slab_base(name): (base // 2 + h, bi, 0, 0))
    vec = lambda n: pl.BlockSpec((1, n), lambda bi, h: (0, 0))
    side_rows, side_cols = side_weights[0].shape
    assert all(w.shape == (side_rows, side_cols) for w in side_weights)
    band = side_rows // (b * DIFF_HEADS)
    side_spec = pl.BlockSpec((band, side_cols), lambda bi, h: (bi * DIFF_HEADS + h, 0))
    outs = pl.pallas_call(
        _diff_attn_kernel,
        out_shape=[jax.ShapeDtypeStruct((DIFF_HEADS, b, s, DIFF_V_DIM), BF16)]
                  + [jax.ShapeDtypeStruct((side_rows, side_cols), BF16)] * len(side_weights),
        grid=(b, DIFF_HEADS),
        in_specs=[full("dq"), full("dk"), full("dv"), full("dz"),
                  vec(DIFF_HEAD_DIM), vec(DIFF_HEAD_DIM), vec(DIFF_HEAD_DIM), vec(DIFF_HEAD_DIM),
                  vec(DIFF_V_DIM)] + [side_spec] * len(side_weights),
        out_specs=[pl.BlockSpec((None, None, s, DIFF_V_DIM), lambda bi, h: (h, bi, 0, 0))]
                  + [side_spec] * len(side_weights),
        scratch_shapes=[pltpu.VMEM((s, DIFF_V_DIM), BF16)],
        compiler_params=_params(2),
        name="diff_attention",
    )(rot4, rot4, plain4, silu4, lq1, lk1, lq2, lk2, subln_w, *side_weights)
    return outs[0], outs[1:]


def _merge_kernel(a_ref, b_ref, w1_ref, w2_ref, g1_ref, g2_ref, o_ref):
    kw = a_ref.shape[2]

    def up(x_ref, w_ref):
        acc = None
        for k in range(x_ref.shape[0]):
            part = jnp.dot(x_ref[k], w_ref[k * kw:(k + 1) * kw, :], preferred_element_type=F32)
            acc = part if acc is None else acc + part
        return acc

    up1 = up(a_ref, w1_ref)
    up2 = up(b_ref, w2_ref)
    for j in range(g1_ref.shape[0]):
        sl = slice(j * LANES, (j + 1) * LANES)
        o_ref[:, sl] = (g1_ref[j].astype(F32) * up1[:, sl]
                        + g2_ref[j].astype(F32) * up2[:, sl]).astype(o_ref.dtype)


def _gated_merge(o_ret, o_diff, w_ret_up_bf, w_diff_up_bf, gates, tm=512):
    nk, m, kw = o_ret.shape
    d = w_ret_up_bf.shape[1]
    gs = d // LANES
    x_spec = pl.BlockSpec((nk, tm, kw), lambda i: (0, i, 0))
    w_spec = pl.BlockSpec((nk * kw, d), lambda i: (0, 0), pipeline_mode=pl.Buffered(1))
    g_spec = lambda name: pl.BlockSpec(
        (gs, tm, LANES), lambda i, base=_slab_base(name): (base // gs, i, 0))
    return pl.pallas_call(
        _merge_kernel,
        out_shape=jax.ShapeDtypeStruct((m, d), BF16),
        grid=(m // tm,),
        in_specs=[x_spec, x_spec, w_spec, w_spec, g_spec("gr"), g_spec("gd")],
        out_specs=pl.BlockSpec((tm, d), lambda i: (i, 0)),
        compiler_params=_params(1),
        name="gated_merge",
    )(o_ret, o_diff, w_ret_up_bf, w_diff_up_bf, gates, gates)


def _outproj_kernel(mx_ref, w_ref, x_ref, fw_ref, o_ref):
    half = mx_ref.shape[0] // 2
    for r in range(2):
        rows = slice(r * half, (r + 1) * half)
        h = x_ref[rows, :] + jnp.dot(mx_ref[rows, :], w_ref[...], preferred_element_type=F32)
        ms = jnp.mean(h * h, axis=-1, keepdims=True)
        o_ref[rows, :] = h * lax.rsqrt(ms + NORM_EPS) * fw_ref[...]


def _output_projection(mixed, w_out_bf, x2, final_w, tm=512):
    m, d = x2.shape
    return pl.pallas_call(
        _outproj_kernel,
        out_shape=jax.ShapeDtypeStruct((m, d), F32),
        grid=(m // tm,),
        in_specs=[pl.BlockSpec((tm, d), lambda i: (i, 0)),
                  pl.BlockSpec((d, d), lambda i: (0, 0), pipeline_mode=pl.Buffered(1)),
                  pl.BlockSpec((tm, d), lambda i: (i, 0)),
                  pl.BlockSpec((1, d), lambda i: (0, 0))],
        out_specs=pl.BlockSpec((tm, d), lambda i: (i, 0)),
        compiler_params=_params(1),
        name="output_projection",
    )(mixed, w_out_bf, x2, final_w)


def _rotary_tables(s, inv_freq):
    pos = jnp.arange(s, dtype=F32)
    ang = pos[:, None] * inv_freq[None, :]
    emb = jnp.concatenate([ang, ang], axis=-1)
    half = emb.shape[-1] // 2
    sign = jnp.concatenate([-jnp.ones((half,), F32), jnp.ones((half,), F32)])
    return jnp.cos(emb), jnp.sin(emb) * sign[None, :]


def kernel(x, norm_w, w_in, w_ret_up, w_diff_up, w_out, lambda_q1, lambda_k1, lambda_q2,
           lambda_k2, subln_w, final_norm_w):
    b, s, d = x.shape
    assert d == D_MODEL and w_in.shape == (1, D_MODEL, IN_WIDTH)
    x2 = x.reshape(b * s, d)

    ret_inv_freq = jnp.exp(-math.log(10000.0) * jnp.linspace(0.0, 1.0, RET_QK_DIM // 2, dtype=F32))
    rope_inv_freq = ROPE_THETA ** (-jnp.arange(0, DIFF_HEAD_DIM, 2, dtype=F32) / DIFF_HEAD_DIM)
    ret_cos, ret_sin = _rotary_tables(s, ret_inv_freq)
    rope_cos, rope_sin = _rotary_tables(s, rope_inv_freq)
    log_gamma = jnp.log1p(-jnp.exp2(-5.0 - jnp.arange(RET_HEADS, dtype=F32)))

    ret_k_scale = RET_QK_DIM ** -0.5
    diff_q_scale = (DIFF_HEAD_DIM ** -0.5) * math.log2(math.e)
    cos_tables = jnp.stack([ret_cos, ret_cos * ret_k_scale, rope_cos * diff_q_scale, rope_cos])
    sin_tables = jnp.stack([ret_sin, ret_sin * ret_k_scale, rope_sin * diff_q_scale, rope_sin])

    xn = _input_rmsnorm(x2, norm_w[0][None, :])
    proj = {kind: _input_projection(xn, w_in[0], cos_tables, sin_tables, kind) for kind in _KINDS}
    by_pos = lambda a: a.reshape(a.shape[0], b, s, LANES)
    rot4, plain4, silu4 = by_pos(proj["rotary"]), by_pos(proj["plain"]), by_pos(proj["silu"])

    o_ret = _retention(rot4, plain4, silu4, log_gamma)
    o_diff, (w_ret_up_bf, w_diff_up_bf, w_out_bf) = _diff_attention(
        rot4, plain4, silu4, lambda_q1, lambda_k1, lambda_q2, lambda_k2, subln_w,
        (w_ret_up[0], w_diff_up[0], w_out[0]))
    o_ret = o_ret.reshape(RET_HEADS, b * s, RET_V_DIM)
    o_diff = o_diff.reshape(DIFF_HEADS, b * s, DIFF_V_DIM)

    mixed = _gated_merge(o_ret, o_diff, w_ret_up_bf, w_diff_up_bf, proj["sigmoid"])
    out = _output_projection(mixed, w_out_bf, x2, final_norm_w[None, :])
    return out.reshape(b, s, d)
```

```python
import functools
import math

import jax
import jax.numpy as jnp
import numpy as np
from jax import lax
from jax.experimental import pallas as pl
from jax.experimental.pallas import tpu as pltpu

F32 = jnp.float32
BF16 = jnp.bfloat16

D_MODEL = 2048
RET_QK_DIM = 128
RET_V_DIM = 256
RET_HEADS = D_MODEL // RET_V_DIM
DIFF_HEAD_DIM = 128
DIFF_V_DIM = 2 * DIFF_HEAD_DIM
DIFF_HEADS = D_MODEL // DIFF_V_DIM
ROPE_THETA = 10000.0
NORM_EPS = 1e-6
SUBLN_EPS = 1e-5
LAMBDA_INIT = 0.8 - 0.6 * math.exp(-0.3 * 0)

RET_QK_WIDTH = RET_HEADS * RET_QK_DIM
RET_WIDTH = RET_HEADS * RET_V_DIM
DIFF_QK_WIDTH = DIFF_HEADS * 2 * DIFF_HEAD_DIM
DIFF_WIDTH = DIFF_HEADS * DIFF_V_DIM
IN_WIDTH = 2 * RET_QK_WIDTH + 2 * RET_WIDTH + 2 * DIFF_QK_WIDTH + 2 * DIFF_WIDTH + 2 * D_MODEL

LANES = 128

_SECTIONS = (("rq", RET_QK_WIDTH, "rotary", 0), ("rk", RET_QK_WIDTH, "rotary", 1),
             ("rv", RET_WIDTH, "plain", 0), ("rz", RET_WIDTH, "silu", 0),
             ("dq", DIFF_QK_WIDTH, "rotary", 2), ("dk", DIFF_QK_WIDTH, "rotary", 3),
             ("dv", DIFF_WIDTH, "plain", 0), ("dz", DIFF_WIDTH, "silu", 0),
             ("gr", D_MODEL, "sigmoid", 0), ("gd", D_MODEL, "sigmoid", 0))
assert sum(sec[1] for sec in _SECTIONS) == IN_WIDTH
_KINDS = ("rotary", "plain", "silu", "sigmoid")


def _slab_base(name):
    kind = next(sec[2] for sec in _SECTIONS if sec[0] == name)
    base = 0
    for sec_name, width, sec_kind, _ in _SECTIONS:
        if sec_name == name:
            return base
        if sec_kind == kind:
            base += width // LANES
    raise KeyError(name)

VMEM_LIMIT = 56 * 1024 * 1024

RET_CHUNK = 256
ATT_BLOCK = 256
ATT_SKEW = 1


def _params(n_axes, vmem=VMEM_LIMIT, flags=None):
    return pltpu.CompilerParams(dimension_semantics=("arbitrary",) * n_axes,
                                vmem_limit_bytes=vmem, flags=flags)


def _rmsnorm_kernel(x_ref, w_ref, o_ref):
    x = x_ref[...]
    ms = jnp.mean(x * x, axis=-1, keepdims=True)
    o_ref[...] = (x * lax.rsqrt(ms + NORM_EPS) * w_ref[...]).astype(o_ref.dtype)


def _input_rmsnorm(x2, w, tm=512):
    m, d = x2.shape
    return pl.pallas_call(
        _rmsnorm_kernel,
        out_shape=jax.ShapeDtypeStruct((m, d), BF16),
        grid=(m // tm,),
        in_specs=[pl.BlockSpec((tm, d), lambda i: (i, 0)),
                  pl.BlockSpec((1, d), lambda i: (0, 0))],
        out_specs=pl.BlockSpec((tm, d), lambda i: (i, 0)),
        compiler_params=_params(1),
        name="input_rmsnorm",
    )(x2, w)


def _rotary(x, cos, sin_signed):
    return x * cos + pltpu.roll(x, x.shape[-1] // 2, axis=x.ndim - 1) * sin_signed


def _sigmoid(g):
    return 0.5 * jnp.tanh(0.5 * g) + 0.5


def _silu(z):
    h = 0.5 * z
    return h * jnp.tanh(h) + h


def _select(j, values):
    return sum(jnp.where(j == t, v, 0) for t, v in enumerate(values) if v)


def _projection_kernel(*refs, epilogue):
    if epilogue == "rotary":
        x_ref, w_ref, cos_ref, sin_ref, o_ref = refs
        fn = lambda a: _rotary(a, cos_ref[...], sin_ref[...])
    else:
        x_ref, w_ref, o_ref = refs
        fn = {"plain": lambda a: a, "silu": _silu, "sigmoid": _sigmoid}[epilogue]
    acc = jnp.dot(x_ref[...], w_ref[...].astype(BF16), preferred_element_type=F32)
    for s in range(o_ref.shape[0]):
        o_ref[s] = fn(acc[:, s * LANES:(s + 1) * LANES]).astype(o_ref.dtype)


def _input_projection(xn, w_in, cos_tables, sin_tables, epilogue, tm=2048, tn=1024):
    m, d = xn.shape
    s_len = cos_tables.shape[1]
    col_tiles, tables = [], []
    col = 0
    for _, width, kind, table in _SECTIONS:
        assert width % tn == 0
        if kind == epilogue:
            col_tiles += [col // tn + t for t in range(width // tn)]
            tables += [table] * (width // tn)
        col += width

    in_specs = [pl.BlockSpec((tm, d), lambda j, i: (i, 0)),
                pl.BlockSpec((d, tn), lambda j, i: (0, _select(j, col_tiles)))]
    operands = [xn, w_in]
    if epilogue == "rotary":
        table_spec = pl.BlockSpec((None, tm, LANES),
                                  lambda j, i: (_select(j, tables), i % (s_len // tm), 0))
        in_specs += [table_spec, table_spec]
        operands += [cos_tables, sin_tables]
    return pl.pallas_call(
        functools.partial(_projection_kernel, epilogue=epilogue),
        out_shape=jax.ShapeDtypeStruct((len(col_tiles) * tn // LANES, m, LANES), BF16),
        grid=(len(col_tiles), m // tm),
        in_specs=in_specs,
        out_specs=pl.BlockSpec((tn // LANES, tm, LANES), lambda j, i: (j, i, 0)),
        compiler_params=_params(2),
        name="projection_" + epilogue,
    )(*operands)


def _retention_kernel(lg_ref, q_ref, k_ref, v_ref, z_ref, o_ref, qx_ref, kz_ref, vc_ref):
    s_len = q_ref.shape[0]
    c = RET_CHUNK
    n_chunks = s_len // c
    lg = lg_ref[pl.program_id(1)]

    rowq = lax.broadcasted_iota(jnp.int32, (c, RET_QK_DIM), 0).astype(F32)
    zeta = jnp.exp(lg * (c - 1.0 - rowq))
    xi = jnp.exp(lg * (rowq + 1.0))
    for n in range(n_chunks):
        sl = slice(n * c, (n + 1) * c)
        qx_ref[sl, :] = (q_ref[sl, :].astype(F32) * xi).astype(BF16)
        kz_ref[sl, :] = (k_ref[sl, :].astype(F32) * zeta).astype(BF16)
    vc_ref[...] = jnp.concatenate([v_ref[0], v_ref[1]], axis=-1)

    row = lax.broadcasted_iota(jnp.int32, (c, c), 0).astype(F32)
    col = lax.broadcasted_iota(jnp.int32, (c, c), 1).astype(F32)
    rel = row - col
    decay_mask = jnp.where(rel >= 0.0, jnp.exp(lg * jnp.maximum(rel, 0.0)), 0.0)
    chunk_decay = jnp.exp(jnp.zeros((1, RET_V_DIM), F32) + lg * c)

    state = jnp.zeros((RET_QK_DIM, RET_V_DIM), F32)
    for n in range(n_chunks):
        sl = slice(n * c, (n + 1) * c)
        vc = vc_ref[sl, :]
        scores = lax.dot_general(q_ref[sl, :], k_ref[sl, :], (((1,), (1,)), ((), ())),
                                 preferred_element_type=F32) * decay_mask
        o = jnp.dot(scores.astype(BF16), vc, preferred_element_type=F32)
        o = o + jnp.dot(qx_ref[sl, :], state.astype(BF16), preferred_element_type=F32)
        if n + 1 < n_chunks:
            kv = lax.dot_general(kz_ref[sl, :], vc, (((0,), (0,)), ((), ())),
                                 preferred_element_type=F32)
            state = state * chunk_decay + kv

        ms = jnp.mean(o * o, axis=-1, keepdims=True)
        o = o * lax.rsqrt(ms + NORM_EPS)
        silu_z = jnp.concatenate([z_ref[0, sl, :], z_ref[1, sl, :]], axis=-1).astype(F32)
        o_ref[sl, :] = (o * silu_z).astype(o_ref.dtype)


def _retention(rot4, plain4, silu4, log_gamma):
    _, b, s, _ = rot4.shape
    qk_spec = lambda name: pl.BlockSpec(
        (None, None, s, LANES), lambda bi, h, base=_slab_base(name): (base + h, bi, 0, 0))
    pair_spec = lambda name: pl.BlockSpec(
        (2, None, s, LANES), lambda bi, h, base=_slab_base(name): (base // 2 + h, bi, 0, 0))
    return pl.pallas_call(
        _retention_kernel,
        out_shape=jax.ShapeDtypeStruct((RET_HEADS, b, s, RET_V_DIM), BF16),
        grid=(b, RET_HEADS),
        in_specs=[pl.BlockSpec(memory_space=pltpu.SMEM),
                  qk_spec("rq"), qk_spec("rk"), pair_spec("rv"), pair_spec("rz")],
        out_specs=pl.BlockSpec((None, None, s, RET_V_DIM), lambda bi, h: (h, bi, 0, 0)),
        scratch_shapes=[pltpu.VMEM((s, RET_QK_DIM), BF16)] * 2
                       + [pltpu.VMEM((s, RET_V_DIM), BF16)],
        compiler_params=_params(2),
        name="retention",
    )(log_gamma, rot4, rot4, plain4, silu4)


def _diff_attn_kernel(q_ref, k_ref, v_ref, z_ref,
                      lq1_ref, lk1_ref, lq2_ref, lk2_ref, subln_ref, s1_ref, s2_ref, s3_ref,
                      o_ref, s1bf_ref, s2bf_ref, s3bf_ref, vc_ref):
    for src, dst in ((s1_ref, s1bf_ref), (s2_ref, s2bf_ref), (s3_ref, s3bf_ref)):
        dst[...] = src[...].astype(BF16)

    t = ATT_BLOCK
    s_len = vc_ref.shape[0]
    vc_ref[...] = jnp.concatenate([v_ref[0], v_ref[1]], axis=-1)

    lam = (jnp.exp(jnp.sum(lq1_ref[...] * lk1_ref[...], axis=-1, keepdims=True))
           - jnp.exp(jnp.sum(lq2_ref[...] * lk2_ref[...], axis=-1, keepdims=True))
           + LAMBDA_INIT)
    causal = (lax.broadcasted_iota(jnp.int32, (t, t), 1)
              <= lax.broadcasted_iota(jnp.int32, (t, t), 0))
    nt_dims = (((1,), (1,)), ((), ()))

    units = [(n, c) for n in reversed(range(s_len // t)) for c in range(2)]

    def scores(u):
        n, c = units[u]
        r0 = n * t
        q = q_ref[c, r0:r0 + t, :]
        s_diag = jnp.where(causal,
                           lax.dot_general(q, k_ref[c, r0:r0 + t, :], nt_dims,
                                           preferred_element_type=F32), -jnp.inf)
        m = jnp.max(s_diag, axis=-1, keepdims=True)
        s_off = None
        if n > 0:
            s_off = lax.dot_general(q, k_ref[c, 0:r0, :], nt_dims, preferred_element_type=F32)
            m = jnp.maximum(m, jnp.max(s_off, axis=-1, keepdims=True))
        return s_diag, s_off, m

    def probs(sc):
        s_diag, s_off, m = sc
        p_diag = jnp.exp2(s_diag - m)
        l = jnp.sum(p_diag, axis=-1, keepdims=True)
        p_off = None
        if s_off is not None:
            p_off = jnp.exp2(s_off - m)
            l = l + jnp.sum(p_off, axis=-1, keepdims=True)
            p_off = p_off.astype(BF16)
        return p_diag.astype(BF16), p_off, l

    def weighted(u, pr):
        n, _ = units[u]
        r0 = n * t
        p_diag, p_off, l = pr
        acc = jnp.dot(p_diag, vc_ref[r0:r0 + t, :], preferred_element_type=F32)
        if p_off is not None:
            acc = acc + jnp.dot(p_off, vc_ref[0:r0, :], preferred_element_type=F32)
        return acc, l

    def finish(n, res):
        r0 = n * t
        o = res[0][0] * (1.0 / res[0][1]) - res[1][0] * (lam / res[1][1])
        ms = jnp.mean(o * o, axis=-1, keepdims=True)
        o = o * lax.rsqrt(ms + SUBLN_EPS) * subln_ref[...] * (1.0 - LAMBDA_INIT)
        silu_z = jnp.concatenate([z_ref[0, r0:r0 + t, :], z_ref[1, r0:r0 + t, :]],
                                 axis=-1).astype(F32)
        o_ref[r0:r0 + t, :] = (o * silu_z).astype(o_ref.dtype)

    skew = ATT_SKEW
    sc, pr, res = {}, {}, {}
    for step in range(len(units) + 2 * skew):
        u_sc, u_pr, u_pv = step, step - skew, step - 2 * skew
        if 0 <= u_sc < len(units):
            sc[u_sc] = scores(u_sc)
        if 0 <= u_pr < len(units):
            pr[u_pr] = probs(sc.pop(u_pr))
        if 0 <= u_pv < len(units):
            res[u_pv] = weighted(u_pv, pr.pop(u_pv))
            if units[u_pv][1] == 1:
                finish(units[u_pv][0], (res.pop(u_pv - 1), res.pop(u_pv)))


def _diff_attention(rot4, plain4, silu4, lq1, lk1, lq2, lk2, subln_w, side_weights):
    _, b, s, _ = rot4.shape
    full = lambda name: pl.BlockSpec(
        (2, None, s, LANES), lambda bi, h, base=_slab_base(name): (base // 2 + h, bi, 0, 0))
    vec = lambda n: pl.BlockSpec((1, n), lambda bi, h: (0, 0))
    side_rows, side_cols = side_weights[0].shape
    assert all(w.shape == (side_rows, side_cols) for w in side_weights)
    band = side_rows // (b * DIFF_HEADS)
    side_spec = pl.BlockSpec((band, side_cols), lambda bi, h: (bi * DIFF_HEADS + h, 0))
    outs = pl.pallas_call(
        _diff_attn_kernel,
        out_shape=[jax.ShapeDtypeStruct((DIFF_HEADS, b, s, DIFF_V_DIM), BF16)]
                  + [jax.ShapeDtypeStruct((side_rows, side_cols), BF16)] * len(side_weights),
        grid=(b, DIFF_HEADS),
        in_specs=[full("dq"), full("dk"), full("dv"), full("dz"),
                  vec(DIFF_HEAD_DIM), vec(DIFF_HEAD_DIM), vec(DIFF_HEAD_DIM), vec(DIFF_HEAD_DIM),
                  vec(DIFF_V_DIM)] + [side_spec] * len(side_weights),
        out_specs=[pl.BlockSpec((None, None, s, DIFF_V_DIM), lambda bi, h: (h, bi, 0, 0))]
                  + [side_spec] * len(side_weights),
        scratch_shapes=[pltpu.VMEM((s, DIFF_V_DIM), BF16)],
        compiler_params=_params(2),
        name="diff_attention",
    )(rot4, rot4, plain4, silu4, lq1, lk1, lq2, lk2, subln_w, *side_weights)
    return outs[0], outs[1:]


def _merge_kernel(a_ref, b_ref, w1_ref, w2_ref, g1_ref, g2_ref, o_ref):
    kw = a_ref.shape[2]

    def up(x_ref, w_ref):
        acc = None
        for k in range(x_ref.shape[0]):
            part = jnp.dot(x_ref[k], w_ref[k * kw:(k + 1) * kw, :], preferred_element_type=F32)
            acc = part if acc is None else acc + part
        return acc

    up1 = up(a_ref, w1_ref)
    up2 = up(b_ref, w2_ref)
    for j in range(g1_ref.shape[0]):
        sl = slice(j * LANES, (j + 1) * LANES)
        o_ref[:, sl] = (g1_ref[j].astype(F32) * up1[:, sl]
                        + g2_ref[j].astype(F32) * up2[:, sl]).astype(o_ref.dtype)


def _gated_merge(o_ret, o_diff, w_ret_up_bf, w_diff_up_bf, gates, tm=512):
    nk, m, kw = o_ret.shape
    d = w_ret_up_bf.shape[1]
    gs = d // LANES
    x_spec = pl.BlockSpec((nk, tm, kw), lambda i: (0, i, 0))
    w_spec = pl.BlockSpec((nk * kw, d), lambda i: (0, 0), pipeline_mode=pl.Buffered(1))
    g_spec = lambda name: pl.BlockSpec(
        (gs, tm, LANES), lambda i, base=_slab_base(name): (base // gs, i, 0))
    return pl.pallas_call(
        _merge_kernel,
        out_shape=jax.ShapeDtypeStruct((m, d), BF16),
        grid=(m // tm,),
        in_specs=[x_spec, x_spec, w_spec, w_spec, g_spec("gr"), g_spec("gd")],
        out_specs=pl.BlockSpec((tm, d), lambda i: (i, 0)),
        compiler_params=_params(1),
        name="gated_merge",
    )(o_ret, o_diff, w_ret_up_bf, w_diff_up_bf, gates, gates)


def _outproj_kernel(mx_ref, w_ref, x_ref, fw_ref, o_ref):
    half = mx_ref.shape[0] // 2
    for r in range(2):
        rows = slice(r * half, (r + 1) * half)
        h = x_ref[rows, :] + jnp.dot(mx_ref[rows, :], w_ref[...], preferred_element_type=F32)
        ms = jnp.mean(h * h, axis=-1, keepdims=True)
        o_ref[rows, :] = h * lax.rsqrt(ms + NORM_EPS) * fw_ref[...]


def _output_projection(mixed, w_out_bf, x2, final_w, tm=512):
    m, d = x2.shape
    return pl.pallas_call(
        _outproj_kernel,
        out_shape=jax.ShapeDtypeStruct((m, d), F32),
        grid=(m // tm,),
        in_specs=[pl.BlockSpec((tm, d), lambda i: (i, 0)),
                  pl.BlockSpec((d, d), lambda i: (0, 0), pipeline_mode=pl.Buffered(1)),
                  pl.BlockSpec((tm, d), lambda i: (i, 0)),
                  pl.BlockSpec((1, d), lambda i: (0, 0))],
        out_specs=pl.BlockSpec((tm, d), lambda i: (i, 0)),
        compiler_params=_params(1),
        name="output_projection",
    )(mixed, w_out_bf, x2, final_w)


def _rotary_tables(s, inv_freq):
    pos = np.arange(s, dtype=np.float32)
    ang = pos[:, None] * inv_freq[None, :].astype(np.float32)
    emb = np.concatenate([ang, ang], axis=-1)
    half = emb.shape[-1] // 2
    sign = np.concatenate([-np.ones((half,), np.float32), np.ones((half,), np.float32)])
    return np.cos(emb), np.sin(emb) * sign[None, :]


def kernel(x, norm_w, w_in, w_ret_up, w_diff_up, w_out, lambda_q1, lambda_k1, lambda_q2,
           lambda_k2, subln_w, final_norm_w):
    b, s, d = x.shape
    assert d == D_MODEL and w_in.shape == (1, D_MODEL, IN_WIDTH)
    x2 = x.reshape(b * s, d)

    f32 = np.float32
    ret_inv_freq = np.exp(f32(-math.log(10000.0)) * np.linspace(0.0, 1.0, RET_QK_DIM // 2, dtype=f32))
    rope_inv_freq = f32(ROPE_THETA) ** (-np.arange(0, DIFF_HEAD_DIM, 2, dtype=f32) / f32(DIFF_HEAD_DIM))
    ret_cos, ret_sin = _rotary_tables(s, ret_inv_freq)
    rope_cos, rope_sin = _rotary_tables(s, rope_inv_freq)
    log_gamma = jnp.asarray(np.log1p(-np.exp2(f32(-5.0) - np.arange(RET_HEADS, dtype=f32))))

    ret_k_scale = f32(RET_QK_DIM ** -0.5)
    diff_q_scale = f32((DIFF_HEAD_DIM ** -0.5) * math.log2(math.e))
    cos_tables = jnp.asarray(
        np.stack([ret_cos, ret_cos * ret_k_scale, rope_cos * diff_q_scale, rope_cos]))
    sin_tables = jnp.asarray(
        np.stack([ret_sin, ret_sin * ret_k_scale, rope_sin * diff_q_scale, rope_sin]))

    xn = _input_rmsnorm(x2, norm_w[0][None, :])
    proj = {kind: _input_projection(xn, w_in[0], cos_tables, sin_tables, kind) for kind in _KINDS}
    by_pos = lambda a: a.reshape(a.shape[0], b, s, LANES)
    rot4, plain4, silu4 = by_pos(proj["rotary"]), by_pos(proj["plain"]), by_pos(proj["silu"])

    o_ret = _retention(rot4, plain4, silu4, log_gamma)
    o_diff, (w_ret_up_bf, w_diff_up_bf, w_out_bf) = _diff_attention(
        rot4, plain4, silu4, lambda_q1, lambda_k1, lambda_q2, lambda_k2, subln_w,
        (w_ret_up[0], w_diff_up[0], w_out[0]))
    o_ret = o_ret.reshape(RET_HEADS, b * s, RET_V_DIM)
    o_diff = o_diff.reshape(DIFF_HEADS, b * s, DIFF_V_DIM)

    mixed = _gated_merge(o_ret, o_diff, w_ret_up_bf, w_diff_up_bf, proj["sigmoid"])
    out = _output_projection(mixed, w_out_bf, x2, final_norm_w[None, :])
    return out.reshape(b, s, d)
```

```python
import functools
import math

import jax
import jax.numpy as jnp
import numpy as np
from jax import lax
from jax.experimental import pallas as pl
from jax.experimental.pallas import tpu as pltpu

F32 = jnp.float32
BF16 = jnp.bfloat16

D_MODEL = 2048
RET_QK_DIM = 128
RET_V_DIM = 256
RET_HEADS = D_MODEL // RET_V_DIM
DIFF_HEAD_DIM = 128
DIFF_V_DIM = 2 * DIFF_HEAD_DIM
DIFF_HEADS = D_MODEL // DIFF_V_DIM
ROPE_THETA = 10000.0
NORM_EPS = 1e-6
SUBLN_EPS = 1e-5
LAMBDA_INIT = 0.8 - 0.6 * math.exp(-0.3 * 0)

RET_QK_WIDTH = RET_HEADS * RET_QK_DIM
RET_WIDTH = RET_HEADS * RET_V_DIM
DIFF_QK_WIDTH = DIFF_HEADS * 2 * DIFF_HEAD_DIM
DIFF_WIDTH = DIFF_HEADS * DIFF_V_DIM
IN_WIDTH = 2 * RET_QK_WIDTH + 2 * RET_WIDTH + 2 * DIFF_QK_WIDTH + 2 * DIFF_WIDTH + 2 * D_MODEL

LANES = 128

_SECTIONS = (("rq", RET_QK_WIDTH, "rotary", 0), ("rk", RET_QK_WIDTH, "rotary", 1),
             ("rv", RET_WIDTH, "plain", 0), ("rz", RET_WIDTH, "silu", 0),
             ("dq", DIFF_QK_WIDTH, "rotary", 2), ("dk", DIFF_QK_WIDTH, "rotary", 3),
             ("dv", DIFF_WIDTH, "plain", 0), ("dz", DIFF_WIDTH, "silu", 0),
             ("gr", D_MODEL, "sigmoid", 0), ("gd", D_MODEL, "sigmoid", 0))
assert sum(sec[1] for sec in _SECTIONS) == IN_WIDTH
_KINDS = ("rotary", "plain", "silu", "sigmoid")


def _slab_base(name):
    kind = next(sec[2] for sec in _SECTIONS if sec[0] == name)
    base = 0
    for sec_name, width, sec_kind, _ in _SECTIONS:
        if sec_name == name:
            return base
        if sec_kind == kind:
            base += width // LANES
    raise KeyError(name)

VMEM_LIMIT = 56 * 1024 * 1024

RET_CHUNK = 256
ATT_BLOCK = 256
ATT_SKEW = 1
MIX_PATTERN = (1, 1)


def _params(n_axes, vmem=VMEM_LIMIT, flags=None):
    return pltpu.CompilerParams(dimension_semantics=("arbitrary",) * n_axes,
                                vmem_limit_bytes=vmem, flags=flags)


def _rmsnorm_kernel(x_ref, w_ref, o_ref):
    x = x_ref[...]
    ms = jnp.mean(x * x, axis=-1, keepdims=True)
    o_ref[...] = (x * lax.rsqrt(ms + NORM_EPS) * w_ref[...]).astype(o_ref.dtype)


def _input_rmsnorm(x2, w, tm=512):
    m, d = x2.shape
    return pl.pallas_call(
        _rmsnorm_kernel,
        out_shape=jax.ShapeDtypeStruct((m, d), BF16),
        grid=(m // tm,),
        in_specs=[pl.BlockSpec((tm, d), lambda i: (i, 0)),
                  pl.BlockSpec((1, d), lambda i: (0, 0))],
        out_specs=pl.BlockSpec((tm, d), lambda i: (i, 0)),
        compiler_params=_params(1),
        name="input_rmsnorm",
    )(x2, w)


def _rotary(x, cos, sin_signed):
    return x * cos + pltpu.roll(x, x.shape[-1] // 2, axis=x.ndim - 1) * sin_signed


def _sigmoid(g):
    return 0.5 * jnp.tanh(0.5 * g) + 0.5


def _silu(z):
    h = 0.5 * z
    return h * jnp.tanh(h) + h


def _select(j, values):
    return sum(jnp.where(j == t, v, 0) for t, v in enumerate(values) if v)


def _projection_kernel(*refs, epilogue):
    if epilogue == "rotary":
        x_ref, w_ref, cos_ref, sin_ref, o_ref = refs
        fn = lambda a: _rotary(a, cos_ref[...], sin_ref[...])
    else:
        x_ref, w_ref, o_ref = refs
        fn = {"plain": lambda a: a, "silu": _silu, "sigmoid": _sigmoid}[epilogue]
    acc = jnp.dot(x_ref[...], w_ref[...].astype(BF16), preferred_element_type=F32)
    for s in range(o_ref.shape[0]):
        o_ref[s] = fn(acc[:, s * LANES:(s + 1) * LANES]).astype(o_ref.dtype)


def _input_projection(xn, w_in, cos_tables, sin_tables, epilogue, tm=2048, tn=1024):
    m, d = xn.shape
    s_len = cos_tables.shape[1]
    col_tiles, tables = [], []
    col = 0
    for _, width, kind, table in _SECTIONS:
        assert width % tn == 0
        if kind == epilogue:
            col_tiles += [col // tn + t for t in range(width // tn)]
            tables += [table] * (width // tn)
        col += width

    in_specs = [pl.BlockSpec((tm, d), lambda j, i: (i, 0)),
                pl.BlockSpec((d, tn), lambda j, i: (0, _select(j, col_tiles)))]
    operands = [xn, w_in]
    if epilogue == "rotary":
        table_spec = pl.BlockSpec((None, tm, LANES),
                                  lambda j, i: (_select(j, tables), i % (s_len // tm), 0))
        in_specs += [table_spec, table_spec]
        operands += [cos_tables, sin_tables]
    return pl.pallas_call(
        functools.partial(_projection_kernel, epilogue=epilogue),
        out_shape=jax.ShapeDtypeStruct((len(col_tiles) * tn // LANES, m, LANES), BF16),
        grid=(len(col_tiles), m // tm),
        in_specs=in_specs,
        out_specs=pl.BlockSpec((tn // LANES, tm, LANES), lambda j, i: (j, i, 0)),
        compiler_params=_params(2),
        name="projection_" + epilogue,
    )(*operands)


def _retention_pieces(lg, q_ref, k_ref, v_ref, z_ref, o_ref, qx_ref, kz_ref, vc_ref):
    s_len = q_ref.shape[0]
    c = RET_CHUNK
    n_chunks = s_len // c

    rowq = lax.broadcasted_iota(jnp.int32, (c, RET_QK_DIM), 0).astype(F32)
    zeta = jnp.exp(lg * (c - 1.0 - rowq))
    xi = jnp.exp(lg * (rowq + 1.0))
    for n in range(n_chunks):
        sl = slice(n * c, (n + 1) * c)
        qx_ref[sl, :] = (q_ref[sl, :].astype(F32) * xi).astype(BF16)
        kz_ref[sl, :] = (k_ref[sl, :].astype(F32) * zeta).astype(BF16)
    vc_ref[...] = jnp.concatenate([v_ref[0], v_ref[1]], axis=-1)

    row = lax.broadcasted_iota(jnp.int32, (c, c), 0).astype(F32)
    col = lax.broadcasted_iota(jnp.int32, (c, c), 1).astype(F32)
    rel = row - col
    decay_mask = jnp.where(rel >= 0.0, jnp.exp(lg * jnp.maximum(rel, 0.0)), 0.0)
    chunk_decay = jnp.exp(jnp.zeros((1, RET_V_DIM), F32) + lg * c)
    yield

    state = jnp.zeros((RET_QK_DIM, RET_V_DIM), F32)
    for n in range(n_chunks):
        sl = slice(n * c, (n + 1) * c)
        vc = vc_ref[sl, :]
        scores = lax.dot_general(q_ref[sl, :], k_ref[sl, :], (((1,), (1,)), ((), ())),
                                 preferred_element_type=F32) * decay_mask
        o = jnp.dot(scores.astype(BF16), vc, preferred_element_type=F32)
        o = o + jnp.dot(qx_ref[sl, :], state.astype(BF16), preferred_element_type=F32)
        if n + 1 < n_chunks:
            kv = lax.dot_general(kz_ref[sl, :], vc, (((0,), (0,)), ((), ())),
                                 preferred_element_type=F32)
            state = state * chunk_decay + kv

        ms = jnp.mean(o * o, axis=-1, keepdims=True)
        o = o * lax.rsqrt(ms + NORM_EPS)
        silu_z = jnp.concatenate([z_ref[0, sl, :], z_ref[1, sl, :]], axis=-1).astype(F32)
        o_ref[sl, :] = (o * silu_z).astype(o_ref.dtype)
        yield


def _diff_attn_pieces(q_ref, k_ref, v_ref, z_ref, lq1_ref, lk1_ref, lq2_ref, lk2_ref, subln_ref,
                      o_ref, vc_ref):
    t = ATT_BLOCK
    s_len = vc_ref.shape[0]
    vc_ref[...] = jnp.concatenate([v_ref[0], v_ref[1]], axis=-1)

    lam = (jnp.exp(jnp.sum(lq1_ref[...] * lk1_ref[...], axis=-1, keepdims=True))
           - jnp.exp(jnp.sum(lq2_ref[...] * lk2_ref[...], axis=-1, keepdims=True))
           + LAMBDA_INIT)
    causal = (lax.broadcasted_iota(jnp.int32, (t, t), 1)
              <= lax.broadcasted_iota(jnp.int32, (t, t), 0))
    nt_dims = (((1,), (1,)), ((), ()))

    units = [(n, c) for n in reversed(range(s_len // t)) for c in range(2)]

    def scores(u):
        n, c = units[u]
        r0 = n * t
        q = q_ref[c, r0:r0 + t, :]
        s_diag = jnp.where(causal,
                           lax.dot_general(q, k_ref[c, r0:r0 + t, :], nt_dims,
                                           preferred_element_type=F32), -jnp.inf)
        m = jnp.max(s_diag, axis=-1, keepdims=True)
        s_off = None
        if n > 0:
            s_off = lax.dot_general(q, k_ref[c, 0:r0, :], nt_dims, preferred_element_type=F32)
            m = jnp.maximum(m, jnp.max(s_off, axis=-1, keepdims=True))
        return s_diag, s_off, m

    def probs(sc):
        s_diag, s_off, m = sc
        p_diag = jnp.exp2(s_diag - m)
        l = jnp.sum(p_diag, axis=-1, keepdims=True)
        p_off = None
        if s_off is not None:
            p_off = jnp.exp2(s_off - m)
            l = l + jnp.sum(p_off, axis=-1, keepdims=True)
            p_off = p_off.astype(BF16)
        return p_diag.astype(BF16), p_off, l

    def weighted(u, pr):
        n, _ = units[u]
        r0 = n * t
        p_diag, p_off, l = pr
        acc = jnp.dot(p_diag, vc_ref[r0:r0 + t, :], preferred_element_type=F32)
        if p_off is not None:
            acc = acc + jnp.dot(p_off, vc_ref[0:r0, :], preferred_element_type=F32)
        return acc, l

    def finish(n, res):
        r0 = n * t
        o = res[0][0] * (1.0 / res[0][1]) - res[1][0] * (lam / res[1][1])
        ms = jnp.mean(o * o, axis=-1, keepdims=True)
        o = o * lax.rsqrt(ms + SUBLN_EPS) * subln_ref[...] * (1.0 - LAMBDA_INIT)
        silu_z = jnp.concatenate([z_ref[0, r0:r0 + t, :], z_ref[1, r0:r0 + t, :]],
                                 axis=-1).astype(F32)
        o_ref[r0:r0 + t, :] = (o * silu_z).astype(o_ref.dtype)

    skew = ATT_SKEW
    sc, pr, res = {}, {}, {}
    for step in range(len(units) + 2 * skew):
        u_sc, u_pr, u_pv = step, step - skew, step - 2 * skew
        if 0 <= u_sc < len(units):
            sc[u_sc] = scores(u_sc)
        if 0 <= u_pr < len(units):
            pr[u_pr] = probs(sc.pop(u_pr))
        if 0 <= u_pv < len(units):
            res[u_pv] = weighted(u_pv, pr.pop(u_pv))
            if units[u_pv][1] == 1:
                finish(units[u_pv][0], (res.pop(u_pv - 1), res.pop(u_pv)))
        yield


def _mixers_kernel(lg_ref, rq_ref, rk_ref, rv_ref, rz_ref, dq_ref, dk_ref, dv_ref, dz_ref,
                   lq1_ref, lk1_ref, lq2_ref, lk2_ref, subln_ref, s1_ref, s2_ref, s3_ref,
                   oret_ref, odiff_ref, s1bf_ref, s2bf_ref, s3bf_ref,
                   qx_ref, kz_ref, rvc_ref, dvc_ref):
    for src, dst in ((s1_ref, s1bf_ref), (s2_ref, s2bf_ref), (s3_ref, s3bf_ref)):
        dst[...] = src[...].astype(BF16)

    retention = _retention_pieces(lg_ref[pl.program_id(1)], rq_ref, rk_ref, rv_ref, rz_ref,
                                  oret_ref, qx_ref, kz_ref, rvc_ref)
    attention = _diff_attn_pieces(dq_ref, dk_ref, dv_ref, dz_ref, lq1_ref, lk1_ref, lq2_ref,
                                  lk2_ref, subln_ref, odiff_ref, dvc_ref)
    live = [attention, retention]
    while live:
        for gen, count in zip((attention, retention), MIX_PATTERN):
            for _ in range(count):
                if gen in live and next(gen, "done") == "done":
                    live.remove(gen)


def _token_mixers(rot4, plain4, silu4, log_gamma, lq1, lk1, lq2, lk2, subln_w, side_weights):
    _, b, s, _ = rot4.shape
    assert RET_HEADS == DIFF_HEADS
    one = lambda name: pl.BlockSpec(
        (None, None, s, LANES), lambda bi, h, base=_slab_base(name): (base + h, bi, 0, 0))
    pair = lambda name: pl.BlockSpec(
        (2, None, s, LANES), lambda bi, h, base=_slab_base(name): (base // 2 + h, bi, 0, 0))
    vec = lambda n: pl.BlockSpec((1, n), lambda bi, h: (0, 0))
    side_rows, side_cols = side_weights[0].shape
    assert all(w.shape == (side_rows, side_cols) for w in side_weights)
    band = side_rows // (b * DIFF_HEADS)
    side_spec = pl.BlockSpec((band, side_cols), lambda bi, h: (bi * DIFF_HEADS + h, 0))
    head_out = pl.BlockSpec((None, None, s, RET_V_DIM), lambda bi, h: (h, bi, 0, 0))
    outs = pl.pallas_call(
        _mixers_kernel,
        out_shape=[jax.ShapeDtypeStruct((RET_HEADS, b, s, RET_V_DIM), BF16),
                   jax.ShapeDtypeStruct((DIFF_HEADS, b, s, DIFF_V_DIM), BF16)]
                  + [jax.ShapeDtypeStruct((side_rows, side_cols), BF16)] * len(side_weights),
        grid=(b, DIFF_HEADS),
        in_specs=[pl.BlockSpec(memory_space=pltpu.SMEM),
                  one("rq"), one("rk"), pair("rv"), pair("rz"),
                  pair("dq"), pair("dk"), pair("dv"), pair("dz"),
                  vec(DIFF_HEAD_DIM), vec(DIFF_HEAD_DIM), vec(DIFF_HEAD_DIM), vec(DIFF_HEAD_DIM),
                  vec(DIFF_V_DIM)] + [side_spec] * len(side_weights),
        out_specs=[head_out, head_out] + [side_spec] * len(side_weights),
        scratch_shapes=[pltpu.VMEM((s, RET_QK_DIM), BF16)] * 2
                       + [pltpu.VMEM((s, RET_V_DIM), BF16), pltpu.VMEM((s, DIFF_V_DIM), BF16)],
        compiler_params=_params(2),
        name="token_mixers",
    )(log_gamma, rot4, rot4, plain4, silu4, rot4, rot4, plain4, silu4,
      lq1, lk1, lq2, lk2, subln_w, *side_weights)
    return outs[0], outs[1], outs[2:]


def _merge_kernel(a_ref, b_ref, w1_ref, w2_ref, g1_ref, g2_ref, o_ref):
    kw = a_ref.shape[2]

    def up(x_ref, w_ref):
        acc = None
        for k in range(x_ref.shape[0]):
            part = jnp.dot(x_ref[k], w_ref[k * kw:(k + 1) * kw, :], preferred_element_type=F32)
            acc = part if acc is None else acc + part
        return acc

    up1 = up(a_ref, w1_ref)
    up2 = up(b_ref, w2_ref)
    for j in range(g1_ref.shape[0]):
        sl = slice(j * LANES, (j + 1) * LANES)
        o_ref[:, sl] = (g1_ref[j].astype(F32) * up1[:, sl]
                        + g2_ref[j].astype(F32) * up2[:, sl]).astype(o_ref.dtype)


def _gated_merge(o_ret, o_diff, w_ret_up_bf, w_diff_up_bf, gates, tm=512):
    nk, m, kw = o_ret.shape
    d = w_ret_up_bf.shape[1]
    gs = d // LANES
    x_spec = pl.BlockSpec((nk, tm, kw), lambda i: (0, i, 0))
    w_spec = pl.BlockSpec((nk * kw, d), lambda i: (0, 0), pipeline_mode=pl.Buffered(1))
    g_spec = lambda name: pl.BlockSpec(
        (gs, tm, LANES), lambda i, base=_slab_base(name): (base // gs, i, 0))
    return pl.pallas_call(
        _merge_kernel,
        out_shape=jax.ShapeDtypeStruct((m, d), BF16),
        grid=(m // tm,),
        in_specs=[x_spec, x_spec, w_spec, w_spec, g_spec("gr"), g_spec("gd")],
        out_specs=pl.BlockSpec((tm, d), lambda i: (i, 0)),
        compiler_params=_params(1),
        name="gated_merge",
    )(o_ret, o_diff, w_ret_up_bf, w_diff_up_bf, gates, gates)


def _outproj_kernel(mx_ref, w_ref, x_ref, fw_ref, o_ref):
    half = mx_ref.shape[0] // 2
    for r in range(2):
        rows = slice(r * half, (r + 1) * half)
        h = x_ref[rows, :] + jnp.dot(mx_ref[rows, :], w_ref[...], preferred_element_type=F32)
        ms = jnp.mean(h * h, axis=-1, keepdims=True)
        o_ref[rows, :] = h * lax.rsqrt(ms + NORM_EPS) * fw_ref[...]


def _output_projection(mixed, w_out_bf, x2, final_w, tm=512):
    m, d = x2.shape
    return pl.pallas_call(
        _outproj_kernel,
        out_shape=jax.ShapeDtypeStruct((m, d), F32),
        grid=(m // tm,),
        in_specs=[pl.BlockSpec((tm, d), lambda i: (i, 0)),
                  pl.BlockSpec((d, d), lambda i: (0, 0), pipeline_mode=pl.Buffered(1)),
                  pl.BlockSpec((tm, d), lambda i: (i, 0)),
                  pl.BlockSpec((1, d), lambda i: (0, 0))],
        out_specs=pl.BlockSpec((tm, d), lambda i: (i, 0)),
        compiler_params=_params(1),
        name="output_projection",
    )(mixed, w_out_bf, x2, final_w)


def _rotary_tables(s, inv_freq):
    pos = np.arange(s, dtype=np.float32)
    ang = pos[:, None] * inv_freq[None, :].astype(np.float32)
    emb = np.concatenate([ang, ang], axis=-1)
    half = emb.shape[-1] // 2
    sign = np.concatenate([-np.ones((half,), np.float32), np.ones((half,), np.float32)])
    return np.cos(emb), np.sin(emb) * sign[None, :]


def kernel(x, norm_w, w_in, w_ret_up, w_diff_up, w_out, lambda_q1, lambda_k1, lambda_q2,
           lambda_k2, subln_w, final_norm_w):
    b, s, d = x.shape
    assert d == D_MODEL and w_in.shape == (1, D_MODEL, IN_WIDTH)
    x2 = x.reshape(b * s, d)

    f32 = np.float32
    ret_inv_freq = np.exp(f32(-math.log(10000.0)) * np.linspace(0.0, 1.0, RET_QK_DIM // 2, dtype=f32))
    rope_inv_freq = f32(ROPE_THETA) ** (-np.arange(0, DIFF_HEAD_DIM, 2, dtype=f32) / f32(DIFF_HEAD_DIM))
    ret_cos, ret_sin = _rotary_tables(s, ret_inv_freq)
    rope_cos, rope_sin = _rotary_tables(s, rope_inv_freq)
    log_gamma = jnp.asarray(np.log1p(-np.exp2(f32(-5.0) - np.arange(RET_HEADS, dtype=f32))))

    ret_k_scale = f32(RET_QK_DIM ** -0.5)
    diff_q_scale = f32((DIFF_HEAD_DIM ** -0.5) * math.log2(math.e))
    cos_tables = jnp.asarray(
        np.stack([ret_cos, ret_cos * ret_k_scale, rope_cos * diff_q_scale, rope_cos]))
    sin_tables = jnp.asarray(
        np.stack([ret_sin, ret_sin * ret_k_scale, rope_sin * diff_q_scale, rope_sin]))

    xn = _input_rmsnorm(x2, norm_w[0][None, :])
    proj = {kind: _input_projection(xn, w_in[0], cos_tables, sin_tables, kind) for kind in _KINDS}
    by_pos = lambda a: a.reshape(a.shape[0], b, s, LANES)
    rot4, plain4, silu4 = by_pos(proj["rotary"]), by_pos(proj["plain"]), by_pos(proj["silu"])

    o_ret, o_diff, (w_ret_up_bf, w_diff_up_bf, w_out_bf) = _token_mixers(
        rot4, plain4, silu4, log_gamma, lambda_q1, lambda_k1, lambda_q2, lambda_k2, subln_w,
        (w_ret_up[0], w_diff_up[0], w_out[0]))
    o_ret = o_ret.reshape(RET_HEADS, b * s, RET_V_DIM)
    o_diff = o_diff.reshape(DIFF_HEADS, b * s, DIFF_V_DIM)

    mixed = _gated_merge(o_ret, o_diff, w_ret_up_bf, w_diff_up_bf, proj["sigmoid"])
    out = _output_projection(mixed, w_out_bf, x2, final_norm_w[None, :])
    return out.reshape(b, s, d)
```

```python
import functools
import math

import jax
import jax.numpy as jnp
import numpy as np
from jax import lax
from jax.experimental import pallas as pl
from jax.experimental.pallas import tpu as pltpu

F32 = jnp.float32
BF16 = jnp.bfloat16

D_MODEL = 2048
RET_QK_DIM = 128
RET_V_DIM = 256
RET_HEADS = D_MODEL // RET_V_DIM
DIFF_HEAD_DIM = 128
DIFF_V_DIM = 2 * DIFF_HEAD_DIM
DIFF_HEADS = D_MODEL // DIFF_V_DIM
ROPE_THETA = 10000.0
NORM_EPS = 1e-6
SUBLN_EPS = 1e-5
LAMBDA_INIT = 0.8 - 0.6 * math.exp(-0.3 * 0)

RET_QK_WIDTH = RET_HEADS * RET_QK_DIM
RET_WIDTH = RET_HEADS * RET_V_DIM
DIFF_QK_WIDTH = DIFF_HEADS * 2 * DIFF_HEAD_DIM
DIFF_WIDTH = DIFF_HEADS * DIFF_V_DIM
IN_WIDTH = 2 * RET_QK_WIDTH + 2 * RET_WIDTH + 2 * DIFF_QK_WIDTH + 2 * DIFF_WIDTH + 2 * D_MODEL

LANES = 128

_SECTIONS = (("rq", RET_QK_WIDTH, "rotary", 0), ("rk", RET_QK_WIDTH, "rotary", 1),
             ("rv", RET_WIDTH, "plain", 0), ("rz", RET_WIDTH, "silu", 0),
             ("dq", DIFF_QK_WIDTH, "rotary", 2), ("dk", DIFF_QK_WIDTH, "rotary", 3),
             ("dv", DIFF_WIDTH, "plain", 0), ("dz", DIFF_WIDTH, "silu", 0),
             ("gr", D_MODEL, "sigmoid", 0), ("gd", D_MODEL, "sigmoid", 0))
assert sum(sec[1] for sec in _SECTIONS) == IN_WIDTH
_KINDS = ("rotary", "plain", "silu", "sigmoid")


def _slab_base(name):
    kind = next(sec[2] for sec in _SECTIONS if sec[0] == name)
    base = 0
    for sec_name, width, sec_kind, _ in _SECTIONS:
        if sec_name == name:
            return base
        if sec_kind == kind:
            base += width // LANES
    raise KeyError(name)

VMEM_LIMIT = 56 * 1024 * 1024

RET_CHUNK = 256
ATT_BLOCK = 256
ATT_SKEW = 1
PROJ_ROW_BANDS = ((1, 4), (1, 4), (1, 4), (1, 8), (1, 16), (1, 16))
MERGE_ROW_BANDS = ((1, 2), (1, 4), (1, 4))
OUT_ROW_BANDS = ((1, 4), (1, 4), (1, 4), (1, 8), (1, 8))
MIX_PATTERN = (1, 1)


def _params(n_axes, vmem=VMEM_LIMIT, flags=None):
    return pltpu.CompilerParams(dimension_semantics=("arbitrary",) * n_axes,
                                vmem_limit_bytes=vmem, flags=flags)


def _rmsnorm_kernel(x_ref, w_ref, o_ref):
    x = x_ref[...]
    ms = jnp.mean(x * x, axis=-1, keepdims=True)
    o_ref[...] = (x * lax.rsqrt(ms + NORM_EPS) * w_ref[...]).astype(o_ref.dtype)


def _input_rmsnorm(x2, w, tm=512):
    m, d = x2.shape
    return pl.pallas_call(
        _rmsnorm_kernel,
        out_shape=jax.ShapeDtypeStruct((m, d), BF16),
        grid=(m // tm,),
        in_specs=[pl.BlockSpec((tm, d), lambda i: (i, 0)),
                  pl.BlockSpec((1, d), lambda i: (0, 0))],
        out_specs=pl.BlockSpec((tm, d), lambda i: (i, 0)),
        compiler_params=_params(1),
        name="input_rmsnorm",
    )(x2, w)


def _rotary(x, cos, sin_signed):
    return x * cos + pltpu.roll(x, x.shape[-1] // 2, axis=x.ndim - 1) * sin_signed


def _sigmoid(g):
    return 0.5 * jnp.tanh(0.5 * g) + 0.5


def _silu(z):
    h = 0.5 * z
    return h * jnp.tanh(h) + h


def _row_bands(tm, fractions):
    bounds = [0]
    for num, den in fractions:
        bounds.append(bounds[-1] + tm * num // den)
    assert bounds[-1] == tm
    return [slice(r0, r1) for r0, r1 in zip(bounds[:-1], bounds[1:])]


def _select(j, values):
    return sum(jnp.where(j == t, v, 0) for t, v in enumerate(values) if v)


def _projection_kernel(*refs, epilogue):
    if epilogue == "rotary":
        x_ref, w_ref, cos_ref, sin_ref, o_ref = refs
        fn = lambda a, rows: _rotary(a, cos_ref[rows, :], sin_ref[rows, :])
    else:
        x_ref, w_ref, o_ref = refs
        fn = {"plain": lambda a, rows: a, "silu": lambda a, rows: _silu(a),
              "sigmoid": lambda a, rows: _sigmoid(a)}[epilogue]
    w = w_ref[...].astype(BF16)
    for rows in _row_bands(x_ref.shape[0], PROJ_ROW_BANDS):
        acc = jnp.dot(x_ref[rows, :], w, preferred_element_type=F32)
        for s in range(o_ref.shape[0]):
            o_ref[s, rows, :] = fn(acc[:, s * LANES:(s + 1) * LANES], rows).astype(o_ref.dtype)


def _input_projection(xn, w_in, cos_tables, sin_tables, epilogue, tm=2048, tn=1024):
    m, d = xn.shape
    s_len = cos_tables.shape[1]
    col_tiles, tables = [], []
    col = 0
    for _, width, kind, table in _SECTIONS:
        assert width % tn == 0
        if kind == epilogue:
            col_tiles += [col // tn + t for t in range(width // tn)]
            tables += [table] * (width // tn)
        col += width

    in_specs = [pl.BlockSpec((tm, d), lambda j, i: (i, 0)),
                pl.BlockSpec((d, tn), lambda j, i: (0, _select(j, col_tiles)))]
    operands = [xn, w_in]
    if epilogue == "rotary":
        table_spec = pl.BlockSpec((None, tm, LANES),
                                  lambda j, i: (_select(j, tables), i % (s_len // tm), 0))
        in_specs += [table_spec, table_spec]
        operands += [cos_tables, sin_tables]
    return pl.pallas_call(
        functools.partial(_projection_kernel, epilogue=epilogue),
        out_shape=jax.ShapeDtypeStruct((len(col_tiles) * tn // LANES, m, LANES), BF16),
        grid=(len(col_tiles), m // tm),
        in_specs=in_specs,
        out_specs=pl.BlockSpec((tn // LANES, tm, LANES), lambda j, i: (j, i, 0)),
        compiler_params=_params(2),
        name="projection_" + epilogue,
    )(*operands)


def _retention_pieces(lg, q_ref, k_ref, v_ref, z_ref, o_ref, qx_ref, kz_ref, vc_ref):
    s_len = q_ref.shape[0]
    c = RET_CHUNK
    n_chunks = s_len // c

    rowq = lax.broadcasted_iota(jnp.int32, (c, RET_QK_DIM), 0).astype(F32)
    zeta = jnp.exp(lg * (c - 1.0 - rowq))
    xi = jnp.exp(lg * (rowq + 1.0))
    for n in range(n_chunks):
        sl = slice(n * c, (n + 1) * c)
        qx_ref[sl, :] = (q_ref[sl, :].astype(F32) * xi).astype(BF16)
        kz_ref[sl, :] = (k_ref[sl, :].astype(F32) * zeta).astype(BF16)
    vc_ref[...] = jnp.concatenate([v_ref[0], v_ref[1]], axis=-1)

    row = lax.broadcasted_iota(jnp.int32, (c, c), 0).astype(F32)
    col = lax.broadcasted_iota(jnp.int32, (c, c), 1).astype(F32)
    rel = row - col
    decay_mask = jnp.where(rel >= 0.0, jnp.exp(lg * jnp.maximum(rel, 0.0)), 0.0)
    chunk_decay = jnp.exp(jnp.zeros((1, RET_V_DIM), F32) + lg * c)
    yield

    state = jnp.zeros((RET_QK_DIM, RET_V_DIM), F32)
    for n in range(n_chunks):
        sl = slice(n * c, (n + 1) * c)
        vc = vc_ref[sl, :]
        scores = lax.dot_general(q_ref[sl, :], k_ref[sl, :], (((1,), (1,)), ((), ())),
                                 preferred_element_type=F32) * decay_mask
        o = jnp.dot(scores.astype(BF16), vc, preferred_element_type=F32)
        o = o + jnp.dot(qx_ref[sl, :], state.astype(BF16), preferred_element_type=F32)
        if n + 1 < n_chunks:
            kv = lax.dot_general(kz_ref[sl, :], vc, (((0,), (0,)), ((), ())),
                                 preferred_element_type=F32)
            state = state * chunk_decay + kv

        ms = jnp.mean(o * o, axis=-1, keepdims=True)
        o = o * lax.rsqrt(ms + NORM_EPS)
        silu_z = jnp.concatenate([z_ref[0, sl, :], z_ref[1, sl, :]], axis=-1).astype(F32)
        o_ref[sl, :] = (o * silu_z).astype(o_ref.dtype)
        yield


def _diff_attn_pieces(q_ref, k_ref, v_ref, z_ref, lq1_ref, lk1_ref, lq2_ref, lk2_ref, subln_ref,
                      o_ref, vc_ref):
    t = ATT_BLOCK
    s_len = vc_ref.shape[0]
    vc_ref[...] = jnp.concatenate([v_ref[0], v_ref[1]], axis=-1)

    lam = (jnp.exp(jnp.sum(lq1_ref[...] * lk1_ref[...], axis=-1, keepdims=True))
           - jnp.exp(jnp.sum(lq2_ref[...] * lk2_ref[...], axis=-1, keepdims=True))
           + LAMBDA_INIT)
    causal = (lax.broadcasted_iota(jnp.int32, (t, t), 1)
              <= lax.broadcasted_iota(jnp.int32, (t, t), 0))
    nt_dims = (((1,), (1,)), ((), ()))

    units = [(n, c) for n in reversed(range(s_len // t)) for c in range(2)]

    def scores(u):
        n, c = units[u]
        r0 = n * t
        q = q_ref[c, r0:r0 + t, :]
        s_diag = jnp.where(causal,
                           lax.dot_general(q, k_ref[c, r0:r0 + t, :], nt_dims,
                                           preferred_element_type=F32), -jnp.inf)
        m = jnp.max(s_diag, axis=-1, keepdims=True)
        s_off = None
        if n > 0:
            s_off = lax.dot_general(q, k_ref[c, 0:r0, :], nt_dims, preferred_element_type=F32)
            m = jnp.maximum(m, jnp.max(s_off, axis=-1, keepdims=True))
        return s_diag, s_off, m

    def probs(sc):
        s_diag, s_off, m = sc
        p_diag = jnp.exp2(s_diag - m)
        l = jnp.sum(p_diag, axis=-1, keepdims=True)
        p_off = None
        if s_off is not None:
            p_off = jnp.exp2(s_off - m)
            l = l + jnp.sum(p_off, axis=-1, keepdims=True)
            p_off = p_off.astype(BF16)
        return p_diag.astype(BF16), p_off, l

    def weighted(u, pr):
        n, _ = units[u]
        r0 = n * t
        p_diag, p_off, l = pr
        acc = jnp.dot(p_diag, vc_ref[r0:r0 + t, :], preferred_element_type=F32)
        if p_off is not None:
            acc = acc + jnp.dot(p_off, vc_ref[0:r0, :], preferred_element_type=F32)
        return acc, l

    def finish(n, res):
        r0 = n * t
        o = res[0][0] * (1.0 / res[0][1]) - res[1][0] * (lam / res[1][1])
        ms = jnp.mean(o * o, axis=-1, keepdims=True)
        o = o * lax.rsqrt(ms + SUBLN_EPS) * subln_ref[...] * (1.0 - LAMBDA_INIT)
        silu_z = jnp.concatenate([z_ref[0, r0:r0 + t, :], z_ref[1, r0:r0 + t, :]],
                                 axis=-1).astype(F32)
        o_ref[r0:r0 + t, :] = (o * silu_z).astype(o_ref.dtype)

    skew = ATT_SKEW
    sc, pr, res = {}, {}, {}
    for step in range(len(units) + 2 * skew):
        u_sc, u_pr, u_pv = step, step - skew, step - 2 * skew
        if 0 <= u_sc < len(units):
            sc[u_sc] = scores(u_sc)
        if 0 <= u_pr < len(units):
            pr[u_pr] = probs(sc.pop(u_pr))
        if 0 <= u_pv < len(units):
            res[u_pv] = weighted(u_pv, pr.pop(u_pv))
            if units[u_pv][1] == 1:
                finish(units[u_pv][0], (res.pop(u_pv - 1), res.pop(u_pv)))
        yield


def _mixers_kernel(lg_ref, rq_ref, rk_ref, rv_ref, rz_ref, dq_ref, dk_ref, dv_ref, dz_ref,
                   lq1_ref, lk1_ref, lq2_ref, lk2_ref, subln_ref, s1_ref, s2_ref, s3_ref,
                   oret_ref, odiff_ref, s1bf_ref, s2bf_ref, s3bf_ref,
                   qx_ref, kz_ref, rvc_ref, dvc_ref):
    for src, dst in ((s1_ref, s1bf_ref), (s2_ref, s2bf_ref), (s3_ref, s3bf_ref)):
        dst[...] = src[...].astype(BF16)

    retention = _retention_pieces(lg_ref[pl.program_id(1)], rq_ref, rk_ref, rv_ref, rz_ref,
                                  oret_ref, qx_ref, kz_ref, rvc_ref)
    attention = _diff_attn_pieces(dq_ref, dk_ref, dv_ref, dz_ref, lq1_ref, lk1_ref, lq2_ref,
                                  lk2_ref, subln_ref, odiff_ref, dvc_ref)
    live = [attention, retention]
    while live:
        for gen, count in zip((attention, retention), MIX_PATTERN):
            for _ in range(count):
                if gen in live and next(gen, "done") == "done":
                    live.remove(gen)


def _token_mixers(rot4, plain4, silu4, log_gamma, lq1, lk1, lq2, lk2, subln_w, side_weights):
    _, b, s, _ = rot4.shape
    assert RET_HEADS == DIFF_HEADS
    one = lambda name: pl.BlockSpec(
        (None, None, s, LANES), lambda bi, h, base=_slab_base(name): (base + h, bi, 0, 0))
    pair = lambda name: pl.BlockSpec(
        (2, None, s, LANES), lambda bi, h, base=_slab_base(name): (base // 2 + h, bi, 0, 0))
    vec = lambda n: pl.BlockSpec((1, n), lambda bi, h: (0, 0))
    side_rows, side_cols = side_weights[0].shape
    assert all(w.shape == (side_rows, side_cols) for w in side_weights)
    band = side_rows // (b * DIFF_HEADS)
    side_spec = pl.BlockSpec((band, side_cols), lambda bi, h: (bi * DIFF_HEADS + h, 0))
    head_out = pl.BlockSpec((None, None, s, RET_V_DIM), lambda bi, h: (h, bi, 0, 0))
    outs = pl.pallas_call(
        _mixers_kernel,
        out_shape=[jax.ShapeDtypeStruct((RET_HEADS, b, s, RET_V_DIM), BF16),
                   jax.ShapeDtypeStruct((DIFF_HEADS, b, s, DIFF_V_DIM), BF16)]
                  + [jax.ShapeDtypeStruct((side_rows, side_cols), BF16)] * len(side_weights),
        grid=(b, DIFF_HEADS),
        in_specs=[pl.BlockSpec(memory_space=pltpu.SMEM),
                  one("rq"), one("rk"), pair("rv"), pair("rz"),
                  pair("dq"), pair("dk"), pair("dv"), pair("dz"),
                  vec(DIFF_HEAD_DIM), vec(DIFF_HEAD_DIM), vec(DIFF_HEAD_DIM), vec(DIFF_HEAD_DIM),
                  vec(DIFF_V_DIM)] + [side_spec] * len(side_weights),
        out_specs=[head_out, head_out] + [side_spec] * len(side_weights),
        scratch_shapes=[pltpu.VMEM((s, RET_QK_DIM), BF16)] * 2
                       + [pltpu.VMEM((s, RET_V_DIM), BF16), pltpu.VMEM((s, DIFF_V_DIM), BF16)],
        compiler_params=_params(2),
        name="token_mixers",
    )(log_gamma, rot4, rot4, plain4, silu4, rot4, rot4, plain4, silu4,
      lq1, lk1, lq2, lk2, subln_w, *side_weights)
    return outs[0], outs[1], outs[2:]


def _merge_kernel(a_ref, b_ref, w1_ref, w2_ref, g1_ref, g2_ref, o_ref):
    kw = a_ref.shape[2]

    def up(x_ref, w_ref, rows):
        acc = None
        for k in range(x_ref.shape[0]):
            part = jnp.dot(x_ref[k, rows, :], w_ref[k * kw:(k + 1) * kw, :],
                           preferred_element_type=F32)
            acc = part if acc is None else acc + part
        return acc

    for rows in _row_bands(a_ref.shape[1], MERGE_ROW_BANDS):
        up1 = up(a_ref, w1_ref, rows)
        up2 = up(b_ref, w2_ref, rows)
        for j in range(g1_ref.shape[0]):
            sl = slice(j * LANES, (j + 1) * LANES)
            o_ref[rows, sl] = (g1_ref[j, rows, :].astype(F32) * up1[:, sl]
                               + g2_ref[j, rows, :].astype(F32) * up2[:, sl]).astype(o_ref.dtype)


def _gated_merge(o_ret, o_diff, w_ret_up_bf, w_diff_up_bf, gates, tm=512):
    nk, m, kw = o_ret.shape
    d = w_ret_up_bf.shape[1]
    gs = d // LANES
    x_spec = pl.BlockSpec((nk, tm, kw), lambda i: (0, i, 0))
    w_spec = pl.BlockSpec((nk * kw, d), lambda i: (0, 0), pipeline_mode=pl.Buffered(1))
    g_spec = lambda name: pl.BlockSpec(
        (gs, tm, LANES), lambda i, base=_slab_base(name): (base // gs, i, 0))
    return pl.pallas_call(
        _merge_kernel,
        out_shape=jax.ShapeDtypeStruct((m, d), BF16),
        grid=(m // tm,),
        in_specs=[x_spec, x_spec, w_spec, w_spec, g_spec("gr"), g_spec("gd")],
        out_specs=pl.BlockSpec((tm, d), lambda i: (i, 0)),
        compiler_params=_params(1),
        name="gated_merge",
    )(o_ret, o_diff, w_ret_up_bf, w_diff_up_bf, gates, gates)


def _outproj_kernel(mx_ref, w_ref, x_ref, fw_ref, o_ref):
    for rows in _row_bands(mx_ref.shape[0], OUT_ROW_BANDS):
        h = x_ref[rows, :] + jnp.dot(mx_ref[rows, :], w_ref[...], preferred_element_type=F32)
        ms = jnp.mean(h * h, axis=-1, keepdims=True)
        o_ref[rows, :] = h * lax.rsqrt(ms + NORM_EPS) * fw_ref[...]


def _output_projection(mixed, w_out_bf, x2, final_w, tm=1024):
    m, d = x2.shape
    return pl.pallas_call(
        _outproj_kernel,
        out_shape=jax.ShapeDtypeStruct((m, d), F32),
        grid=(m // tm,),
        in_specs=[pl.BlockSpec((tm, d), lambda i: (i, 0)),
                  pl.BlockSpec((d, d), lambda i: (0, 0), pipeline_mode=pl.Buffered(1)),
                  pl.BlockSpec((tm, d), lambda i: (i, 0)),
                  pl.BlockSpec((1, d), lambda i: (0, 0))],
        out_specs=pl.BlockSpec((tm, d), lambda i: (i, 0)),
        compiler_params=_params(1),
        name="output_projection",
    )(mixed, w_out_bf, x2, final_w)


def _rotary_tables(s, inv_freq):
    pos = np.arange(s, dtype=np.float32)
    ang = pos[:, None] * inv_freq[None, :].astype(np.float32)
    emb = np.concatenate([ang, ang], axis=-1)
    half = emb.shape[-1] // 2
    sign = np.concatenate([-np.ones((half,), np.float32), np.ones((half,), np.float32)])
    return np.cos(emb), np.sin(emb) * sign[None, :]


def kernel(x, norm_w, w_in, w_ret_up, w_diff_up, w_out, lambda_q1, lambda_k1, lambda_q2,
           lambda_k2, subln_w, final_norm_w):
    b, s, d = x.shape
    assert d == D_MODEL and w_in.shape == (1, D_MODEL, IN_WIDTH)
    x2 = x.reshape(b * s, d)

    f32 = np.float32
    ret_inv_freq = np.exp(f32(-math.log(10000.0)) * np.linspace(0.0, 1.0, RET_QK_DIM // 2, dtype=f32))
    rope_inv_freq = f32(ROPE_THETA) ** (-np.arange(0, DIFF_HEAD_DIM, 2, dtype=f32) / f32(DIFF_HEAD_DIM))
    ret_cos, ret_sin = _rotary_tables(s, ret_inv_freq)
    rope_cos, rope_sin = _rotary_tables(s, rope_inv_freq)
    log_gamma = jnp.asarray(np.log1p(-np.exp2(f32(-5.0) - np.arange(RET_HEADS, dtype=f32))))

    ret_k_scale = f32(RET_QK_DIM ** -0.5)
    diff_q_scale = f32((DIFF_HEAD_DIM ** -0.5) * math.log2(math.e))
    cos_tables = jnp.asarray(
        np.stack([ret_cos, ret_cos * ret_k_scale, rope_cos * diff_q_scale, rope_cos]))
    sin_tables = jnp.asarray(
        np.stack([ret_sin, ret_sin * ret_k_scale, rope_sin * diff_q_scale, rope_sin]))

    xn = _input_rmsnorm(x2, norm_w[0][None, :])
    proj = {kind: _input_projection(xn, w_in[0], cos_tables, sin_tables, kind) for kind in _KINDS}
    by_pos = lambda a: a.reshape(a.shape[0], b, s, LANES)
    rot4, plain4, silu4 = by_pos(proj["rotary"]), by_pos(proj["plain"]), by_pos(proj["silu"])

    o_ret, o_diff, (w_ret_up_bf, w_diff_up_bf, w_out_bf) = _token_mixers(
        rot4, plain4, silu4, log_gamma, lambda_q1, lambda_k1, lambda_q2, lambda_k2, subln_w,
        (w_ret_up[0], w_diff_up[0], w_out[0]))
    o_ret = o_ret.reshape(RET_HEADS, b * s, RET_V_DIM)
    o_diff = o_diff.reshape(DIFF_HEADS, b * s, DIFF_V_DIM)

    mixed = _gated_merge(o_ret, o_diff, w_ret_up_bf, w_diff_up_bf, proj["sigmoid"])
    out = _output_projection(mixed, w_out_bf, x2, final_norm_w[None, :])
    return out.reshape(b, s, d)
```

```python
import functools
import math

import jax
import jax.numpy as jnp
import numpy as np
from jax import lax
from jax.experimental import pallas as pl
from jax.experimental.pallas import tpu as pltpu

F32 = jnp.float32
BF16 = jnp.bfloat16

D_MODEL = 2048
RET_QK_DIM = 128
RET_V_DIM = 256
RET_HEADS = D_MODEL // RET_V_DIM
DIFF_HEAD_DIM = 128
DIFF_V_DIM = 2 * DIFF_HEAD_DIM
DIFF_HEADS = D_MODEL // DIFF_V_DIM
ROPE_THETA = 10000.0
NORM_EPS = 1e-6
SUBLN_EPS = 1e-5
LAMBDA_INIT = 0.8 - 0.6 * math.exp(-0.3 * 0)

RET_QK_WIDTH = RET_HEADS * RET_QK_DIM
RET_WIDTH = RET_HEADS * RET_V_DIM
DIFF_QK_WIDTH = DIFF_HEADS * 2 * DIFF_HEAD_DIM
DIFF_WIDTH = DIFF_HEADS * DIFF_V_DIM
IN_WIDTH = 2 * RET_QK_WIDTH + 2 * RET_WIDTH + 2 * DIFF_QK_WIDTH + 2 * DIFF_WIDTH + 2 * D_MODEL

LANES = 128

_SECTIONS = (("rq", RET_QK_WIDTH, "rotary", 0), ("rk", RET_QK_WIDTH, "rotary", 1),
             ("rv", RET_WIDTH, "plain", 0), ("rz", RET_WIDTH, "silu", 0),
             ("dq", DIFF_QK_WIDTH, "rotary", 2), ("dk", DIFF_QK_WIDTH, "rotary", 3),
             ("dv", DIFF_WIDTH, "plain", 0), ("dz", DIFF_WIDTH, "silu", 0),
             ("gr", D_MODEL, "sigmoid", 0), ("gd", D_MODEL, "sigmoid", 0))
assert sum(sec[1] for sec in _SECTIONS) == IN_WIDTH
_KINDS = ("rotary", "plain", "silu", "sigmoid")


def _slab_base(name):
    kind = next(sec[2] for sec in _SECTIONS if sec[0] == name)
    base = 0
    for sec_name, width, sec_kind, _ in _SECTIONS:
        if sec_name == name:
            return base
        if sec_kind == kind:
            base += width // LANES
    raise KeyError(name)

VMEM_LIMIT = 56 * 1024 * 1024

RET_CHUNK = 256
ATT_BLOCK = 256
ATT_SKEW = 1
PROJ_ROW_BANDS = ((1, 4), (1, 4), (1, 4), (1, 8), (1, 16), (1, 16))
NORM_PROJ_ROW_BANDS = ((1, 4), (1, 4), (1, 4), (1, 8), (1, 8))
MERGE_ROW_BANDS = ((1, 2), (1, 4), (1, 4))
OUT_ROW_BANDS = ((1, 2), (1, 2))
MIX_PATTERN = (1, 1)


def _params(n_axes, vmem=VMEM_LIMIT, flags=None):
    return pltpu.CompilerParams(dimension_semantics=("arbitrary",) * n_axes,
                                vmem_limit_bytes=vmem, flags=flags)


def _rotary(x, cos, sin_signed):
    return x * cos + pltpu.roll(x, x.shape[-1] // 2, axis=x.ndim - 1) * sin_signed


def _sigmoid(g):
    return 0.5 * jnp.tanh(0.5 * g) + 0.5


def _silu(z):
    h = 0.5 * z
    return h * jnp.tanh(h) + h


def _row_bands(tm, fractions):
    bounds = [0]
    for num, den in fractions:
        bounds.append(bounds[-1] + tm * num // den)
    assert bounds[-1] == tm
    return [slice(r0, r1) for r0, r1 in zip(bounds[:-1], bounds[1:])]


def _select(j, values):
    return sum(jnp.where(j == t, v, 0) for t, v in enumerate(values) if v)


def _projection_kernel(*refs, epilogue):
    if epilogue == "rotary":
        x_ref, w_ref, cos_ref, sin_ref, o_ref = refs
        fn = lambda a, rows: _rotary(a, cos_ref[rows, :], sin_ref[rows, :])
    else:
        x_ref, w_ref, o_ref = refs
        fn = {"plain": lambda a, rows: a, "silu": lambda a, rows: _silu(a),
              "sigmoid": lambda a, rows: _sigmoid(a)}[epilogue]
    w = w_ref[...].astype(BF16)
    for rows in _row_bands(x_ref.shape[0], PROJ_ROW_BANDS):
        acc = jnp.dot(x_ref[rows, :], w, preferred_element_type=F32)
        for s in range(o_ref.shape[0]):
            o_ref[s, rows, :] = fn(acc[:, s * LANES:(s + 1) * LANES], rows).astype(o_ref.dtype)


def _section_tiles(epilogue, tn):
    col_tiles, tables = [], []
    col = 0
    for _, width, kind, table in _SECTIONS:
        assert width % tn == 0
        if kind == epilogue:
            col_tiles += [col // tn + t for t in range(width // tn)]
            tables += [table] * (width // tn)
        col += width
    return col_tiles, tables


def _norm_projection_kernel(x_ref, nw_ref, w_ref, o_ref, xn_ref, xn_scr):
    first = pl.program_id(1) == 0
    bands = _row_bands(x_ref.shape[0], NORM_PROJ_ROW_BANDS)

    def project(rows, xn):
        acc = jnp.dot(xn, w_ref[...].astype(BF16), preferred_element_type=F32)
        for s in range(o_ref.shape[0]):
            o_ref[s, rows, :] = acc[:, s * LANES:(s + 1) * LANES].astype(o_ref.dtype)

    @pl.when(first)
    def _():
        for rows in bands:
            x = x_ref[rows, :]
            ms = jnp.mean(x * x, axis=-1, keepdims=True)
            xn = (x * lax.rsqrt(ms + NORM_EPS) * nw_ref[...]).astype(BF16)
            xn_scr[rows, :] = xn
            xn_ref[rows, :] = xn
            project(rows, xn)

    @pl.when(jnp.logical_not(first))
    def _():
        for rows in bands:
            project(rows, xn_scr[rows, :])


def _norm_projection(x2, norm_w, w_in, epilogue="plain", tm=1024, tn=1024):
    m, d = x2.shape
    col_tiles, _ = _section_tiles(epilogue, tn)
    return pl.pallas_call(
        _norm_projection_kernel,
        out_shape=[jax.ShapeDtypeStruct((len(col_tiles) * tn // LANES, m, LANES), BF16),
                   jax.ShapeDtypeStruct((m, d), BF16)],
        grid=(m // tm, len(col_tiles)),
        in_specs=[pl.BlockSpec((tm, d), lambda i, j: (i, 0)),
                  pl.BlockSpec((1, d), lambda i, j: (0, 0)),
                  pl.BlockSpec((d, tn), lambda i, j: (0, _select(j, col_tiles)))],
        out_specs=[pl.BlockSpec((tn // LANES, tm, LANES), lambda i, j: (j, i, 0)),
                   pl.BlockSpec((tm, d), lambda i, j: (i, 0))],
        scratch_shapes=[pltpu.VMEM((tm, d), BF16)],
        compiler_params=_params(2),
        name="norm_projection_" + epilogue,
    )(x2, norm_w, w_in)


def _input_projection(xn, w_in, cos_tables, sin_tables, epilogue, tm=2048, tn=1024):
    m, d = xn.shape
    s_len = cos_tables.shape[1]
    col_tiles, tables = _section_tiles(epilogue, tn)

    in_specs = [pl.BlockSpec((tm, d), lambda j, i: (i, 0)),
                pl.BlockSpec((d, tn), lambda j, i: (0, _select(j, col_tiles)))]
    operands = [xn, w_in]
    if epilogue == "rotary":
        table_spec = pl.BlockSpec((None, tm, LANES),
                                  lambda j, i: (_select(j, tables), i % (s_len // tm), 0))
        in_specs += [table_spec, table_spec]
        operands += [cos_tables, sin_tables]
    return pl.pallas_call(
        functools.partial(_projection_kernel, epilogue=epilogue),
        out_shape=jax.ShapeDtypeStruct((len(col_tiles) * tn // LANES, m, LANES), BF16),
        grid=(len(col_tiles), m // tm),
        in_specs=in_specs,
        out_specs=pl.BlockSpec((tn // LANES, tm, LANES), lambda j, i: (j, i, 0)),
        compiler_params=_params(2),
        name="projection_" + epilogue,
    )(*operands)


def _retention_pieces(lg, q_ref, k_ref, v_ref, z_ref, o_ref, qx_ref, kz_ref, vc_ref):
    s_len = q_ref.shape[0]
    c = RET_CHUNK
    n_chunks = s_len // c

    rowq = lax.broadcasted_iota(jnp.int32, (c, RET_QK_DIM), 0).astype(F32)
    zeta = jnp.exp(lg * (c - 1.0 - rowq))
    xi = jnp.exp(lg * (rowq + 1.0))
    for n in range(n_chunks):
        sl = slice(n * c, (n + 1) * c)
        qx_ref[sl, :] = (q_ref[sl, :].astype(F32) * xi).astype(BF16)
        kz_ref[sl, :] = (k_ref[sl, :].astype(F32) * zeta).astype(BF16)
    vc_ref[...] = jnp.concatenate([v_ref[0], v_ref[1]], axis=-1)

    row = lax.broadcasted_iota(jnp.int32, (c, c), 0).astype(F32)
    col = lax.broadcasted_iota(jnp.int32, (c, c), 1).astype(F32)
    rel = row - col
    decay_mask = jnp.where(rel >= 0.0, jnp.exp(lg * jnp.maximum(rel, 0.0)), 0.0)
    chunk_decay = jnp.exp(jnp.zeros((1, RET_V_DIM), F32) + lg * c)
    yield

    state = jnp.zeros((RET_QK_DIM, RET_V_DIM), F32)
    for n in range(n_chunks):
        sl = slice(n * c, (n + 1) * c)
        vc = vc_ref[sl, :]
        scores = lax.dot_general(q_ref[sl, :], k_ref[sl, :], (((1,), (1,)), ((), ())),
                                 preferred_element_type=F32) * decay_mask
        o = jnp.dot(scores.astype(BF16), vc, preferred_element_type=F32)
        o = o + jnp.dot(qx_ref[sl, :], state.astype(BF16), preferred_element_type=F32)
        if n + 1 < n_chunks:
            kv = lax.dot_general(kz_ref[sl, :], vc, (((0,), (0,)), ((), ())),
                                 preferred_element_type=F32)
            state = state * chunk_decay + kv

        ms = jnp.mean(o * o, axis=-1, keepdims=True)
        o = o * lax.rsqrt(ms + NORM_EPS)
        silu_z = jnp.concatenate([z_ref[0, sl, :], z_ref[1, sl, :]], axis=-1).astype(F32)
        o_ref[sl, :] = (o * silu_z).astype(o_ref.dtype)
        yield


def _diff_attn_pieces(q_ref, k_ref, v_ref, z_ref, lq1_ref, lk1_ref, lq2_ref, lk2_ref, subln_ref,
                      o_ref, vc_ref):
    t = ATT_BLOCK
    s_len = vc_ref.shape[0]
    vc_ref[...] = jnp.concatenate([v_ref[0], v_ref[1]], axis=-1)

    lam = (jnp.exp(jnp.sum(lq1_ref[...] * lk1_ref[...], axis=-1, keepdims=True))
           - jnp.exp(jnp.sum(lq2_ref[...] * lk2_ref[...], axis=-1, keepdims=True))
           + LAMBDA_INIT)
    causal = (lax.broadcasted_iota(jnp.int32, (t, t), 1)
              <= lax.broadcasted_iota(jnp.int32, (t, t), 0))
    nt_dims = (((1,), (1,)), ((), ()))

    units = [(n, c) for n in reversed(range(s_len // t)) for c in range(2)]

    def scores(u):
        n, c = units[u]
        r0 = n * t
        q = q_ref[c, r0:r0 + t, :]
        s_diag = jnp.where(causal,
                           lax.dot_general(q, k_ref[c, r0:r0 + t, :], nt_dims,
                                           preferred_element_type=F32), -jnp.inf)
        m = jnp.max(s_diag, axis=-1, keepdims=True)
        s_off = None
        if n > 0:
            s_off = lax.dot_general(q, k_ref[c, 0:r0, :], nt_dims, preferred_element_type=F32)
            m = jnp.maximum(m, jnp.max(s_off, axis=-1, keepdims=True))
        return s_diag, s_off, m

    def probs(sc):
        s_diag, s_off, m = sc
        p_diag = jnp.exp2(s_diag - m)
        l = jnp.sum(p_diag, axis=-1, keepdims=True)
        p_off = None
        if s_off is not None:
            p_off = jnp.exp2(s_off - m)
            l = l + jnp.sum(p_off, axis=-1, keepdims=True)
            p_off = p_off.astype(BF16)
        return p_diag.astype(BF16), p_off, l

    def weighted(u, pr):
        n, _ = units[u]
        r0 = n * t
        p_diag, p_off, l = pr
        acc = jnp.dot(p_diag, vc_ref[r0:r0 + t, :], preferred_element_type=F32)
        if p_off is not None:
            acc = acc + jnp.dot(p_off, vc_ref[0:r0, :], preferred_element_type=F32)
        return acc, l

    def finish(n, res):
        r0 = n * t
        o = res[0][0] * (1.0 / res[0][1]) - res[1][0] * (lam / res[1][1])
        ms = jnp.mean(o * o, axis=-1, keepdims=True)
        o = o * lax.rsqrt(ms + SUBLN_EPS) * subln_ref[...] * (1.0 - LAMBDA_INIT)
        silu_z = jnp.concatenate([z_ref[0, r0:r0 + t, :], z_ref[1, r0:r0 + t, :]],
                                 axis=-1).astype(F32)
        o_ref[r0:r0 + t, :] = (o * silu_z).astype(o_ref.dtype)

    skew = ATT_SKEW
    sc, pr, res = {}, {}, {}
    for step in range(len(units) + 2 * skew):
        u_sc, u_pr, u_pv = step, step - skew, step - 2 * skew
        if 0 <= u_sc < len(units):
            sc[u_sc] = scores(u_sc)
        if 0 <= u_pr < len(units):
            pr[u_pr] = probs(sc.pop(u_pr))
        if 0 <= u_pv < len(units):
            res[u_pv] = weighted(u_pv, pr.pop(u_pv))
            if units[u_pv][1] == 1:
                finish(units[u_pv][0], (res.pop(u_pv - 1), res.pop(u_pv)))
        yield


def _mixers_kernel(lg_ref, rq_ref, rk_ref, rv_ref, rz_ref, dq_ref, dk_ref, dv_ref, dz_ref,
                   lq1_ref, lk1_ref, lq2_ref, lk2_ref, subln_ref, s1_ref, s2_ref, s3_ref,
                   oret_ref, odiff_ref, s1bf_ref, s2bf_ref, s3bf_ref,
                   qx_ref, kz_ref, rvc_ref, dvc_ref):
    for src, dst in ((s1_ref, s1bf_ref), (s2_ref, s2bf_ref), (s3_ref, s3bf_ref)):
        dst[...] = src[...].astype(BF16)

    retention = _retention_pieces(lg_ref[pl.program_id(1)], rq_ref, rk_ref, rv_ref, rz_ref,
                                  oret_ref, qx_ref, kz_ref, rvc_ref)
    attention = _diff_attn_pieces(dq_ref, dk_ref, dv_ref, dz_ref, lq1_ref, lk1_ref, lq2_ref,
                                  lk2_ref, subln_ref, odiff_ref, dvc_ref)
    live = [attention, retention]
    while live:
        for gen, count in zip((attention, retention), MIX_PATTERN):
            for _ in range(count):
                if gen in live and next(gen, "done") == "done":
                    live.remove(gen)


def _token_mixers(rot4, plain4, silu4, log_gamma, lq1, lk1, lq2, lk2, subln_w, side_weights):
    _, b, s, _ = rot4.shape
    assert RET_HEADS == DIFF_HEADS
    one = lambda name: pl.BlockSpec(
        (None, None, s, LANES), lambda bi, h, base=_slab_base(name): (base + h, bi, 0, 0))
    pair = lambda name: pl.BlockSpec(
        (2, None, s, LANES), lambda bi, h, base=_slab_base(name): (base // 2 + h, bi, 0, 0))
    vec = lambda n: pl.BlockSpec((1, n), lambda bi, h: (0, 0))
    side_rows, side_cols = side_weights[0].shape
    assert all(w.shape == (side_rows, side_cols) for w in side_weights)
    band = side_rows // (b * DIFF_HEADS)
    side_spec = pl.BlockSpec((band, side_cols), lambda bi, h: (bi * DIFF_HEADS + h, 0))
    head_out = pl.BlockSpec((None, None, s, RET_V_DIM), lambda bi, h: (h, bi, 0, 0))
    outs = pl.pallas_call(
        _mixers_kernel,
        out_shape=[jax.ShapeDtypeStruct((RET_HEADS, b, s, RET_V_DIM), BF16),
                   jax.ShapeDtypeStruct((DIFF_HEADS, b, s, DIFF_V_DIM), BF16)]
                  + [jax.ShapeDtypeStruct((side_rows, side_cols), BF16)] * len(side_weights),
        grid=(b, DIFF_HEADS),
        in_specs=[pl.BlockSpec(memory_space=pltpu.SMEM),
                  one("rq"), one("rk"), pair("rv"), pair("rz"),
                  pair("dq"), pair("dk"), pair("dv"), pair("dz"),
                  vec(DIFF_HEAD_DIM), vec(DIFF_HEAD_DIM), vec(DIFF_HEAD_DIM), vec(DIFF_HEAD_DIM),
                  vec(DIFF_V_DIM)] + [side_spec] * len(side_weights),
        out_specs=[head_out, head_out] + [side_spec] * len(side_weights),
        scratch_shapes=[pltpu.VMEM((s, RET_QK_DIM), BF16)] * 2
                       + [pltpu.VMEM((s, RET_V_DIM), BF16), pltpu.VMEM((s, DIFF_V_DIM), BF16)],
        compiler_params=_params(2),
        name="token_mixers",
    )(log_gamma, rot4, rot4, plain4, silu4, rot4, rot4, plain4, silu4,
      lq1, lk1, lq2, lk2, subln_w, *side_weights)
    return outs[0], outs[1], outs[2:]


def _merge_kernel(a_ref, b_ref, w1_ref, w2_ref, g1_ref, g2_ref, o_ref):
    kw = a_ref.shape[2]

    def up(x_ref, w_ref, rows):
        acc = None
        for k in range(x_ref.shape[0]):
            part = jnp.dot(x_ref[k, rows, :], w_ref[k * kw:(k + 1) * kw, :],
                           preferred_element_type=F32)
            acc = part if acc is None else acc + part
        return acc

    for rows in _row_bands(a_ref.shape[1], MERGE_ROW_BANDS):
        up1 = up(a_ref, w1_ref, rows)
        up2 = up(b_ref, w2_ref, rows)
        for j in range(g1_ref.shape[0]):
            sl = slice(j * LANES, (j + 1) * LANES)
            o_ref[rows, sl] = (g1_ref[j, rows, :].astype(F32) * up1[:, sl]
                               + g2_ref[j, rows, :].astype(F32) * up2[:, sl]).astype(o_ref.dtype)


def _gated_merge(o_ret, o_diff, w_ret_up_bf, w_diff_up_bf, gates, tm=512):
    nk, m, kw = o_ret.shape
    d = w_ret_up_bf.shape[1]
    gs = d // LANES
    x_spec = pl.BlockSpec((nk, tm, kw), lambda i: (0, i, 0))
    w_spec = pl.BlockSpec((nk * kw, d), lambda i: (0, 0), pipeline_mode=pl.Buffered(1))
    g_spec = lambda name: pl.BlockSpec(
        (gs, tm, LANES), lambda i, base=_slab_base(name): (base // gs, i, 0))
    return pl.pallas_call(
        _merge_kernel,
        out_shape=jax.ShapeDtypeStruct((m, d), BF16),
        grid=(m // tm,),
        in_specs=[x_spec, x_spec, w_spec, w_spec, g_spec("gr"), g_spec("gd")],
        out_specs=pl.BlockSpec((tm, d), lambda i: (i, 0)),
        compiler_params=_params(1),
        name="gated_merge",
    )(o_ret, o_diff, w_ret_up_bf, w_diff_up_bf, gates, gates)


def _outproj_kernel(mx_ref, w_ref, x_ref, fw_ref, o_ref):
    for rows in _row_bands(mx_ref.shape[0], OUT_ROW_BANDS):
        h = x_ref[rows, :] + jnp.dot(mx_ref[rows, :], w_ref[...], preferred_element_type=F32)
        ms = jnp.mean(h * h, axis=-1, keepdims=True)
        o_ref[rows, :] = h * lax.rsqrt(ms + NORM_EPS) * fw_ref[...]


def _output_projection(mixed, w_out_bf, x2, final_w, tm=512):
    m, d = x2.shape
    return pl.pallas_call(
        _outproj_kernel,
        out_shape=jax.ShapeDtypeStruct((m, d), F32),
        grid=(m // tm,),
        in_specs=[pl.BlockSpec((tm, d), lambda i: (i, 0)),
                  pl.BlockSpec((d, d), lambda i: (0, 0), pipeline_mode=pl.Buffered(1)),
                  pl.BlockSpec((tm, d), lambda i: (i, 0)),
                  pl.BlockSpec((1, d), lambda i: (0, 0))],
        out_specs=pl.BlockSpec((tm, d), lambda i: (i, 0)),
        compiler_params=_params(1),
        name="output_projection",
    )(mixed, w_out_bf, x2, final_w)


def _rotary_tables(s, inv_freq):
    pos = np.arange(s, dtype=np.float32)
    ang = pos[:, None] * inv_freq[None, :].astype(np.float32)
    emb = np.concatenate([ang, ang], axis=-1)
    half = emb.shape[-1] // 2
    sign = np.concatenate([-np.ones((half,), np.float32), np.ones((half,), np.float32)])
    return np.cos(emb), np.sin(emb) * sign[None, :]


def kernel(x, norm_w, w_in, w_ret_up, w_diff_up, w_out, lambda_q1, lambda_k1, lambda_q2,
           lambda_k2, subln_w, final_norm_w):
    b, s, d = x.shape
    assert d == D_MODEL and w_in.shape == (1, D_MODEL, IN_WIDTH)
    x2 = x.reshape(b * s, d)

    f32 = np.float32
    ret_inv_freq = np.exp(f32(-math.log(10000.0)) * np.linspace(0.0, 1.0, RET_QK_DIM // 2, dtype=f32))
    rope_inv_freq = f32(ROPE_THETA) ** (-np.arange(0, DIFF_HEAD_DIM, 2, dtype=f32) / f32(DIFF_HEAD_DIM))
    ret_cos, ret_sin = _rotary_tables(s, ret_inv_freq)
    rope_cos, rope_sin = _rotary_tables(s, rope_inv_freq)
    log_gamma = jnp.asarray(np.log1p(-np.exp2(f32(-5.0) - np.arange(RET_HEADS, dtype=f32))))

    ret_k_scale = f32(RET_QK_DIM ** -0.5)
    diff_q_scale = f32((DIFF_HEAD_DIM ** -0.5) * math.log2(math.e))
    cos_tables = jnp.asarray(
        np.stack([ret_cos, ret_cos * ret_k_scale, rope_cos * diff_q_scale, rope_cos]))
    sin_tables = jnp.asarray(
        np.stack([ret_sin, ret_sin * ret_k_scale, rope_sin * diff_q_scale, rope_sin]))

    proj = {}
    proj["plain"], xn = _norm_projection(x2, norm_w[0][None, :], w_in[0])
    for kind in _KINDS:
        if kind not in proj:
            proj[kind] = _input_projection(xn, w_in[0], cos_tables, sin_tables, kind)
    by_pos = lambda a: a.reshape(a.shape[0], b, s, LANES)
    rot4, plain4, silu4 = by_pos(proj["rotary"]), by_pos(proj["plain"]), by_pos(proj["silu"])

    o_ret, o_diff, (w_ret_up_bf, w_diff_up_bf, w_out_bf) = _token_mixers(
        rot4, plain4, silu4, log_gamma, lambda_q1, lambda_k1, lambda_q2, lambda_k2, subln_w,
        (w_ret_up[0], w_diff_up[0], w_out[0]))
    o_ret = o_ret.reshape(RET_HEADS, b * s, RET_V_DIM)
    o_diff = o_diff.reshape(DIFF_HEADS, b * s, DIFF_V_DIM)

    mixed = _gated_merge(o_ret, o_diff, w_ret_up_bf, w_diff_up_bf, proj["sigmoid"])
    out = _output_projection(mixed, w_out_bf, x2, final_norm_w[None, :])
    return out.reshape(b, s, d)
```

```python
import functools
import math

import jax
import jax.numpy as jnp
import numpy as np
from jax import lax
from jax.experimental import pallas as pl
from jax.experimental.pallas import tpu as pltpu

F32 = jnp.float32
BF16 = jnp.bfloat16

D_MODEL = 2048
RET_QK_DIM = 128
RET_V_DIM = 256
RET_HEADS = D_MODEL // RET_V_DIM
DIFF_HEAD_DIM = 128
DIFF_V_DIM = 2 * DIFF_HEAD_DIM
DIFF_HEADS = D_MODEL // DIFF_V_DIM
ROPE_THETA = 10000.0
NORM_EPS = 1e-6
SUBLN_EPS = 1e-5
LAMBDA_INIT = 0.8 - 0.6 * math.exp(-0.3 * 0)

RET_QK_WIDTH = RET_HEADS * RET_QK_DIM
RET_WIDTH = RET_HEADS * RET_V_DIM
DIFF_QK_WIDTH = DIFF_HEADS * 2 * DIFF_HEAD_DIM
DIFF_WIDTH = DIFF_HEADS * DIFF_V_DIM
IN_WIDTH = 2 * RET_QK_WIDTH + 2 * RET_WIDTH + 2 * DIFF_QK_WIDTH + 2 * DIFF_WIDTH + 2 * D_MODEL

LANES = 128

_SECTIONS = (("rq", RET_QK_WIDTH, "rotary", 0), ("rk", RET_QK_WIDTH, "rotary", 1),
             ("rv", RET_WIDTH, "plain", 0), ("rz", RET_WIDTH, "silu", 0),
             ("dq", DIFF_QK_WIDTH, "rotary", 2), ("dk", DIFF_QK_WIDTH, "rotary", 3),
             ("dv", DIFF_WIDTH, "plain", 0), ("dz", DIFF_WIDTH, "silu", 0),
             ("gr", D_MODEL, "sigmoid", 0), ("gd", D_MODEL, "sigmoid", 0))
assert sum(sec[1] for sec in _SECTIONS) == IN_WIDTH
_KINDS = ("rotary", "plain", "silu", "sigmoid")


def _slab_base(name):
    kind = next(sec[2] for sec in _SECTIONS if sec[0] == name)
    base = 0
    for sec_name, width, sec_kind, _ in _SECTIONS:
        if sec_name == name:
            return base
        if sec_kind == kind:
            base += width // LANES
    raise KeyError(name)

VMEM_LIMIT = 56 * 1024 * 1024

RET_CHUNK = 256
ATT_BLOCK = 256
ATT_SKEW = 1
PROJ_ROW_BANDS = ((1, 4), (1, 4), (1, 4), (1, 8), (1, 16), (1, 16))
NORM_PROJ_ROW_BANDS = ((1, 2), (1, 4), (1, 4))
MERGE_ROW_BANDS = ((1, 2), (1, 4), (1, 4))
OUT_ROW_BANDS = ((1, 2), (1, 2))
MIX_PATTERN = (1, 1)


def _params(n_axes, vmem=VMEM_LIMIT, flags=None):
    return pltpu.CompilerParams(dimension_semantics=("arbitrary",) * n_axes,
                                vmem_limit_bytes=vmem, flags=flags)


def _rotary(x, cos, sin_signed):
    return x * cos + pltpu.roll(x, x.shape[-1] // 2, axis=x.ndim - 1) * sin_signed


def _sigmoid(g):
    return 0.5 * jnp.tanh(0.5 * g) + 0.5


def _silu(z):
    h = 0.5 * z
    return h * jnp.tanh(h) + h


def _row_bands(tm, fractions):
    bounds = [0]
    for num, den in fractions:
        bounds.append(bounds[-1] + tm * num // den)
    assert bounds[-1] == tm
    return [slice(r0, r1) for r0, r1 in zip(bounds[:-1], bounds[1:])]


def _select(j, values):
    return sum(jnp.where(j == t, v, 0) for t, v in enumerate(values) if v)


def _projection_kernel(*refs, epilogue):
    if epilogue == "rotary":
        x_ref, w_ref, cos_ref, sin_ref, o_ref = refs
        fn = lambda a, rows: _rotary(a, cos_ref[rows, :], sin_ref[rows, :])
    else:
        x_ref, w_ref, o_ref = refs
        fn = {"plain": lambda a, rows: a, "silu": lambda a, rows: _silu(a),
              "sigmoid": lambda a, rows: _sigmoid(a)}[epilogue]
    w = w_ref[...].astype(BF16)
    for rows in _row_bands(x_ref.shape[0], PROJ_ROW_BANDS):
        acc = jnp.dot(x_ref[rows, :], w, preferred_element_type=F32)
        for s in range(o_ref.shape[0]):
            o_ref[s, rows, :] = fn(acc[:, s * LANES:(s + 1) * LANES], rows).astype(o_ref.dtype)


def _section_tiles(epilogue, tn):
    col_tiles, tables = [], []
    col = 0
    for _, width, kind, table in _SECTIONS:
        assert width % tn == 0
        if kind == epilogue:
            col_tiles += [col // tn + t for t in range(width // tn)]
            tables += [table] * (width // tn)
        col += width
    return col_tiles, tables


def _norm_projection_kernel(x_ref, nw_ref, w_ref, o_ref, xn_ref, xn_scr, wbf_scr):
    j = pl.program_id(1)
    first = j == 0
    bands = _row_bands(x_ref.shape[0], NORM_PROJ_ROW_BANDS)

    @pl.when(pl.program_id(0) == 0)
    def _():
        wbf_scr[j] = w_ref[...].astype(BF16)

    def project(rows, xn):
        acc = jnp.dot(xn, wbf_scr[j], preferred_element_type=F32)
        for s in range(o_ref.shape[0]):
            o_ref[s, rows, :] = acc[:, s * LANES:(s + 1) * LANES].astype(o_ref.dtype)

    @pl.when(first)
    def _():
        for rows in bands:
            x = x_ref[rows, :]
            ms = jnp.mean(x * x, axis=-1, keepdims=True)
            xn = (x * lax.rsqrt(ms + NORM_EPS) * nw_ref[...]).astype(BF16)
            xn_scr[rows, :] = xn
            xn_ref[rows, :] = xn
            project(rows, xn)

    @pl.when(jnp.logical_not(first))
    def _():
        for rows in bands:
            project(rows, xn_scr[rows, :])


def _norm_projection(x2, norm_w, w_in, epilogue="plain", tm=512, tn=1024):
    m, d = x2.shape
    col_tiles, _ = _section_tiles(epilogue, tn)
    n_tiles = len(col_tiles)
    w_map = lambda i, j: (0, _select(jnp.where(i == 0, j, n_tiles - 1), col_tiles))
    return pl.pallas_call(
        _norm_projection_kernel,
        out_shape=[jax.ShapeDtypeStruct((n_tiles * tn // LANES, m, LANES), BF16),
                   jax.ShapeDtypeStruct((m, d), BF16)],
        grid=(m // tm, n_tiles),
        in_specs=[pl.BlockSpec((tm, d), lambda i, j: (i, 0)),
                  pl.BlockSpec((1, d), lambda i, j: (0, 0)),
                  pl.BlockSpec((d, tn), w_map)],
        out_specs=[pl.BlockSpec((tn // LANES, tm, LANES), lambda i, j: (j, i, 0)),
                   pl.BlockSpec((tm, d), lambda i, j: (i, 0))],
        scratch_shapes=[pltpu.VMEM((tm, d), BF16), pltpu.VMEM((n_tiles, d, tn), BF16)],
        compiler_params=_params(2),
        name="norm_projection_" + epilogue,
    )(x2, norm_w, w_in)


def _input_projection(xn, w_in, cos_tables, sin_tables, epilogue, tm=2048, tn=1024):
    m, d = xn.shape
    s_len = cos_tables.shape[1]
    col_tiles, tables = _section_tiles(epilogue, tn)

    in_specs = [pl.BlockSpec((tm, d), lambda j, i: (i, 0)),
                pl.BlockSpec((d, tn), lambda j, i: (0, _select(j, col_tiles)))]
    operands = [xn, w_in]
    if epilogue == "rotary":
        table_spec = pl.BlockSpec((None, tm, LANES),
                                  lambda j, i: (_select(j, tables), i % (s_len // tm), 0))
        in_specs += [table_spec, table_spec]
        operands += [cos_tables, sin_tables]
    return pl.pallas_call(
        functools.partial(_projection_kernel, epilogue=epilogue),
        out_shape=jax.ShapeDtypeStruct((len(col_tiles) * tn // LANES, m, LANES), BF16),
        grid=(len(col_tiles), m // tm),
        in_specs=in_specs,
        out_specs=pl.BlockSpec((tn // LANES, tm, LANES), lambda j, i: (j, i, 0)),
        compiler_params=_params(2),
        name="projection_" + epilogue,
    )(*operands)


def _retention_pieces(lg, q_ref, k_ref, v_ref, z_ref, o_ref, qx_ref, kz_ref, vc_ref):
    s_len = q_ref.shape[0]
    c = RET_CHUNK
    n_chunks = s_len // c

    rowq = lax.broadcasted_iota(jnp.int32, (c, RET_QK_DIM), 0).astype(F32)
    zeta = jnp.exp(lg * (c - 1.0 - rowq))
    xi = jnp.exp(lg * (rowq + 1.0))
    for n in range(n_chunks):
        sl = slice(n * c, (n + 1) * c)
        qx_ref[sl, :] = (q_ref[sl, :].astype(F32) * xi).astype(BF16)
        kz_ref[sl, :] = (k_ref[sl, :].astype(F32) * zeta).astype(BF16)
    vc_ref[...] = jnp.concatenate([v_ref[0], v_ref[1]], axis=-1)

    row = lax.broadcasted_iota(jnp.int32, (c, c), 0).astype(F32)
    col = lax.broadcasted_iota(jnp.int32, (c, c), 1).astype(F32)
    rel = row - col
    decay_mask = jnp.where(rel >= 0.0, jnp.exp(lg * jnp.maximum(rel, 0.0)), 0.0)
    chunk_decay = jnp.exp(jnp.zeros((1, RET_V_DIM), F32) + lg * c)
    yield

    state = jnp.zeros((RET_QK_DIM, RET_V_DIM), F32)
    for n in range(n_chunks):
        sl = slice(n * c, (n + 1) * c)
        vc = vc_ref[sl, :]
        scores = lax.dot_general(q_ref[sl, :], k_ref[sl, :], (((1,), (1,)), ((), ())),
                                 preferred_element_type=F32) * decay_mask
        o = jnp.dot(scores.astype(BF16), vc, preferred_element_type=F32)
        o = o + jnp.dot(qx_ref[sl, :], state.astype(BF16), preferred_element_type=F32)
        if n + 1 < n_chunks:
            kv = lax.dot_general(kz_ref[sl, :], vc, (((0,), (0,)), ((), ())),
                                 preferred_element_type=F32)
            state = state * chunk_decay + kv

        ms = jnp.mean(o * o, axis=-1, keepdims=True)
        o = o * lax.rsqrt(ms + NORM_EPS)
        silu_z = jnp.concatenate([z_ref[0, sl, :], z_ref[1, sl, :]], axis=-1).astype(F32)
        o_ref[sl, :] = (o * silu_z).astype(o_ref.dtype)
        yield


def _diff_attn_pieces(q_ref, k_ref, v_ref, z_ref, lq1_ref, lk1_ref, lq2_ref, lk2_ref, subln_ref,
                      o_ref, vc_ref):
    t = ATT_BLOCK
    s_len = vc_ref.shape[0]
    vc_ref[...] = jnp.concatenate([v_ref[0], v_ref[1]], axis=-1)

    lam = (jnp.exp(jnp.sum(lq1_ref[...] * lk1_ref[...], axis=-1, keepdims=True))
           - jnp.exp(jnp.sum(lq2_ref[...] * lk2_ref[...], axis=-1, keepdims=True))
           + LAMBDA_INIT)
    causal = (lax.broadcasted_iota(jnp.int32, (t, t), 1)
              <= lax.broadcasted_iota(jnp.int32, (t, t), 0))
    nt_dims = (((1,), (1,)), ((), ()))

    units = [(n, c) for n in reversed(range(s_len // t)) for c in range(2)]

    def scores(u):
        n, c = units[u]
        r0 = n * t
        q = q_ref[c, r0:r0 + t, :]
        s_diag = jnp.where(causal,
                           lax.dot_general(q, k_ref[c, r0:r0 + t, :], nt_dims,
                                           preferred_element_type=F32), -jnp.inf)
        m = jnp.max(s_diag, axis=-1, keepdims=True)
        s_off = None
        if n > 0:
            s_off = lax.dot_general(q, k_ref[c, 0:r0, :], nt_dims, preferred_element_type=F32)
            m = jnp.maximum(m, jnp.max(s_off, axis=-1, keepdims=True))
        return s_diag, s_off, m

    def probs(sc):
        s_diag, s_off, m = sc
        p_diag = jnp.exp2(s_diag - m)
        l = jnp.sum(p_diag, axis=-1, keepdims=True)
        p_off = None
        if s_off is not None:
            p_off = jnp.exp2(s_off - m)
            l = l + jnp.sum(p_off, axis=-1, keepdims=True)
            p_off = p_off.astype(BF16)
        return p_diag.astype(BF16), p_off, l

    def weighted(u, pr):
        n, _ = units[u]
        r0 = n * t
        p_diag, p_off, l = pr
        acc = jnp.dot(p_diag, vc_ref[r0:r0 + t, :], preferred_element_type=F32)
        if p_off is not None:
            acc = acc + jnp.dot(p_off, vc_ref[0:r0, :], preferred_element_type=F32)
        return acc, l

    def finish(n, res):
        r0 = n * t
        o = res[0][0] * (1.0 / res[0][1]) - res[1][0] * (lam / res[1][1])
        ms = jnp.mean(o * o, axis=-1, keepdims=True)
        o = o * lax.rsqrt(ms + SUBLN_EPS) * subln_ref[...] * (1.0 - LAMBDA_INIT)
        silu_z = jnp.concatenate([z_ref[0, r0:r0 + t, :], z_ref[1, r0:r0 + t, :]],
                                 axis=-1).astype(F32)
        o_ref[r0:r0 + t, :] = (o * silu_z).astype(o_ref.dtype)

    skew = ATT_SKEW
    sc, pr, res = {}, {}, {}
    for step in range(len(units) + 2 * skew):
        u_sc, u_pr, u_pv = step, step - skew, step - 2 * skew
        if 0 <= u_sc < len(units):
            sc[u_sc] = scores(u_sc)
        if 0 <= u_pr < len(units):
            pr[u_pr] = probs(sc.pop(u_pr))
        if 0 <= u_pv < len(units):
            res[u_pv] = weighted(u_pv, pr.pop(u_pv))
            if units[u_pv][1] == 1:
                finish(units[u_pv][0], (res.pop(u_pv - 1), res.pop(u_pv)))
        yield


def _mixers_kernel(lg_ref, rq_ref, rk_ref, rv_ref, rz_ref, dq_ref, dk_ref, dv_ref, dz_ref,
                   lq1_ref, lk1_ref, lq2_ref, lk2_ref, subln_ref, s1_ref, s2_ref, s3_ref,
                   oret_ref, odiff_ref, s1bf_ref, s2bf_ref, s3bf_ref,
                   qx_ref, kz_ref, rvc_ref, dvc_ref):
    for src, dst in ((s1_ref, s1bf_ref), (s2_ref, s2bf_ref), (s3_ref, s3bf_ref)):
        dst[...] = src[...].astype(BF16)

    retention = _retention_pieces(lg_ref[pl.program_id(1)], rq_ref, rk_ref, rv_ref, rz_ref,
                                  oret_ref, qx_ref, kz_ref, rvc_ref)
    attention = _diff_attn_pieces(dq_ref, dk_ref, dv_ref, dz_ref, lq1_ref, lk1_ref, lq2_ref,
                                  lk2_ref, subln_ref, odiff_ref, dvc_ref)
    live = [attention, retention]
    while live:
        for gen, count in zip((attention, retention), MIX_PATTERN):
            for _ in range(count):
                if gen in live and next(gen, "done") == "done":
                    live.remove(gen)


def _token_mixers(rot4, plain4, silu4, log_gamma, lq1, lk1, lq2, lk2, subln_w, side_weights):
    _, b, s, _ = rot4.shape
    assert RET_HEADS == DIFF_HEADS
    one = lambda name: pl.BlockSpec(
        (None, None, s, LANES), lambda bi, h, base=_slab_base(name): (base + h, bi, 0, 0))
    pair = lambda name: pl.BlockSpec(
        (2, None, s, LANES), lambda bi, h, base=_slab_base(name): (base // 2 + h, bi, 0, 0))
    vec = lambda n: pl.BlockSpec((1, n), lambda bi, h: (0, 0))
    side_rows, side_cols = side_weights[0].shape
    assert all(w.shape == (side_rows, side_cols) for w in side_weights)
    band = side_rows // (b * DIFF_HEADS)
    side_spec = pl.BlockSpec((band, side_cols), lambda bi, h: (bi * DIFF_HEADS + h, 0))
    head_out = pl.BlockSpec((None, None, s, RET_V_DIM), lambda bi, h: (h, bi, 0, 0))
    outs = pl.pallas_call(
        _mixers_kernel,
        out_shape=[jax.ShapeDtypeStruct((RET_HEADS, b, s, RET_V_DIM), BF16),
                   jax.ShapeDtypeStruct((DIFF_HEADS, b, s, DIFF_V_DIM), BF16)]
                  + [jax.ShapeDtypeStruct((side_rows, side_cols), BF16)] * len(side_weights),
        grid=(b, DIFF_HEADS),
        in_specs=[pl.BlockSpec(memory_space=pltpu.SMEM),
                  one("rq"), one("rk"), pair("rv"), pair("rz"),
                  pair("dq"), pair("dk"), pair("dv"), pair("dz"),
                  vec(DIFF_HEAD_DIM), vec(DIFF_HEAD_DIM), vec(DIFF_HEAD_DIM), vec(DIFF_HEAD_DIM),
                  vec(DIFF_V_DIM)] + [side_spec] * len(side_weights),
        out_specs=[head_out, head_out] + [side_spec] * len(side_weights),
        scratch_shapes=[pltpu.VMEM((s, RET_QK_DIM), BF16)] * 2
                       + [pltpu.VMEM((s, RET_V_DIM), BF16), pltpu.VMEM((s, DIFF_V_DIM), BF16)],
        compiler_params=_params(2),
        name="token_mixers",
    )(log_gamma, rot4, rot4, plain4, silu4, rot4, rot4, plain4, silu4,
      lq1, lk1, lq2, lk2, subln_w, *side_weights)
    return outs[0], outs[1], outs[2:]


def _merge_kernel(a_ref, b_ref, w1_ref, w2_ref, g1_ref, g2_ref, o_ref):
    kw = a_ref.shape[2]

    def up(x_ref, w_ref, rows):
        acc = None
        for k in range(x_ref.shape[0]):
            part = jnp.dot(x_ref[k, rows, :], w_ref[k * kw:(k + 1) * kw, :],
                           preferred_element_type=F32)
            acc = part if acc is None else acc + part
        return acc

    for rows in _row_bands(a_ref.shape[1], MERGE_ROW_BANDS):
        up1 = up(a_ref, w1_ref, rows)
        up2 = up(b_ref, w2_ref, rows)
        for j in range(g1_ref.shape[0]):
            sl = slice(j * LANES, (j + 1) * LANES)
            o_ref[rows, sl] = (g1_ref[j, rows, :].astype(F32) * up1[:, sl]
                               + g2_ref[j, rows, :].astype(F32) * up2[:, sl]).astype(o_ref.dtype)


def _gated_merge(o_ret, o_diff, w_ret_up_bf, w_diff_up_bf, gates, tm=512):
    nk, m, kw = o_ret.shape
    d = w_ret_up_bf.shape[1]
    gs = d // LANES
    x_spec = pl.BlockSpec((nk, tm, kw), lambda i: (0, i, 0))
    w_spec = pl.BlockSpec((nk * kw, d), lambda i: (0, 0), pipeline_mode=pl.Buffered(1))
    g_spec = lambda name: pl.BlockSpec(
        (gs, tm, LANES), lambda i, base=_slab_base(name): (base // gs, i, 0))
    return pl.pallas_call(
        _merge_kernel,
        out_shape=jax.ShapeDtypeStruct((m, d), BF16),
        grid=(m // tm,),
        in_specs=[x_spec, x_spec, w_spec, w_spec, g_spec("gr"), g_spec("gd")],
        out_specs=pl.BlockSpec((tm, d), lambda i: (i, 0)),
        compiler_params=_params(1),
        name="gated_merge",
    )(o_ret, o_diff, w_ret_up_bf, w_diff_up_bf, gates, gates)


def _outproj_kernel(mx_ref, w_ref, x_ref, fw_ref, o_ref):
    for rows in _row_bands(mx_ref.shape[0], OUT_ROW_BANDS):
        h = x_ref[rows, :] + jnp.dot(mx_ref[rows, :], w_ref[...], preferred_element_type=F32)
        ms = jnp.mean(h * h, axis=-1, keepdims=True)
        o_ref[rows, :] = h * lax.rsqrt(ms + NORM_EPS) * fw_ref[...]


def _output_projection(mixed, w_out_bf, x2, final_w, tm=512):
    m, d = x2.shape
    return pl.pallas_call(
        _outproj_kernel,
        out_shape=jax.ShapeDtypeStruct((m, d), F32),
        grid=(m // tm,),
        in_specs=[pl.BlockSpec((tm, d), lambda i: (i, 0)),
                  pl.BlockSpec((d, d), lambda i: (0, 0), pipeline_mode=pl.Buffered(1)),
                  pl.BlockSpec((tm, d), lambda i: (i, 0)),
                  pl.BlockSpec((1, d), lambda i: (0, 0))],
        out_specs=pl.BlockSpec((tm, d), lambda i: (i, 0)),
        compiler_params=_params(1),
        name="output_projection",
    )(mixed, w_out_bf, x2, final_w)


def _rotary_tables(s, inv_freq):
    pos = np.arange(s, dtype=np.float32)
    ang = pos[:, None] * inv_freq[None, :].astype(np.float32)
    emb = np.concatenate([ang, ang], axis=-1)
    half = emb.shape[-1] // 2
    sign = np.concatenate([-np.ones((half,), np.float32), np.ones((half,), np.float32)])
    return np.cos(emb), np.sin(emb) * sign[None, :]


def kernel(x, norm_w, w_in, w_ret_up, w_diff_up, w_out, lambda_q1, lambda_k1, lambda_q2,
           lambda_k2, subln_w, final_norm_w):
    b, s, d = x.shape
    assert d == D_MODEL and w_in.shape == (1, D_MODEL, IN_WIDTH)
    x2 = x.reshape(b * s, d)

    f32 = np.float32
    ret_inv_freq = np.exp(f32(-math.log(10000.0)) * np.linspace(0.0, 1.0, RET_QK_DIM // 2, dtype=f32))
    rope_inv_freq = f32(ROPE_THETA) ** (-np.arange(0, DIFF_HEAD_DIM, 2, dtype=f32) / f32(DIFF_HEAD_DIM))
    ret_cos, ret_sin = _rotary_tables(s, ret_inv_freq)
    rope_cos, rope_sin = _rotary_tables(s, rope_inv_freq)
    log_gamma = jnp.asarray(np.log1p(-np.exp2(f32(-5.0) - np.arange(RET_HEADS, dtype=f32))))

    ret_k_scale = f32(RET_QK_DIM ** -0.5)
    diff_q_scale = f32((DIFF_HEAD_DIM ** -0.5) * math.log2(math.e))
    cos_tables = jnp.asarray(
        np.stack([ret_cos, ret_cos * ret_k_scale, rope_cos * diff_q_scale, rope_cos]))
    sin_tables = jnp.asarray(
        np.stack([ret_sin, ret_sin * ret_k_scale, rope_sin * diff_q_scale, rope_sin]))

    proj = {}
    proj["plain"], xn = _norm_projection(x2, norm_w[0][None, :], w_in[0])
    for kind in _KINDS:
        if kind not in proj:
            proj[kind] = _input_projection(xn, w_in[0], cos_tables, sin_tables, kind)
    by_pos = lambda a: a.reshape(a.shape[0], b, s, LANES)
    rot4, plain4, silu4 = by_pos(proj["rotary"]), by_pos(proj["plain"]), by_pos(proj["silu"])

    o_ret, o_diff, (w_ret_up_bf, w_diff_up_bf, w_out_bf) = _token_mixers(
        rot4, plain4, silu4, log_gamma, lambda_q1, lambda_k1, lambda_q2, lambda_k2, subln_w,
        (w_ret_up[0], w_diff_up[0], w_out[0]))
    o_ret = o_ret.reshape(RET_HEADS, b * s, RET_V_DIM)
    o_diff = o_diff.reshape(DIFF_HEADS, b * s, DIFF_V_DIM)

    mixed = _gated_merge(o_ret, o_diff, w_ret_up_bf, w_diff_up_bf, proj["sigmoid"])
    out = _output_projection(mixed, w_out_bf, x2, final_norm_w[None, :])
    return out.reshape(b, s, d)
```

```python
import functools
import math

import jax
import jax.numpy as jnp
import numpy as np
from jax import lax
from jax.experimental import pallas as pl
from jax.experimental.pallas import tpu as pltpu

F32 = jnp.float32
BF16 = jnp.bfloat16

D_MODEL = 2048
RET_QK_DIM = 128
RET_V_DIM = 256
RET_HEADS = D_MODEL // RET_V_DIM
DIFF_HEAD_DIM = 128
DIFF_V_DIM = 2 * DIFF_HEAD_DIM
DIFF_HEADS = D_MODEL // DIFF_V_DIM
ROPE_THETA = 10000.0
NORM_EPS = 1e-6
SUBLN_EPS = 1e-5
LAMBDA_INIT = 0.8 - 0.6 * math.exp(-0.3 * 0)

RET_QK_WIDTH = RET_HEADS * RET_QK_DIM
RET_WIDTH = RET_HEADS * RET_V_DIM
DIFF_QK_WIDTH = DIFF_HEADS * 2 * DIFF_HEAD_DIM
DIFF_WIDTH = DIFF_HEADS * DIFF_V_DIM
IN_WIDTH = 2 * RET_QK_WIDTH + 2 * RET_WIDTH + 2 * DIFF_QK_WIDTH + 2 * DIFF_WIDTH + 2 * D_MODEL

LANES = 128

_SECTIONS = (("rq", RET_QK_WIDTH, "rotary", 0), ("rk", RET_QK_WIDTH, "rotary", 1),
             ("rv", RET_WIDTH, "plain", 0), ("rz", RET_WIDTH, "silu", 0),
             ("dq", DIFF_QK_WIDTH, "rotary", 2), ("dk", DIFF_QK_WIDTH, "rotary", 3),
             ("dv", DIFF_WIDTH, "plain", 0), ("dz", DIFF_WIDTH, "silu", 0),
             ("gr", D_MODEL, "sigmoid", 0), ("gd", D_MODEL, "sigmoid", 0))
assert sum(sec[1] for sec in _SECTIONS) == IN_WIDTH
_PLAIN_GROUP = ("plain",)
_EPILOGUE_GROUP = ("rotary", "silu", "sigmoid")


def _group_sections(group):
    return [sec for kind in group for sec in _SECTIONS if sec[2] == kind]


def _slab_base(name):
    for group in (_PLAIN_GROUP, _EPILOGUE_GROUP):
        base = 0
        for sec_name, width, _, _ in _group_sections(group):
            if sec_name == name:
                return base
            base += width // LANES
    raise KeyError(name)

VMEM_LIMIT = 56 * 1024 * 1024

RET_CHUNK = 256
ATT_BLOCK = 256
ATT_SKEW = 1
PROJ_ROW_BANDS = ((1, 4), (1, 4), (1, 4), (1, 8), (1, 16), (1, 16))
NORM_PROJ_ROW_BANDS = ((1, 2), (1, 4), (1, 4))
MERGE_ROW_BANDS = ((1, 2), (1, 4), (1, 4))
OUT_ROW_BANDS = ((1, 2), (1, 2))
MIX_PATTERN = (1, 1)


def _params(n_axes, vmem=VMEM_LIMIT, flags=None):
    return pltpu.CompilerParams(dimension_semantics=("arbitrary",) * n_axes,
                                vmem_limit_bytes=vmem, flags=flags)


def _rotary(x, cos, sin_signed):
    return x * cos + pltpu.roll(x, x.shape[-1] // 2, axis=x.ndim - 1) * sin_signed


def _sigmoid(g):
    return 0.5 * jnp.tanh(0.5 * g) + 0.5


def _silu(z):
    h = 0.5 * z
    return h * jnp.tanh(h) + h


def _row_bands(tm, fractions):
    bounds = [0]
    for num, den in fractions:
        bounds.append(bounds[-1] + tm * num // den)
    assert bounds[-1] == tm
    return [slice(r0, r1) for r0, r1 in zip(bounds[:-1], bounds[1:])]


def _select(j, values):
    return sum(jnp.where(j == t, v, 0) for t, v in enumerate(values) if v)


def _projection_kernel(x_ref, w_ref, cos_ref, sin_ref, o_ref, *, tile_kinds):
    epilogues = {
        "rotary": lambda a, rows: _rotary(a, cos_ref[rows, :], sin_ref[rows, :]),
        "silu": lambda a, rows: _silu(a),
        "sigmoid": lambda a, rows: _sigmoid(a),
    }
    j = pl.program_id(0)
    for kind in dict.fromkeys(tile_kinds):
        tiles = [t for t, k in enumerate(tile_kinds) if k == kind]

        @pl.when(functools.reduce(jnp.logical_or, [j == t for t in tiles]))
        def _(fn=epilogues[kind]):
            w = w_ref[...].astype(BF16)
            for rows in _row_bands(x_ref.shape[0], PROJ_ROW_BANDS):
                acc = jnp.dot(x_ref[rows, :], w, preferred_element_type=F32)
                for s in range(o_ref.shape[0]):
                    o_ref[s, rows, :] = fn(acc[:, s * LANES:(s + 1) * LANES],
                                           rows).astype(o_ref.dtype)


def _section_tiles(group, tn):
    col_of = {}
    col = 0
    for name, width, _, _ in _SECTIONS:
        assert width % tn == 0
        col_of[name] = col
        col += width
    col_tiles, tables, kinds = [], [], []
    for name, width, kind, table in _group_sections(group):
        n = width // tn
        col_tiles += [col_of[name] // tn + t for t in range(n)]
        tables += [table] * n
        kinds += [kind] * n
    return col_tiles, tables, kinds


def _norm_projection_kernel(x_ref, nw_ref, w_ref, o_ref, xn_ref, xn_scr, wbf_scr):
    j = pl.program_id(1)
    first = j == 0
    bands = _row_bands(x_ref.shape[0], NORM_PROJ_ROW_BANDS)

    @pl.when(pl.program_id(0) == 0)
    def _():
        wbf_scr[j] = w_ref[...].astype(BF16)

    def project(rows, xn):
        acc = jnp.dot(xn, wbf_scr[j], preferred_element_type=F32)
        for s in range(o_ref.shape[0]):
            o_ref[s, rows, :] = acc[:, s * LANES:(s + 1) * LANES].astype(o_ref.dtype)

    @pl.when(first)
    def _():
        for rows in bands:
            x = x_ref[rows, :]
            ms = jnp.mean(x * x, axis=-1, keepdims=True)
            xn = (x * lax.rsqrt(ms + NORM_EPS) * nw_ref[...]).astype(BF16)
            xn_scr[rows, :] = xn
            xn_ref[rows, :] = xn
            project(rows, xn)

    @pl.when(jnp.logical_not(first))
    def _():
        for rows in bands:
            project(rows, xn_scr[rows, :])


def _norm_projection(x2, norm_w, w_in, tm=512, tn=1024):
    m, d = x2.shape
    col_tiles, _, _ = _section_tiles(_PLAIN_GROUP, tn)
    n_tiles = len(col_tiles)
    w_map = lambda i, j: (0, _select(jnp.where(i == 0, j, n_tiles - 1), col_tiles))
    return pl.pallas_call(
        _norm_projection_kernel,
        out_shape=[jax.ShapeDtypeStruct((n_tiles * tn // LANES, m, LANES), BF16),
                   jax.ShapeDtypeStruct((m, d), BF16)],
        grid=(m // tm, n_tiles),
        in_specs=[pl.BlockSpec((tm, d), lambda i, j: (i, 0)),
                  pl.BlockSpec((1, d), lambda i, j: (0, 0)),
                  pl.BlockSpec((d, tn), w_map)],
        out_specs=[pl.BlockSpec((tn // LANES, tm, LANES), lambda i, j: (j, i, 0)),
                   pl.BlockSpec((tm, d), lambda i, j: (i, 0))],
        scratch_shapes=[pltpu.VMEM((tm, d), BF16), pltpu.VMEM((n_tiles, d, tn), BF16)],
        compiler_params=_params(2),
        name="norm_projection",
    )(x2, norm_w, w_in)


def _input_projection(xn, w_in, cos_tables, sin_tables, tm=2048, tn=1024):
    m, d = xn.shape
    s_len = cos_tables.shape[1]
    col_tiles, tables, kinds = _section_tiles(_EPILOGUE_GROUP, tn)
    n_rotary = kinds.count("rotary")
    assert kinds[:n_rotary] == ["rotary"] * n_rotary

    def table_map(j, i):
        rotary = j < n_rotary
        return (_select(jnp.minimum(j, n_rotary - 1), tables),
                jnp.where(rotary, i % (s_len // tm), s_len // tm - 1), 0)

    table_spec = pl.BlockSpec((None, tm, LANES), table_map)
    return pl.pallas_call(
        functools.partial(_projection_kernel, tile_kinds=kinds),
        out_shape=jax.ShapeDtypeStruct((len(col_tiles) * tn // LANES, m, LANES), BF16),
        grid=(len(col_tiles), m // tm),
        in_specs=[pl.BlockSpec((tm, d), lambda j, i: (i, 0)),
                  pl.BlockSpec((d, tn), lambda j, i: (0, _select(j, col_tiles))),
                  table_spec, table_spec],
        out_specs=pl.BlockSpec((tn // LANES, tm, LANES), lambda j, i: (j, i, 0)),
        compiler_params=_params(2),
        name="epilogue_projection",
    )(xn, w_in, cos_tables, sin_tables)


def _retention_pieces(lg, q_ref, k_ref, v_ref, z_ref, o_ref, qx_ref, kz_ref, vc_ref):
    s_len = q_ref.shape[0]
    c = RET_CHUNK
    n_chunks = s_len // c

    rowq = lax.broadcasted_iota(jnp.int32, (c, RET_QK_DIM), 0).astype(F32)
    zeta = jnp.exp(lg * (c - 1.0 - rowq))
    xi = jnp.exp(lg * (rowq + 1.0))
    for n in range(n_chunks):
        sl = slice(n * c, (n + 1) * c)
        qx_ref[sl, :] = (q_ref[sl, :].astype(F32) * xi).astype(BF16)
        kz_ref[sl, :] = (k_ref[sl, :].astype(F32) * zeta).astype(BF16)
    vc_ref[...] = jnp.concatenate([v_ref[0], v_ref[1]], axis=-1)

    row = lax.broadcasted_iota(jnp.int32, (c, c), 0).astype(F32)
    col = lax.broadcasted_iota(jnp.int32, (c, c), 1).astype(F32)
    rel = row - col
    decay_mask = jnp.where(rel >= 0.0, jnp.exp(lg * jnp.maximum(rel, 0.0)), 0.0)
    chunk_decay = jnp.exp(jnp.zeros((1, RET_V_DIM), F32) + lg * c)
    yield

    state = jnp.zeros((RET_QK_DIM, RET_V_DIM), F32)
    for n in range(n_chunks):
        sl = slice(n * c, (n + 1) * c)
        vc = vc_ref[sl, :]
        scores = lax.dot_general(q_ref[sl, :], k_ref[sl, :], (((1,), (1,)), ((), ())),
                                 preferred_element_type=F32) * decay_mask
        o = jnp.dot(scores.astype(BF16), vc, preferred_element_type=F32)
        o = o + jnp.dot(qx_ref[sl, :], state.astype(BF16), preferred_element_type=F32)
        if n + 1 < n_chunks:
            kv = lax.dot_general(kz_ref[sl, :], vc, (((0,), (0,)), ((), ())),
                                 preferred_element_type=F32)
            state = state * chunk_decay + kv

        ms = jnp.mean(o * o, axis=-1, keepdims=True)
        o = o * lax.rsqrt(ms + NORM_EPS)
        silu_z = jnp.concatenate([z_ref[0, sl, :], z_ref[1, sl, :]], axis=-1).astype(F32)
        o_ref[sl, :] = (o * silu_z).astype(o_ref.dtype)
        yield


def _diff_attn_pieces(q_ref, k_ref, v_ref, z_ref, lq1_ref, lk1_ref, lq2_ref, lk2_ref, subln_ref,
                      o_ref, vc_ref):
    t = ATT_BLOCK
    s_len = vc_ref.shape[0]
    vc_ref[...] = jnp.concatenate([v_ref[0], v_ref[1]], axis=-1)

    lam = (jnp.exp(jnp.sum(lq1_ref[...] * lk1_ref[...], axis=-1, keepdims=True))
           - jnp.exp(jnp.sum(lq2_ref[...] * lk2_ref[...], axis=-1, keepdims=True))
           + LAMBDA_INIT)
    causal = (lax.broadcasted_iota(jnp.int32, (t, t), 1)
              <= lax.broadcasted_iota(jnp.int32, (t, t), 0))
    nt_dims = (((1,), (1,)), ((), ()))

    units = [(n, c) for n in reversed(range(s_len // t)) for c in range(2)]

    def scores(u):
        n, c = units[u]
        r0 = n * t
        q = q_ref[c, r0:r0 + t, :]
        s_diag = jnp.where(causal,
                           lax.dot_general(q, k_ref[c, r0:r0 + t, :], nt_dims,
                                           preferred_element_type=F32), -jnp.inf)
        m = jnp.max(s_diag, axis=-1, keepdims=True)
        s_off = None
        if n > 0:
            s_off = lax.dot_general(q, k_ref[c, 0:r0, :], nt_dims, preferred_element_type=F32)
            m = jnp.maximum(m, jnp.max(s_off, axis=-1, keepdims=True))
        return s_diag, s_off, m

    def probs(sc):
        s_diag, s_off, m = sc
        p_diag = jnp.exp2(s_diag - m)
        l = jnp.sum(p_diag, axis=-1, keepdims=True)
        p_off = None
        if s_off is not None:
            p_off = jnp.exp2(s_off - m)
            l = l + jnp.sum(p_off, axis=-1, keepdims=True)
            p_off = p_off.astype(BF16)
        return p_diag.astype(BF16), p_off, l

    def weighted(u, pr):
        n, _ = units[u]
        r0 = n * t
        p_diag, p_off, l = pr
        acc = jnp.dot(p_diag, vc_ref[r0:r0 + t, :], preferred_element_type=F32)
        if p_off is not None:
            acc = acc + jnp.dot(p_off, vc_ref[0:r0, :], preferred_element_type=F32)
        return acc, l

    def finish(n, res):
        r0 = n * t
        o = res[0][0] * (1.0 / res[0][1]) - res[1][0] * (lam / res[1][1])
        ms = jnp.mean(o * o, axis=-1, keepdims=True)
        o = o * lax.rsqrt(ms + SUBLN_EPS) * subln_ref[...] * (1.0 - LAMBDA_INIT)
        silu_z = jnp.concatenate([z_ref[0, r0:r0 + t, :], z_ref[1, r0:r0 + t, :]],
                                 axis=-1).astype(F32)
        o_ref[r0:r0 + t, :] = (o * silu_z).astype(o_ref.dtype)

    skew = ATT_SKEW
    sc, pr, res = {}, {}, {}
    for step in range(len(units) + 2 * skew):
        u_sc, u_pr, u_pv = step, step - skew, step - 2 * skew
        if 0 <= u_sc < len(units):
            sc[u_sc] = scores(u_sc)
        if 0 <= u_pr < len(units):
            pr[u_pr] = probs(sc.pop(u_pr))
        if 0 <= u_pv < len(units):
            res[u_pv] = weighted(u_pv, pr.pop(u_pv))
            if units[u_pv][1] == 1:
                finish(units[u_pv][0], (res.pop(u_pv - 1), res.pop(u_pv)))
        yield


def _mixers_kernel(lg_ref, rq_ref, rk_ref, rv_ref, rz_ref, dq_ref, dk_ref, dv_ref, dz_ref,
                   lq1_ref, lk1_ref, lq2_ref, lk2_ref, subln_ref, s1_ref, s2_ref, s3_ref,
                   oret_ref, odiff_ref, s1bf_ref, s2bf_ref, s3bf_ref,
                   qx_ref, kz_ref, rvc_ref, dvc_ref):
    for src, dst in ((s1_ref, s1bf_ref), (s2_ref, s2bf_ref), (s3_ref, s3bf_ref)):
        dst[...] = src[...].astype(BF16)

    retention = _retention_pieces(lg_ref[pl.program_id(1)], rq_ref, rk_ref, rv_ref, rz_ref,
                                  oret_ref, qx_ref, kz_ref, rvc_ref)
    attention = _diff_attn_pieces(dq_ref, dk_ref, dv_ref, dz_ref, lq1_ref, lk1_ref, lq2_ref,
                                  lk2_ref, subln_ref, odiff_ref, dvc_ref)
    live = [attention, retention]
    while live:
        for gen, count in zip((attention, retention), MIX_PATTERN):
            for _ in range(count):
                if gen in live and next(gen, "done") == "done":
                    live.remove(gen)


def _token_mixers(epi4, plain4, log_gamma, lq1, lk1, lq2, lk2, subln_w, side_weights):
    _, b, s, _ = epi4.shape
    assert RET_HEADS == DIFF_HEADS
    one = lambda name: pl.BlockSpec(
        (None, None, s, LANES), lambda bi, h, base=_slab_base(name): (base + h, bi, 0, 0))
    pair = lambda name: pl.BlockSpec(
        (2, None, s, LANES), lambda bi, h, base=_slab_base(name): (base // 2 + h, bi, 0, 0))
    vec = lambda n: pl.BlockSpec((1, n), lambda bi, h: (0, 0))
    side_rows, side_cols = side_weights[0].shape
    assert all(w.shape == (side_rows, side_cols) for w in side_weights)
    band = side_rows // (b * DIFF_HEADS)
    side_spec = pl.BlockSpec((band, side_cols), lambda bi, h: (bi * DIFF_HEADS + h, 0))
    head_out = pl.BlockSpec((None, None, s, RET_V_DIM), lambda bi, h: (h, bi, 0, 0))
    outs = pl.pallas_call(
        _mixers_kernel,
        out_shape=[jax.ShapeDtypeStruct((RET_HEADS, b, s, RET_V_DIM), BF16),
                   jax.ShapeDtypeStruct((DIFF_HEADS, b, s, DIFF_V_DIM), BF16)]
                  + [jax.ShapeDtypeStruct((side_rows, side_cols), BF16)] * len(side_weights),
        grid=(b, DIFF_HEADS),
        in_specs=[pl.BlockSpec(memory_space=pltpu.SMEM),
                  one("rq"), one("rk"), pair("rv"), pair("rz"),
                  pair("dq"), pair("dk"), pair("dv"), pair("dz"),
                  vec(DIFF_HEAD_DIM), vec(DIFF_HEAD_DIM), vec(DIFF_HEAD_DIM), vec(DIFF_HEAD_DIM),
                  vec(DIFF_V_DIM)] + [side_spec] * len(side_weights),
        out_specs=[head_out, head_out] + [side_spec] * len(side_weights),
        scratch_shapes=[pltpu.VMEM((s, RET_QK_DIM), BF16)] * 2
                       + [pltpu.VMEM((s, RET_V_DIM), BF16), pltpu.VMEM((s, DIFF_V_DIM), BF16)],
        compiler_params=_params(2),
        name="token_mixers",
    )(log_gamma, epi4, epi4, plain4, epi4, epi4, epi4, plain4, epi4,
      lq1, lk1, lq2, lk2, subln_w, *side_weights)
    return outs[0], outs[1], outs[2:]


def _merge_kernel(a_ref, b_ref, w1_ref, w2_ref, g1_ref, g2_ref, o_ref):
    kw = a_ref.shape[2]

    def up(x_ref, w_ref, rows):
        acc = None
        for k in range(x_ref.shape[0]):
            part = jnp.dot(x_ref[k, rows, :], w_ref[k * kw:(k + 1) * kw, :],
                           preferred_element_type=F32)
            acc = part if acc is None else acc + part
        return acc

    for rows in _row_bands(a_ref.shape[1], MERGE_ROW_BANDS):
        up1 = up(a_ref, w1_ref, rows)
        up2 = up(b_ref, w2_ref, rows)
        for j in range(g1_ref.shape[0]):
            sl = slice(j * LANES, (j + 1) * LANES)
            o_ref[rows, sl] = (g1_ref[j, rows, :].astype(F32) * up1[:, sl]
                               + g2_ref[j, rows, :].astype(F32) * up2[:, sl]).astype(o_ref.dtype)


def _gated_merge(o_ret, o_diff, w_ret_up_bf, w_diff_up_bf, gates, tm=512):
    nk, m, kw = o_ret.shape
    d = w_ret_up_bf.shape[1]
    gs = d // LANES
    x_spec = pl.BlockSpec((nk, tm, kw), lambda i: (0, i, 0))
    w_spec = pl.BlockSpec((nk * kw, d), lambda i: (0, 0), pipeline_mode=pl.Buffered(1))
    g_spec = lambda name: pl.BlockSpec(
        (gs, tm, LANES), lambda i, base=_slab_base(name): (base // gs, i, 0))
    return pl.pallas_call(
        _merge_kernel,
        out_shape=jax.ShapeDtypeStruct((m, d), BF16),
        grid=(m // tm,),
        in_specs=[x_spec, x_spec, w_spec, w_spec, g_spec("gr"), g_spec("gd")],
        out_specs=pl.BlockSpec((tm, d), lambda i: (i, 0)),
        compiler_params=_params(1),
        name="gated_merge",
    )(o_ret, o_diff, w_ret_up_bf, w_diff_up_bf, gates, gates)


def _outproj_kernel(mx_ref, w_ref, x_ref, fw_ref, o_ref):
    for rows in _row_bands(mx_ref.shape[0], OUT_ROW_BANDS):
        h = x_ref[rows, :] + jnp.dot(mx_ref[rows, :], w_ref[...], preferred_element_type=F32)
        ms = jnp.mean(h * h, axis=-1, keepdims=True)
        o_ref[rows, :] = h * lax.rsqrt(ms + NORM_EPS) * fw_ref[...]


def _output_projection(mixed, w_out_bf, x2, final_w, tm=512):
    m, d = x2.shape
    return pl.pallas_call(
        _outproj_kernel,
        out_shape=jax.ShapeDtypeStruct((m, d), F32),
        grid=(m // tm,),
        in_specs=[pl.BlockSpec((tm, d), lambda i: (i, 0)),
                  pl.BlockSpec((d, d), lambda i: (0, 0), pipeline_mode=pl.Buffered(1)),
                  pl.BlockSpec((tm, d), lambda i: (i, 0)),
                  pl.BlockSpec((1, d), lambda i: (0, 0))],
        out_specs=pl.BlockSpec((tm, d), lambda i: (i, 0)),
        compiler_params=_params(1),
        name="output_projection",
    )(mixed, w_out_bf, x2, final_w)


def _rotary_tables(s, inv_freq):
    pos = np.arange(s, dtype=np.float32)
    ang = pos[:, None] * inv_freq[None, :].astype(np.float32)
    emb = np.concatenate([ang, ang], axis=-1)
    half = emb.shape[-1] // 2
    sign = np.concatenate([-np.ones((half,), np.float32), np.ones((half,), np.float32)])
    return np.cos(emb), np.sin(emb) * sign[None, :]


def kernel(x, norm_w, w_in, w_ret_up, w_diff_up, w_out, lambda_q1, lambda_k1, lambda_q2,
           lambda_k2, subln_w, final_norm_w):
    b, s, d = x.shape
    assert d == D_MODEL and w_in.shape == (1, D_MODEL, IN_WIDTH)
    x2 = x.reshape(b * s, d)

    f32 = np.float32
    ret_inv_freq = np.exp(f32(-math.log(10000.0)) * np.linspace(0.0, 1.0, RET_QK_DIM // 2, dtype=f32))
    rope_inv_freq = f32(ROPE_THETA) ** (-np.arange(0, DIFF_HEAD_DIM, 2, dtype=f32) / f32(DIFF_HEAD_DIM))
    ret_cos, ret_sin = _rotary_tables(s, ret_inv_freq)
    rope_cos, rope_sin = _rotary_tables(s, rope_inv_freq)
    log_gamma = jnp.asarray(np.log1p(-np.exp2(f32(-5.0) - np.arange(RET_HEADS, dtype=f32))))

    ret_k_scale = f32(RET_QK_DIM ** -0.5)
    diff_q_scale = f32((DIFF_HEAD_DIM ** -0.5) * math.log2(math.e))
    cos_tables = jnp.asarray(
        np.stack([ret_cos, ret_cos * ret_k_scale, rope_cos * diff_q_scale, rope_cos]))
    sin_tables = jnp.asarray(
        np.stack([ret_sin, ret_sin * ret_k_scale, rope_sin * diff_q_scale, rope_sin]))

    plain, xn = _norm_projection(x2, norm_w[0][None, :], w_in[0])
    epi = _input_projection(xn, w_in[0], cos_tables, sin_tables)
    by_pos = lambda a: a.reshape(a.shape[0], b, s, LANES)

    o_ret, o_diff, (w_ret_up_bf, w_diff_up_bf, w_out_bf) = _token_mixers(
        by_pos(epi), by_pos(plain), log_gamma, lambda_q1, lambda_k1, lambda_q2, lambda_k2,
        subln_w, (w_ret_up[0], w_diff_up[0], w_out[0]))
    o_ret = o_ret.reshape(RET_HEADS, b * s, RET_V_DIM)
    o_diff = o_diff.reshape(DIFF_HEADS, b * s, DIFF_V_DIM)

    mixed = _gated_merge(o_ret, o_diff, w_ret_up_bf, w_diff_up_bf, epi)
    out = _output_projection(mixed, w_out_bf, x2, final_norm_w[None, :])
    return out.reshape(b, s, d)
```

```python
import functools
import math

import jax
import jax.numpy as jnp
import numpy as np
from jax import lax
from jax.experimental import pallas as pl
from jax.experimental.pallas import tpu as pltpu

F32 = jnp.float32
BF16 = jnp.bfloat16

D_MODEL = 2048
RET_QK_DIM = 128
RET_V_DIM = 256
RET_HEADS = D_MODEL // RET_V_DIM
DIFF_HEAD_DIM = 128
DIFF_V_DIM = 2 * DIFF_HEAD_DIM
DIFF_HEADS = D_MODEL // DIFF_V_DIM
ROPE_THETA = 10000.0
NORM_EPS = 1e-6
SUBLN_EPS = 1e-5
LAMBDA_INIT = 0.8 - 0.6 * math.exp(-0.3 * 0)

RET_QK_WIDTH = RET_HEADS * RET_QK_DIM
RET_WIDTH = RET_HEADS * RET_V_DIM
DIFF_QK_WIDTH = DIFF_HEADS * 2 * DIFF_HEAD_DIM
DIFF_WIDTH = DIFF_HEADS * DIFF_V_DIM
IN_WIDTH = 2 * RET_QK_WIDTH + 2 * RET_WIDTH + 2 * DIFF_QK_WIDTH + 2 * DIFF_WIDTH + 2 * D_MODEL

LANES = 128

_SECTIONS = (("rq", RET_QK_WIDTH, "rotary", 0), ("rk", RET_QK_WIDTH, "rotary", 1),
             ("rv", RET_WIDTH, "plain", 0), ("rz", RET_WIDTH, "silu", 0),
             ("dq", DIFF_QK_WIDTH, "rotary", 2), ("dk", DIFF_QK_WIDTH, "rotary", 3),
             ("dv", DIFF_WIDTH, "plain", 0), ("dz", DIFF_WIDTH, "silu", 0),
             ("gr", D_MODEL, "sigmoid", 0), ("gd", D_MODEL, "sigmoid", 0))
assert sum(sec[1] for sec in _SECTIONS) == IN_WIDTH
_PLAIN_GROUP = ("plain",)
_EPILOGUE_GROUP = ("rotary", "silu", "sigmoid")


def _group_sections(group):
    return [sec for kind in group for sec in _SECTIONS if sec[2] == kind]


def _slab_base(name):
    for group in (_PLAIN_GROUP, _EPILOGUE_GROUP):
        base = 0
        for sec_name, width, _, _ in _group_sections(group):
            if sec_name == name:
                return base
            base += width // LANES
    raise KeyError(name)

VMEM_LIMIT = 56 * 1024 * 1024

RET_CHUNK = 256
ATT_BLOCK = 256
ATT_SKEW = 1
PROJ_ROW_BANDS = ((1, 4), (1, 4), (1, 4), (1, 8), (1, 16), (1, 16))
NORM_PROJ_ROW_BANDS = ((1, 2), (1, 4), (1, 4))
TAIL_ROW_BANDS = ((1, 2), (1, 2))
MIX_PATTERN = (1, 1)


def _params(n_axes, vmem=VMEM_LIMIT, flags=None):
    return pltpu.CompilerParams(dimension_semantics=("arbitrary",) * n_axes,
                                vmem_limit_bytes=vmem, flags=flags)


def _rotary(x, cos, sin_signed):
    return x * cos + pltpu.roll(x, x.shape[-1] // 2, axis=x.ndim - 1) * sin_signed


def _sigmoid(g):
    return 0.5 * jnp.tanh(0.5 * g) + 0.5


def _silu(z):
    h = 0.5 * z
    return h * jnp.tanh(h) + h


def _row_bands(tm, fractions):
    bounds = [0]
    for num, den in fractions:
        bounds.append(bounds[-1] + tm * num // den)
    assert bounds[-1] == tm
    return [slice(r0, r1) for r0, r1 in zip(bounds[:-1], bounds[1:])]


def _select(j, values):
    return sum(jnp.where(j == t, v, 0) for t, v in enumerate(values) if v)


def _projection_kernel(x_ref, w_ref, cos_ref, sin_ref, o_ref, *, tile_kinds):
    epilogues = {
        "rotary": lambda a, rows: _rotary(a, cos_ref[rows, :], sin_ref[rows, :]),
        "silu": lambda a, rows: _silu(a),
        "sigmoid": lambda a, rows: _sigmoid(a),
    }
    j = pl.program_id(0)
    for kind in dict.fromkeys(tile_kinds):
        tiles = [t for t, k in enumerate(tile_kinds) if k == kind]

        @pl.when(functools.reduce(jnp.logical_or, [j == t for t in tiles]))
        def _(fn=epilogues[kind]):
            w = w_ref[...].astype(BF16)
            for rows in _row_bands(x_ref.shape[0], PROJ_ROW_BANDS):
                acc = jnp.dot(x_ref[rows, :], w, preferred_element_type=F32)
                for s in range(o_ref.shape[0]):
                    o_ref[s, rows, :] = fn(acc[:, s * LANES:(s + 1) * LANES],
                                           rows).astype(o_ref.dtype)


def _section_tiles(group, tn):
    col_of = {}
    col = 0
    for name, width, _, _ in _SECTIONS:
        assert width % tn == 0
        col_of[name] = col
        col += width
    col_tiles, tables, kinds = [], [], []
    for name, width, kind, table in _group_sections(group):
        n = width // tn
        col_tiles += [col_of[name] // tn + t for t in range(n)]
        tables += [table] * n
        kinds += [kind] * n
    return col_tiles, tables, kinds


def _norm_projection_kernel(x_ref, nw_ref, w_ref, o_ref, xn_ref, xn_scr, wbf_scr):
    j = pl.program_id(1)
    first = j == 0
    bands = _row_bands(x_ref.shape[0], NORM_PROJ_ROW_BANDS)

    @pl.when(pl.program_id(0) == 0)
    def _():
        wbf_scr[j] = w_ref[...].astype(BF16)

    def project(rows, xn):
        acc = jnp.dot(xn, wbf_scr[j], preferred_element_type=F32)
        for s in range(o_ref.shape[0]):
            o_ref[s, rows, :] = acc[:, s * LANES:(s + 1) * LANES].astype(o_ref.dtype)

    @pl.when(first)
    def _():
        for rows in bands:
            x = x_ref[rows, :]
            ms = jnp.mean(x * x, axis=-1, keepdims=True)
            xn = (x * lax.rsqrt(ms + NORM_EPS) * nw_ref[...]).astype(BF16)
            xn_scr[rows, :] = xn
            xn_ref[rows, :] = xn
            project(rows, xn)

    @pl.when(jnp.logical_not(first))
    def _():
        for rows in bands:
            project(rows, xn_scr[rows, :])


def _norm_projection(x2, norm_w, w_in, tm=512, tn=1024):
    m, d = x2.shape
    col_tiles, _, _ = _section_tiles(_PLAIN_GROUP, tn)
    n_tiles = len(col_tiles)
    w_map = lambda i, j: (0, _select(jnp.where(i == 0, j, n_tiles - 1), col_tiles))
    return pl.pallas_call(
        _norm_projection_kernel,
        out_shape=[jax.ShapeDtypeStruct((n_tiles * tn // LANES, m, LANES), BF16),
                   jax.ShapeDtypeStruct((m, d), BF16)],
        grid=(m // tm, n_tiles),
        in_specs=[pl.BlockSpec((tm, d), lambda i, j: (i, 0)),
                  pl.BlockSpec((1, d), lambda i, j: (0, 0)),
                  pl.BlockSpec((d, tn), w_map)],
        out_specs=[pl.BlockSpec((tn // LANES, tm, LANES), lambda i, j: (j, i, 0)),
                   pl.BlockSpec((tm, d), lambda i, j: (i, 0))],
        scratch_shapes=[pltpu.VMEM((tm, d), BF16), pltpu.VMEM((n_tiles, d, tn), BF16)],
        compiler_params=_params(2),
        name="norm_projection",
    )(x2, norm_w, w_in)


def _input_projection(xn, w_in, cos_tables, sin_tables, tm=2048, tn=1024):
    m, d = xn.shape
    s_len = cos_tables.shape[1]
    col_tiles, tables, kinds = _section_tiles(_EPILOGUE_GROUP, tn)
    n_rotary = kinds.count("rotary")
    assert kinds[:n_rotary] == ["rotary"] * n_rotary

    def table_map(j, i):
        rotary = j < n_rotary
        return (_select(jnp.minimum(j, n_rotary - 1), tables),
                jnp.where(rotary, i % (s_len // tm), s_len // tm - 1), 0)

    table_spec = pl.BlockSpec((None, tm, LANES), table_map)
    return pl.pallas_call(
        functools.partial(_projection_kernel, tile_kinds=kinds),
        out_shape=jax.ShapeDtypeStruct((len(col_tiles) * tn // LANES, m, LANES), BF16),
        grid=(len(col_tiles), m // tm),
        in_specs=[pl.BlockSpec((tm, d), lambda j, i: (i, 0)),
                  pl.BlockSpec((d, tn), lambda j, i: (0, _select(j, col_tiles))),
                  table_spec, table_spec],
        out_specs=pl.BlockSpec((tn // LANES, tm, LANES), lambda j, i: (j, i, 0)),
        compiler_params=_params(2),
        name="epilogue_projection",
    )(xn, w_in, cos_tables, sin_tables)


def _retention_pieces(lg, q_ref, k_ref, v_ref, z_ref, o_ref, qx_ref, kz_ref, vc_ref):
    s_len = q_ref.shape[0]
    c = RET_CHUNK
    n_chunks = s_len // c

    rowq = lax.broadcasted_iota(jnp.int32, (c, RET_QK_DIM), 0).astype(F32)
    zeta = jnp.exp(lg * (c - 1.0 - rowq))
    xi = jnp.exp(lg * (rowq + 1.0))
    for n in range(n_chunks):
        sl = slice(n * c, (n + 1) * c)
        qx_ref[sl, :] = (q_ref[sl, :].astype(F32) * xi).astype(BF16)
        kz_ref[sl, :] = (k_ref[sl, :].astype(F32) * zeta).astype(BF16)
    vc_ref[...] = jnp.concatenate([v_ref[0], v_ref[1]], axis=-1)

    row = lax.broadcasted_iota(jnp.int32, (c, c), 0).astype(F32)
    col = lax.broadcasted_iota(jnp.int32, (c, c), 1).astype(F32)
    rel = row - col
    decay_mask = jnp.where(rel >= 0.0, jnp.exp(lg * jnp.maximum(rel, 0.0)), 0.0)
    chunk_decay = jnp.exp(jnp.zeros((1, RET_V_DIM), F32) + lg * c)
    yield

    state = jnp.zeros((RET_QK_DIM, RET_V_DIM), F32)
    for n in range(n_chunks):
        sl = slice(n * c, (n + 1) * c)
        vc = vc_ref[sl, :]
        scores = lax.dot_general(q_ref[sl, :], k_ref[sl, :], (((1,), (1,)), ((), ())),
                                 preferred_element_type=F32) * decay_mask
        o = jnp.dot(scores.astype(BF16), vc, preferred_element_type=F32)
        o = o + jnp.dot(qx_ref[sl, :], state.astype(BF16), preferred_element_type=F32)
        if n + 1 < n_chunks:
            kv = lax.dot_general(kz_ref[sl, :], vc, (((0,), (0,)), ((), ())),
                                 preferred_element_type=F32)
            state = state * chunk_decay + kv

        ms = jnp.mean(o * o, axis=-1, keepdims=True)
        o = o * lax.rsqrt(ms + NORM_EPS)
        silu_z = jnp.concatenate([z_ref[0, sl, :], z_ref[1, sl, :]], axis=-1).astype(F32)
        o_ref[sl, :] = (o * silu_z).astype(o_ref.dtype)
        yield


def _diff_attn_pieces(q_ref, k_ref, v_ref, z_ref, lq1_ref, lk1_ref, lq2_ref, lk2_ref, subln_ref,
                      o_ref, vc_ref):
    t = ATT_BLOCK
    s_len = vc_ref.shape[0]
    vc_ref[...] = jnp.concatenate([v_ref[0], v_ref[1]], axis=-1)

    lam = (jnp.exp(jnp.sum(lq1_ref[...] * lk1_ref[...], axis=-1, keepdims=True))
           - jnp.exp(jnp.sum(lq2_ref[...] * lk2_ref[...], axis=-1, keepdims=True))
           + LAMBDA_INIT)
    causal = (lax.broadcasted_iota(jnp.int32, (t, t), 1)
              <= lax.broadcasted_iota(jnp.int32, (t, t), 0))
    nt_dims = (((1,), (1,)), ((), ()))

    units = [(n, c) for n in reversed(range(s_len // t)) for c in range(2)]

    def scores(u):
        n, c = units[u]
        r0 = n * t
        q = q_ref[c, r0:r0 + t, :]
        s_diag = jnp.where(causal,
                           lax.dot_general(q, k_ref[c, r0:r0 + t, :], nt_dims,
                                           preferred_element_type=F32), -jnp.inf)
        m = jnp.max(s_diag, axis=-1, keepdims=True)
        s_off = None
        if n > 0:
            s_off = lax.dot_general(q, k_ref[c, 0:r0, :], nt_dims, preferred_element_type=F32)
            m = jnp.maximum(m, jnp.max(s_off, axis=-1, keepdims=True))
        return s_diag, s_off, m

    def probs(sc):
        s_diag, s_off, m = sc
        p_diag = jnp.exp2(s_diag - m)
        l = jnp.sum(p_diag, axis=-1, keepdims=True)
        p_off = None
        if s_off is not None:
            p_off = jnp.exp2(s_off - m)
            l = l + jnp.sum(p_off, axis=-1, keepdims=True)
            p_off = p_off.astype(BF16)
        return p_diag.astype(BF16), p_off, l

    def weighted(u, pr):
        n, _ = units[u]
        r0 = n * t
        p_diag, p_off, l = pr
        acc = jnp.dot(p_diag, vc_ref[r0:r0 + t, :], preferred_element_type=F32)
        if p_off is not None:
            acc = acc + jnp.dot(p_off, vc_ref[0:r0, :], preferred_element_type=F32)
        return acc, l

    def finish(n, res):
        r0 = n * t
        o = res[0][0] * (1.0 / res[0][1]) - res[1][0] * (lam / res[1][1])
        ms = jnp.mean(o * o, axis=-1, keepdims=True)
        o = o * lax.rsqrt(ms + SUBLN_EPS) * subln_ref[...] * (1.0 - LAMBDA_INIT)
        silu_z = jnp.concatenate([z_ref[0, r0:r0 + t, :], z_ref[1, r0:r0 + t, :]],
                                 axis=-1).astype(F32)
        o_ref[r0:r0 + t, :] = (o * silu_z).astype(o_ref.dtype)

    skew = ATT_SKEW
    sc, pr, res = {}, {}, {}
    for step in range(len(units) + 2 * skew):
        u_sc, u_pr, u_pv = step, step - skew, step - 2 * skew
        if 0 <= u_sc < len(units):
            sc[u_sc] = scores(u_sc)
        if 0 <= u_pr < len(units):
            pr[u_pr] = probs(sc.pop(u_pr))
        if 0 <= u_pv < len(units):
            res[u_pv] = weighted(u_pv, pr.pop(u_pv))
            if units[u_pv][1] == 1:
                finish(units[u_pv][0], (res.pop(u_pv - 1), res.pop(u_pv)))
        yield


def _mixers_kernel(lg_ref, rq_ref, rk_ref, rv_ref, rz_ref, dq_ref, dk_ref, dv_ref, dz_ref,
                   lq1_ref, lk1_ref, lq2_ref, lk2_ref, subln_ref, s1_ref, s2_ref, s3_ref,
                   oret_ref, odiff_ref, s1bf_ref, s2bf_ref, s3bf_ref,
                   qx_ref, kz_ref, rvc_ref, dvc_ref):
    for src, dst in ((s1_ref, s1bf_ref), (s2_ref, s2bf_ref), (s3_ref, s3bf_ref)):
        dst[...] = src[...].astype(BF16)

    retention = _retention_pieces(lg_ref[pl.program_id(1)], rq_ref, rk_ref, rv_ref, rz_ref,
                                  oret_ref, qx_ref, kz_ref, rvc_ref)
    attention = _diff_attn_pieces(dq_ref, dk_ref, dv_ref, dz_ref, lq1_ref, lk1_ref, lq2_ref,
                                  lk2_ref, subln_ref, odiff_ref, dvc_ref)
    live = [attention, retention]
    while live:
        for gen, count in zip((attention, retention), MIX_PATTERN):
            for _ in range(count):
                if gen in live and next(gen, "done") == "done":
                    live.remove(gen)


def _token_mixers(epi4, plain4, log_gamma, lq1, lk1, lq2, lk2, subln_w, side_weights):
    _, b, s, _ = epi4.shape
    assert RET_HEADS == DIFF_HEADS
    one = lambda name: pl.BlockSpec(
        (None, None, s, LANES), lambda bi, h, base=_slab_base(name): (base + h, bi, 0, 0))
    pair = lambda name: pl.BlockSpec(
        (2, None, s, LANES), lambda bi, h, base=_slab_base(name): (base // 2 + h, bi, 0, 0))
    vec = lambda n: pl.BlockSpec((1, n), lambda bi, h: (0, 0))
    side_rows, side_cols = side_weights[0].shape
    assert all(w.shape == (side_rows, side_cols) for w in side_weights)
    band = side_rows // (b * DIFF_HEADS)
    side_spec = pl.BlockSpec((band, side_cols), lambda bi, h: (bi * DIFF_HEADS + h, 0))
    head_out = pl.BlockSpec((None, None, s, RET_V_DIM), lambda bi, h: (h, bi, 0, 0))
    outs = pl.pallas_call(
        _mixers_kernel,
        out_shape=[jax.ShapeDtypeStruct((RET_HEADS, b, s, RET_V_DIM), BF16),
                   jax.ShapeDtypeStruct((DIFF_HEADS, b, s, DIFF_V_DIM), BF16)]
                  + [jax.ShapeDtypeStruct((side_rows, side_cols), BF16)] * len(side_weights),
        grid=(b, DIFF_HEADS),
        in_specs=[pl.BlockSpec(memory_space=pltpu.SMEM),
                  one("rq"), one("rk"), pair("rv"), pair("rz"),
                  pair("dq"), pair("dk"), pair("dv"), pair("dz"),
                  vec(DIFF_HEAD_DIM), vec(DIFF_HEAD_DIM), vec(DIFF_HEAD_DIM), vec(DIFF_HEAD_DIM),
                  vec(DIFF_V_DIM)] + [side_spec] * len(side_weights),
        out_specs=[head_out, head_out] + [side_spec] * len(side_weights),
        scratch_shapes=[pltpu.VMEM((s, RET_QK_DIM), BF16)] * 2
                       + [pltpu.VMEM((s, RET_V_DIM), BF16), pltpu.VMEM((s, DIFF_V_DIM), BF16)],
        compiler_params=_params(2),
        name="token_mixers",
    )(log_gamma, epi4, epi4, plain4, epi4, epi4, epi4, plain4, epi4,
      lq1, lk1, lq2, lk2, subln_w, *side_weights)
    return outs[0], outs[1], outs[2:]


def _tail_kernel(a_ref, b_ref, g1_ref, g2_ref, x_ref, w1_ref, w2_ref, wo_ref, fw_ref, o_ref):
    kw = a_ref.shape[2]

    def up(x_slabs, w_ref, rows):
        acc = None
        for k in range(x_slabs.shape[0]):
            part = jnp.dot(x_slabs[k, rows, :], w_ref[k * kw:(k + 1) * kw, :],
                           preferred_element_type=F32)
            acc = part if acc is None else acc + part
        return acc

    for rows in _row_bands(a_ref.shape[1], TAIL_ROW_BANDS):
        up1 = up(a_ref, w1_ref, rows)
        up2 = up(b_ref, w2_ref, rows)
        mixed = jnp.concatenate(
            [(g1_ref[j, rows, :].astype(F32) * up1[:, j * LANES:(j + 1) * LANES]
              + g2_ref[j, rows, :].astype(F32) * up2[:, j * LANES:(j + 1) * LANES]).astype(BF16)
             for j in range(g1_ref.shape[0])], axis=-1)
        h = x_ref[rows, :] + jnp.dot(mixed, wo_ref[...], preferred_element_type=F32)
        ms = jnp.mean(h * h, axis=-1, keepdims=True)
        o_ref[rows, :] = h * lax.rsqrt(ms + NORM_EPS) * fw_ref[...]


def _merge_and_output(o_ret, o_diff, gates, x2, w_ret_up_bf, w_diff_up_bf, w_out_bf, final_w,
                      tm=256):
    nk, m, kw = o_ret.shape
    d = x2.shape[1]
    gs = d // LANES
    x_spec = pl.BlockSpec((nk, tm, kw), lambda i: (0, i, 0))
    w_spec = pl.BlockSpec((d, d), lambda i: (0, 0), pipeline_mode=pl.Buffered(1))
    g_spec = lambda name: pl.BlockSpec(
        (gs, tm, LANES), lambda i, base=_slab_base(name): (base // gs, i, 0))
    row_spec = pl.BlockSpec((tm, d), lambda i: (i, 0))
    return pl.pallas_call(
        _tail_kernel,
        out_shape=jax.ShapeDtypeStruct((m, d), F32),
        grid=(m // tm,),
        in_specs=[x_spec, x_spec, g_spec("gr"), g_spec("gd"), row_spec, w_spec, w_spec, w_spec,
                  pl.BlockSpec((1, d), lambda i: (0, 0))],
        out_specs=row_spec,
        compiler_params=_params(1),
        name="merge_and_output",
    )(o_ret, o_diff, gates, gates, x2, w_ret_up_bf, w_diff_up_bf, w_out_bf, final_w)


def _rotary_tables(s, inv_freq):
    pos = np.arange(s, dtype=np.float32)
    ang = pos[:, None] * inv_freq[None, :].astype(np.float32)
    emb = np.concatenate([ang, ang], axis=-1)
    half = emb.shape[-1] // 2
    sign = np.concatenate([-np.ones((half,), np.float32), np.ones((half,), np.float32)])
    return np.cos(emb), np.sin(emb) * sign[None, :]


def kernel(x, norm_w, w_in, w_ret_up, w_diff_up, w_out, lambda_q1, lambda_k1, lambda_q2,
           lambda_k2, subln_w, final_norm_w):
    b, s, d = x.shape
    assert d == D_MODEL and w_in.shape == (1, D_MODEL, IN_WIDTH)
    x2 = x.reshape(b * s, d)

    f32 = np.float32
    ret_inv_freq = np.exp(f32(-math.log(10000.0)) * np.linspace(0.0, 1.0, RET_QK_DIM // 2, dtype=f32))
    rope_inv_freq = f32(ROPE_THETA) ** (-np.arange(0, DIFF_HEAD_DIM, 2, dtype=f32) / f32(DIFF_HEAD_DIM))
    ret_cos, ret_sin = _rotary_tables(s, ret_inv_freq)
    rope_cos, rope_sin = _rotary_tables(s, rope_inv_freq)
    log_gamma = jnp.asarray(np.log1p(-np.exp2(f32(-5.0) - np.arange(RET_HEADS, dtype=f32))))

    ret_k_scale = f32(RET_QK_DIM ** -0.5)
    diff_q_scale = f32((DIFF_HEAD_DIM ** -0.5) * math.log2(math.e))
    cos_tables = jnp.asarray(
        np.stack([ret_cos, ret_cos * ret_k_scale, rope_cos * diff_q_scale, rope_cos]))
    sin_tables = jnp.asarray(
        np.stack([ret_sin, ret_sin * ret_k_scale, rope_sin * diff_q_scale, rope_sin]))

    plain, xn = _norm_projection(x2, norm_w[0][None, :], w_in[0])
    epi = _input_projection(xn, w_in[0], cos_tables, sin_tables)
    by_pos = lambda a: a.reshape(a.shape[0], b, s, LANES)

    o_ret, o_diff, (w_ret_up_bf, w_diff_up_bf, w_out_bf) = _token_mixers(
        by_pos(epi), by_pos(plain), log_gamma, lambda_q1, lambda_k1, lambda_q2, lambda_k2,
        subln_w, (w_ret_up[0], w_diff_up[0], w_out[0]))
    o_ret = o_ret.reshape(RET_HEADS, b * s, RET_V_DIM)
    o_diff = o_diff.reshape(DIFF_HEADS, b * s, DIFF_V_DIM)

    out = _merge_and_output(o_ret, o_diff, epi, x2, w_ret_up_bf, w_diff_up_bf, w_out_bf,
                            final_norm_w[None, :])
    return out.reshape(b, s, d)
```

```python
import functools
import math

import jax
import jax.numpy as jnp
import numpy as np
from jax import lax
from jax.experimental import pallas as pl
from jax.experimental.pallas import tpu as pltpu

F32 = jnp.float32
BF16 = jnp.bfloat16

D_MODEL = 2048
RET_QK_DIM = 128
RET_V_DIM = 256
RET_HEADS = D_MODEL // RET_V_DIM
DIFF_HEAD_DIM = 128
DIFF_V_DIM = 2 * DIFF_HEAD_DIM
DIFF_HEADS = D_MODEL // DIFF_V_DIM
ROPE_THETA = 10000.0
NORM_EPS = 1e-6
SUBLN_EPS = 1e-5
LAMBDA_INIT = 0.8 - 0.6 * math.exp(-0.3 * 0)

RET_QK_WIDTH = RET_HEADS * RET_QK_DIM
RET_WIDTH = RET_HEADS * RET_V_DIM
DIFF_QK_WIDTH = DIFF_HEADS * 2 * DIFF_HEAD_DIM
DIFF_WIDTH = DIFF_HEADS * DIFF_V_DIM
IN_WIDTH = 2 * RET_QK_WIDTH + 2 * RET_WIDTH + 2 * DIFF_QK_WIDTH + 2 * DIFF_WIDTH + 2 * D_MODEL

LANES = 128

_SECTIONS = (("rq", RET_QK_WIDTH, "rotary", 0), ("rk", RET_QK_WIDTH, "rotary", 1),
             ("rv", RET_WIDTH, "plain", 0), ("rz", RET_WIDTH, "silu", 0),
             ("dq", DIFF_QK_WIDTH, "rotary", 2), ("dk", DIFF_QK_WIDTH, "rotary", 3),
             ("dv", DIFF_WIDTH, "plain", 0), ("dz", DIFF_WIDTH, "silu", 0),
             ("gr", D_MODEL, "sigmoid", 0), ("gd", D_MODEL, "sigmoid", 0))
assert sum(sec[1] for sec in _SECTIONS) == IN_WIDTH
_NORM_GROUP = ("rv",)
_MAIN_GROUP = ("rq", "rk", "dq", "dk", "dv", "rz", "dz", "gr", "gd")
assert sorted(_NORM_GROUP + _MAIN_GROUP) == sorted(sec[0] for sec in _SECTIONS)


def _group_sections(group):
    return [next(sec for sec in _SECTIONS if sec[0] == name) for name in group]


def _slab_base(name):
    for group in (_NORM_GROUP, _MAIN_GROUP):
        base = 0
        for sec_name, width, _, _ in _group_sections(group):
            if sec_name == name:
                return base
            base += width // LANES
    raise KeyError(name)


VMEM_LIMIT = 56 * 1024 * 1024
NORM_PROJ_VMEM_LIMIT = 58 * 1024 * 1024

RET_CHUNK = 256
ATT_BLOCK = 256
ATT_SKEW = 1
PROJ_ROW_BANDS = ((1, 4), (1, 4), (1, 4), (1, 8), (1, 16), (1, 16))
NORM_PROJ_ROW_BANDS = ((1, 4), (1, 4), (1, 4), (1, 8), (1, 8))
TAIL_ROW_BANDS = ((1, 2), (1, 2))
MIX_PATTERN = (1, 1)


def _params(n_axes, vmem=VMEM_LIMIT, flags=None):
    return pltpu.CompilerParams(dimension_semantics=("arbitrary",) * n_axes,
                                vmem_limit_bytes=vmem, flags=flags)


def _rotary(x, cos, sin_signed):
    return x * cos + pltpu.roll(x, x.shape[-1] // 2, axis=x.ndim - 1) * sin_signed


def _sigmoid(g):
    return 0.5 * jnp.tanh(0.5 * g) + 0.5


def _silu(z):
    h = 0.5 * z
    return h * jnp.tanh(h) + h


def _row_bands(tm, fractions):
    bounds = [0]
    for num, den in fractions:
        bounds.append(bounds[-1] + tm * num // den)
    assert bounds[-1] == tm
    return [slice(r0, r1) for r0, r1 in zip(bounds[:-1], bounds[1:])]


def _select(j, values):
    return sum(jnp.where(j == t, v, 0) for t, v in enumerate(values) if v)


def _projection_kernel(x_ref, w_ref, cos_ref, sin_ref, o_ref, *, tile_kinds):
    epilogues = {
        "rotary": lambda a, rows: _rotary(a, cos_ref[rows, :], sin_ref[rows, :]),
        "plain": lambda a, rows: a,
        "silu": lambda a, rows: _silu(a),
        "sigmoid": lambda a, rows: _sigmoid(a),
    }
    j = pl.program_id(0)
    for kind in dict.fromkeys(tile_kinds):
        tiles = [t for t, k in enumerate(tile_kinds) if k == kind]

        @pl.when(functools.reduce(jnp.logical_or, [j == t for t in tiles]))
        def _(fn=epilogues[kind]):
            w = w_ref[...].astype(BF16)
            for rows in _row_bands(x_ref.shape[0], PROJ_ROW_BANDS):
                acc = jnp.dot(x_ref[rows, :], w, preferred_element_type=F32)
                for s in range(o_ref.shape[0]):
                    o_ref[s, rows, :] = fn(acc[:, s * LANES:(s + 1) * LANES],
                                           rows).astype(o_ref.dtype)


def _section_tiles(group, tn):
    col_of = {}
    col = 0
    for name, width, _, _ in _SECTIONS:
        col_of[name] = col
        col += width
    col_tiles, tables, kinds = [], [], []
    for name, width, kind, table in _group_sections(group):
        assert width % tn == 0 and col_of[name] % tn == 0
        n = width // tn
        col_tiles += [col_of[name] // tn + t for t in range(n)]
        tables += [table] * n
        kinds += [kind] * n
    return col_tiles, tables, kinds


def _norm_projection_kernel(x_ref, nw_ref, w_ref, o_ref, xn_ref):
    w = w_ref[...].astype(BF16)
    for rows in _row_bands(x_ref.shape[0], NORM_PROJ_ROW_BANDS):
        x = x_ref[rows, :]
        ms = jnp.mean(x * x, axis=-1, keepdims=True)
        xn = (x * lax.rsqrt(ms + NORM_EPS) * nw_ref[...]).astype(BF16)
        xn_ref[rows, :] = xn
        acc = jnp.dot(xn, w, preferred_element_type=F32)
        for s in range(o_ref.shape[0]):
            o_ref[s, rows, :] = acc[:, s * LANES:(s + 1) * LANES].astype(o_ref.dtype)


def _norm_projection(x2, norm_w, w_in, tm=1024):
    m, d = x2.shape
    (_, tn, kind, _), = _group_sections(_NORM_GROUP)
    assert kind == "plain"
    (col_tile,), _, _ = _section_tiles(_NORM_GROUP, tn)
    return pl.pallas_call(
        _norm_projection_kernel,
        out_shape=[jax.ShapeDtypeStruct((tn // LANES, m, LANES), BF16),
                   jax.ShapeDtypeStruct((m, d), BF16)],
        grid=(m // tm,),
        in_specs=[pl.BlockSpec((tm, d), lambda i: (i, 0)),
                  pl.BlockSpec((1, d), lambda i: (0, 0)),
                  pl.BlockSpec((d, tn), lambda i: (0, col_tile), pipeline_mode=pl.Buffered(1))],
        out_specs=[pl.BlockSpec((tn // LANES, tm, LANES), lambda i: (0, i, 0)),
                   pl.BlockSpec((tm, d), lambda i: (i, 0))],
        compiler_params=_params(1, vmem=NORM_PROJ_VMEM_LIMIT),
        name="norm_projection",
    )(x2, norm_w, w_in)


def _input_projection(xn, w_in, cos_tables, sin_tables, tm=2048, tn=1024):
    m, d = xn.shape
    s_len = cos_tables.shape[1]
    col_tiles, tables, kinds = _section_tiles(_MAIN_GROUP, tn)
    n_rotary = kinds.count("rotary")
    assert kinds[:n_rotary] == ["rotary"] * n_rotary

    def table_map(j, i):
        rotary = j < n_rotary
        return (_select(jnp.minimum(j, n_rotary - 1), tables),
                jnp.where(rotary, i % (s_len // tm), s_len // tm - 1), 0)

    table_spec = pl.BlockSpec((None, tm, LANES), table_map)
    return pl.pallas_call(
        functools.partial(_projection_kernel, tile_kinds=kinds),
        out_shape=jax.ShapeDtypeStruct((len(col_tiles) * tn // LANES, m, LANES), BF16),
        grid=(len(col_tiles), m // tm),
        in_specs=[pl.BlockSpec((tm, d), lambda j, i: (i, 0)),
                  pl.BlockSpec((d, tn), lambda j, i: (0, _select(j, col_tiles))),
                  table_spec, table_spec],
        out_specs=pl.BlockSpec((tn // LANES, tm, LANES), lambda j, i: (j, i, 0)),
        compiler_params=_params(2),
        name="epilogue_projection",
    )(xn, w_in, cos_tables, sin_tables)


def _retention_pieces(lg, q_ref, k_ref, v_ref, z_ref, o_ref, qx_ref, kz_ref, vc_ref):
    s_len = q_ref.shape[0]
    c = RET_CHUNK
    n_chunks = s_len // c

    rowq = lax.broadcasted_iota(jnp.int32, (c, RET_QK_DIM), 0).astype(F32)
    zeta = jnp.exp(lg * (c - 1.0 - rowq))
    xi = jnp.exp(lg * (rowq + 1.0))
    for n in range(n_chunks):
        sl = slice(n * c, (n + 1) * c)
        qx_ref[sl, :] = (q_ref[sl, :].astype(F32) * xi).astype(BF16)
        kz_ref[sl, :] = (k_ref[sl, :].astype(F32) * zeta).astype(BF16)
    vc_ref[...] = jnp.concatenate([v_ref[0], v_ref[1]], axis=-1)

    row = lax.broadcasted_iota(jnp.int32, (c, c), 0).astype(F32)
    col = lax.broadcasted_iota(jnp.int32, (c, c), 1).astype(F32)
    rel = row - col
    decay_mask = jnp.where(rel >= 0.0, jnp.exp(lg * jnp.maximum(rel, 0.0)), 0.0)
    chunk_decay = jnp.exp(jnp.zeros((1, RET_V_DIM), F32) + lg * c)
    yield

    state = jnp.zeros((RET_QK_DIM, RET_V_DIM), F32)
    for n in range(n_chunks):
        sl = slice(n * c, (n + 1) * c)
        vc = vc_ref[sl, :]
        scores = lax.dot_general(q_ref[sl, :], k_ref[sl, :], (((1,), (1,)), ((), ())),
                                 preferred_element_type=F32) * decay_mask
        o = jnp.dot(scores.astype(BF16), vc, preferred_element_type=F32)
        o = o + jnp.dot(qx_ref[sl, :], state.astype(BF16), preferred_element_type=F32)
        if n + 1 < n_chunks:
            kv = lax.dot_general(kz_ref[sl, :], vc, (((0,), (0,)), ((), ())),
                                 preferred_element_type=F32)
            state = state * chunk_decay + kv

        ms = jnp.mean(o * o, axis=-1, keepdims=True)
        o = o * lax.rsqrt(ms + NORM_EPS)
        silu_z = jnp.concatenate([z_ref[0, sl, :], z_ref[1, sl, :]], axis=-1).astype(F32)
        o_ref[sl, :] = (o * silu_z).astype(o_ref.dtype)
        yield


def _diff_attn_pieces(q_ref, k_ref, v_ref, z_ref, lq1_ref, lk1_ref, lq2_ref, lk2_ref, subln_ref,
                      o_ref, vc_ref):
    t = ATT_BLOCK
    s_len = vc_ref.shape[0]
    vc_ref[...] = jnp.concatenate([v_ref[0], v_ref[1]], axis=-1)

    lam = (jnp.exp(jnp.sum(lq1_ref[...] * lk1_ref[...], axis=-1, keepdims=True))
           - jnp.exp(jnp.sum(lq2_ref[...] * lk2_ref[...], axis=-1, keepdims=True))
           + LAMBDA_INIT)
    causal = (lax.broadcasted_iota(jnp.int32, (t, t), 1)
              <= lax.broadcasted_iota(jnp.int32, (t, t), 0))
    nt_dims = (((1,), (1,)), ((), ()))

    units = [(n, c) for n in reversed(range(s_len // t)) for c in range(2)]

    def scores(u):
        n, c = units[u]
        r0 = n * t
        q = q_ref[c, r0:r0 + t, :]
        s_diag = jnp.where(causal,
                           lax.dot_general(q, k_ref[c, r0:r0 + t, :], nt_dims,
                                           preferred_element_type=F32), -jnp.inf)
        m = jnp.max(s_diag, axis=-1, keepdims=True)
        s_off = None
        if n > 0:
            s_off = lax.dot_general(q, k_ref[c, 0:r0, :], nt_dims, preferred_element_type=F32)
            m = jnp.maximum(m, jnp.max(s_off, axis=-1, keepdims=True))
        return s_diag, s_off, m

    def probs(sc):
        s_diag, s_off, m = sc
        p_diag = jnp.exp2(s_diag - m)
        l = jnp.sum(p_diag, axis=-1, keepdims=True)
        p_off = None
        if s_off is not None:
            p_off = jnp.exp2(s_off - m)
            l = l + jnp.sum(p_off, axis=-1, keepdims=True)
            p_off = p_off.astype(BF16)
        return p_diag.astype(BF16), p_off, l

    def weighted(u, pr):
        n, _ = units[u]
        r0 = n * t
        p_diag, p_off, l = pr
        acc = jnp.dot(p_diag, vc_ref[r0:r0 + t, :], preferred_element_type=F32)
        if p_off is not None:
            acc = acc + jnp.dot(p_off, vc_ref[0:r0, :], preferred_element_type=F32)
        return acc, l

    def finish(n, res):
        r0 = n * t
        o = res[0][0] * (1.0 / res[0][1]) - res[1][0] * (lam / res[1][1])
        ms = jnp.mean(o * o, axis=-1, keepdims=True)
        o = o * lax.rsqrt(ms + SUBLN_EPS) * subln_ref[...] * (1.0 - LAMBDA_INIT)
        silu_z = jnp.concatenate([z_ref[0, r0:r0 + t, :], z_ref[1, r0:r0 + t, :]],
                                 axis=-1).astype(F32)
        o_ref[r0:r0 + t, :] = (o * silu_z).astype(o_ref.dtype)

    skew = ATT_SKEW
    sc, pr, res = {}, {}, {}
    for step in range(len(units) + 2 * skew):
        u_sc, u_pr, u_pv = step, step - skew, step - 2 * skew
        if 0 <= u_sc < len(units):
            sc[u_sc] = scores(u_sc)
        if 0 <= u_pr < len(units):
            pr[u_pr] = probs(sc.pop(u_pr))
        if 0 <= u_pv < len(units):
            res[u_pv] = weighted(u_pv, pr.pop(u_pv))
            if units[u_pv][1] == 1:
                finish(units[u_pv][0], (res.pop(u_pv - 1), res.pop(u_pv)))
        yield


def _mixers_kernel(lg_ref, rq_ref, rk_ref, rv_ref, rz_ref, dq_ref, dk_ref, dv_ref, dz_ref,
                   lq1_ref, lk1_ref, lq2_ref, lk2_ref, subln_ref, s1_ref, s2_ref, s3_ref,
                   oret_ref, odiff_ref, s1bf_ref, s2bf_ref, s3bf_ref,
                   qx_ref, kz_ref, rvc_ref, dvc_ref):
    for src, dst in ((s1_ref, s1bf_ref), (s2_ref, s2bf_ref), (s3_ref, s3bf_ref)):
        dst[...] = src[...].astype(BF16)

    retention = _retention_pieces(lg_ref[pl.program_id(1)], rq_ref, rk_ref, rv_ref, rz_ref,
                                  oret_ref, qx_ref, kz_ref, rvc_ref)
    attention = _diff_attn_pieces(dq_ref, dk_ref, dv_ref, dz_ref, lq1_ref, lk1_ref, lq2_ref,
                                  lk2_ref, subln_ref, odiff_ref, dvc_ref)
    live = [attention, retention]
    while live:
        for gen, count in zip((attention, retention), MIX_PATTERN):
            for _ in range(count):
                if gen in live and next(gen, "done") == "done":
                    live.remove(gen)


def _token_mixers(main4, norm4, log_gamma, lq1, lk1, lq2, lk2, subln_w, side_weights):
    _, b, s, _ = main4.shape
    by_section = lambda names: [norm4 if n in _NORM_GROUP else main4 for n in names]
    assert RET_HEADS == DIFF_HEADS
    one = lambda name: pl.BlockSpec(
        (None, None, s, LANES), lambda bi, h, base=_slab_base(name): (base + h, bi, 0, 0))
    pair = lambda name: pl.BlockSpec(
        (2, None, s, LANES), lambda bi, h, base=_slab_base(name): (base // 2 + h, bi, 0, 0))
    vec = lambda n: pl.BlockSpec((1, n), lambda bi, h: (0, 0))
    side_rows, side_cols = side_weights[0].shape
    assert all(w.shape == (side_rows, side_cols) for w in side_weights)
    band = side_rows // (b * DIFF_HEADS)
    side_spec = pl.BlockSpec((band, side_cols), lambda bi, h: (bi * DIFF_HEADS + h, 0))
    head_out = pl.BlockSpec((None, None, s, RET_V_DIM), lambda bi, h: (h, bi, 0, 0))
    outs = pl.pallas_call(
        _mixers_kernel,
        out_shape=[jax.ShapeDtypeStruct((RET_HEADS, b, s, RET_V_DIM), BF16),
                   jax.ShapeDtypeStruct((DIFF_HEADS, b, s, DIFF_V_DIM), BF16)]
                  + [jax.ShapeDtypeStruct((side_rows, side_cols), BF16)] * len(side_weights),
        grid=(b, DIFF_HEADS),
        in_specs=[pl.BlockSpec(memory_space=pltpu.SMEM),
                  one("rq"), one("rk"), pair("rv"), pair("rz"),
                  pair("dq"), pair("dk"), pair("dv"), pair("dz"),
                  vec(DIFF_HEAD_DIM), vec(DIFF_HEAD_DIM), vec(DIFF_HEAD_DIM), vec(DIFF_HEAD_DIM),
                  vec(DIFF_V_DIM)] + [side_spec] * len(side_weights),
        out_specs=[head_out, head_out] + [side_spec] * len(side_weights),
        scratch_shapes=[pltpu.VMEM((s, RET_QK_DIM), BF16)] * 2
                       + [pltpu.VMEM((s, RET_V_DIM), BF16), pltpu.VMEM((s, DIFF_V_DIM), BF16)],
        compiler_params=_params(2),
        name="token_mixers",
    )(log_gamma, *by_section(("rq", "rk", "rv", "rz", "dq", "dk", "dv", "dz")),
      lq1, lk1, lq2, lk2, subln_w, *side_weights)
    return outs[0], outs[1], outs[2:]


def _tail_kernel(a_ref, b_ref, g1_ref, g2_ref, x_ref, w1_ref, w2_ref, wo_ref, fw_ref, o_ref):
    kw = a_ref.shape[2]

    def up(x_slabs, w_ref, rows):
        acc = None
        for k in range(x_slabs.shape[0]):
            part = jnp.dot(x_slabs[k, rows, :], w_ref[k * kw:(k + 1) * kw, :],
                           preferred_element_type=F32)
            acc = part if acc is None else acc + part
        return acc

    for rows in _row_bands(a_ref.shape[1], TAIL_ROW_BANDS):
        up1 = up(a_ref, w1_ref, rows)
        up2 = up(b_ref, w2_ref, rows)
        mixed = jnp.concatenate(
            [(g1_ref[j, rows, :].astype(F32) * up1[:, j * LANES:(j + 1) * LANES]
              + g2_ref[j, rows, :].astype(F32) * up2[:, j * LANES:(j + 1) * LANES]).astype(BF16)
             for j in range(g1_ref.shape[0])], axis=-1)
        h = x_ref[rows, :] + jnp.dot(mixed, wo_ref[...], preferred_element_type=F32)
        ms = jnp.mean(h * h, axis=-1, keepdims=True)
        o_ref[rows, :] = h * lax.rsqrt(ms + NORM_EPS) * fw_ref[...]


def _merge_and_output(o_ret, o_diff, gates, x2, w_ret_up_bf, w_diff_up_bf, w_out_bf, final_w,
                      tm=256):
    nk, m, kw = o_ret.shape
    d = x2.shape[1]
    gs = d // LANES
    x_spec = pl.BlockSpec((nk, tm, kw), lambda i: (0, i, 0))
    w_spec = pl.BlockSpec((d, d), lambda i: (0, 0), pipeline_mode=pl.Buffered(1))
    g_spec = lambda name: pl.BlockSpec(
        (gs, tm, LANES), lambda i, base=_slab_base(name): (base // gs, i, 0))
    row_spec = pl.BlockSpec((tm, d), lambda i: (i, 0))
    return pl.pallas_call(
        _tail_kernel,
        out_shape=jax.ShapeDtypeStruct((m, d), F32),
        grid=(m // tm,),
        in_specs=[x_spec, x_spec, g_spec("gr"), g_spec("gd"), row_spec, w_spec, w_spec, w_spec,
                  pl.BlockSpec((1, d), lambda i: (0, 0))],
        out_specs=row_spec,
        compiler_params=_params(1),
        name="merge_and_output",
    )(o_ret, o_diff, gates, gates, x2, w_ret_up_bf, w_diff_up_bf, w_out_bf, final_w)


def _rotary_tables(s, inv_freq):
    pos = np.arange(s, dtype=np.float32)
    ang = pos[:, None] * inv_freq[None, :].astype(np.float32)
    emb = np.concatenate([ang, ang], axis=-1)
    half = emb.shape[-1] // 2
    sign = np.concatenate([-np.ones((half,), np.float32), np.ones((half,), np.float32)])
    return np.cos(emb), np.sin(emb) * sign[None, :]


def kernel(x, norm_w, w_in, w_ret_up, w_diff_up, w_out, lambda_q1, lambda_k1, lambda_q2,
           lambda_k2, subln_w, final_norm_w):
    b, s, d = x.shape
    assert d == D_MODEL and w_in.shape == (1, D_MODEL, IN_WIDTH)
    x2 = x.reshape(b * s, d)

    f32 = np.float32
    ret_inv_freq = np.exp(f32(-math.log(10000.0)) * np.linspace(0.0, 1.0, RET_QK_DIM // 2, dtype=f32))
    rope_inv_freq = f32(ROPE_THETA) ** (-np.arange(0, DIFF_HEAD_DIM, 2, dtype=f32) / f32(DIFF_HEAD_DIM))
    ret_cos, ret_sin = _rotary_tables(s, ret_inv_freq)
    rope_cos, rope_sin = _rotary_tables(s, rope_inv_freq)
    log_gamma = jnp.asarray(np.log1p(-np.exp2(f32(-5.0) - np.arange(RET_HEADS, dtype=f32))))

    ret_k_scale = f32(RET_QK_DIM ** -0.5)
    diff_q_scale = f32((DIFF_HEAD_DIM ** -0.5) * math.log2(math.e))
    cos_tables = jnp.asarray(
        np.stack([ret_cos, ret_cos * ret_k_scale, rope_cos * diff_q_scale, rope_cos]))
    sin_tables = jnp.asarray(
        np.stack([ret_sin, ret_sin * ret_k_scale, rope_sin * diff_q_scale, rope_sin]))

    proj_norm, xn = _norm_projection(x2, norm_w[0][None, :], w_in[0])
    proj_main = _input_projection(xn, w_in[0], cos_tables, sin_tables)
    by_pos = lambda a: a.reshape(a.shape[0], b, s, LANES)

    o_ret, o_diff, (w_ret_up_bf, w_diff_up_bf, w_out_bf) = _token_mixers(
        by_pos(proj_main), by_pos(proj_norm), log_gamma, lambda_q1, lambda_k1, lambda_q2,
        lambda_k2, subln_w, (w_ret_up[0], w_diff_up[0], w_out[0]))
    o_ret = o_ret.reshape(RET_HEADS, b * s, RET_V_DIM)
    o_diff = o_diff.reshape(DIFF_HEADS, b * s, DIFF_V_DIM)

    out = _merge_and_output(o_ret, o_diff, proj_main, x2, w_ret_up_bf, w_diff_up_bf, w_out_bf,
                            final_norm_w[None, :])
    return out.reshape(b, s, d)
```

```python
import functools
import math

import jax
import jax.numpy as jnp
import numpy as np
from jax import lax
from jax.experimental import pallas as pl
from jax.experimental.pallas import tpu as pltpu

F32 = jnp.float32
BF16 = jnp.bfloat16

D_MODEL = 2048
RET_QK_DIM = 128
RET_V_DIM = 256
RET_HEADS = D_MODEL // RET_V_DIM
DIFF_HEAD_DIM = 128
DIFF_V_DIM = 2 * DIFF_HEAD_DIM
DIFF_HEADS = D_MODEL // DIFF_V_DIM
ROPE_THETA = 10000.0
NORM_EPS = 1e-6
SUBLN_EPS = 1e-5
LAMBDA_INIT = 0.8 - 0.6 * math.exp(-0.3 * 0)

RET_QK_WIDTH = RET_HEADS * RET_QK_DIM
RET_WIDTH = RET_HEADS * RET_V_DIM
DIFF_QK_WIDTH = DIFF_HEADS * 2 * DIFF_HEAD_DIM
DIFF_WIDTH = DIFF_HEADS * DIFF_V_DIM
IN_WIDTH = 2 * RET_QK_WIDTH + 2 * RET_WIDTH + 2 * DIFF_QK_WIDTH + 2 * DIFF_WIDTH + 2 * D_MODEL

LANES = 128

_SECTIONS = (("rq", RET_QK_WIDTH, "rotary", 0), ("rk", RET_QK_WIDTH, "rotary", 1),
             ("rv", RET_WIDTH, "plain", 0), ("rz", RET_WIDTH, "silu", 0),
             ("dq", DIFF_QK_WIDTH, "rotary", 2), ("dk", DIFF_QK_WIDTH, "rotary", 3),
             ("dv", DIFF_WIDTH, "plain", 0), ("dz", DIFF_WIDTH, "silu", 0),
             ("gr", D_MODEL, "sigmoid", 0), ("gd", D_MODEL, "sigmoid", 0))
assert sum(sec[1] for sec in _SECTIONS) == IN_WIDTH
_NORM_GROUP = ("rv",)
_MAIN_GROUP = ("rq", "rk", "dq", "dk", "dv", "rz", "dz", "gr", "gd")
assert sorted(_NORM_GROUP + _MAIN_GROUP) == sorted(sec[0] for sec in _SECTIONS)


def _group_sections(group):
    return [next(sec for sec in _SECTIONS if sec[0] == name) for name in group]


def _slab_base(name):
    for group in (_NORM_GROUP, _MAIN_GROUP):
        base = 0
        for sec_name, width, _, _ in _group_sections(group):
            if sec_name == name:
                return base
            base += width // LANES
    raise KeyError(name)


VMEM_LIMIT = 56 * 1024 * 1024
NORM_PROJ_VMEM_LIMIT = 58 * 1024 * 1024

RET_CHUNK = 256
ATT_BLOCK = 256
ATT_SKEW = 1
PROJ_ROW_BANDS = ((1, 4), (1, 4), (1, 4), (1, 8), (1, 16), (1, 16))
NORM_PROJ_ROW_BANDS = ((1, 4), (1, 4), (1, 4), (1, 8), (1, 8))
TAIL_ROW_BANDS = ((1, 1),)
MIX_PATTERN = (1, 1)


def _params(n_axes, vmem=VMEM_LIMIT, flags=None):
    return pltpu.CompilerParams(dimension_semantics=("arbitrary",) * n_axes,
                                vmem_limit_bytes=vmem, flags=flags)


def _rotary(x, cos, sin_signed):
    return x * cos + pltpu.roll(x, x.shape[-1] // 2, axis=x.ndim - 1) * sin_signed


def _sigmoid(g):
    return 0.5 * jnp.tanh(0.5 * g) + 0.5


def _silu(z):
    h = 0.5 * z
    return h * jnp.tanh(h) + h


def _row_bands(tm, fractions):
    bounds = [0]
    for num, den in fractions:
        bounds.append(bounds[-1] + tm * num // den)
    assert bounds[-1] == tm
    return [slice(r0, r1) for r0, r1 in zip(bounds[:-1], bounds[1:])]


def _select(j, values):
    return sum(jnp.where(j == t, v, 0) for t, v in enumerate(values) if v)


def _projection_kernel(x_ref, w_ref, cos_ref, sin_ref, o_ref, *, tile_kinds):
    epilogues = {
        "rotary": lambda a, rows: _rotary(a, cos_ref[rows, :], sin_ref[rows, :]),
        "plain": lambda a, rows: a,
        "silu": lambda a, rows: _silu(a),
        "sigmoid": lambda a, rows: _sigmoid(a),
    }
    j = pl.program_id(0)
    for kind in dict.fromkeys(tile_kinds):
        tiles = [t for t, k in enumerate(tile_kinds) if k == kind]

        @pl.when(functools.reduce(jnp.logical_or, [j == t for t in tiles]))
        def _(fn=epilogues[kind]):
            w = w_ref[...].astype(BF16)
            for rows in _row_bands(x_ref.shape[0], PROJ_ROW_BANDS):
                acc = jnp.dot(x_ref[rows, :], w, preferred_element_type=F32)
                for s in range(o_ref.shape[0]):
                    o_ref[s, rows, :] = fn(acc[:, s * LANES:(s + 1) * LANES],
                                           rows).astype(o_ref.dtype)


def _section_tiles(group, tn):
    col_of = {}
    col = 0
    for name, width, _, _ in _SECTIONS:
        col_of[name] = col
        col += width
    col_tiles, tables, kinds = [], [], []
    for name, width, kind, table in _group_sections(group):
        assert width % tn == 0 and col_of[name] % tn == 0
        n = width // tn
        col_tiles += [col_of[name] // tn + t for t in range(n)]
        tables += [table] * n
        kinds += [kind] * n
    return col_tiles, tables, kinds


def _norm_projection_kernel(x_ref, nw_ref, w_ref, o_ref, xn_ref):
    w = w_ref[...].astype(BF16)
    for rows in _row_bands(x_ref.shape[0], NORM_PROJ_ROW_BANDS):
        x = x_ref[rows, :]
        ms = jnp.mean(x * x, axis=-1, keepdims=True)
        xn = (x * lax.rsqrt(ms + NORM_EPS) * nw_ref[...]).astype(BF16)
        xn_ref[rows, :] = xn
        acc = jnp.dot(xn, w, preferred_element_type=F32)
        for s in range(o_ref.shape[0]):
            o_ref[s, rows, :] = acc[:, s * LANES:(s + 1) * LANES].astype(o_ref.dtype)


def _norm_projection(x2, norm_w, w_in, tm=1024):
    m, d = x2.shape
    (_, tn, kind, _), = _group_sections(_NORM_GROUP)
    assert kind == "plain"
    (col_tile,), _, _ = _section_tiles(_NORM_GROUP, tn)
    return pl.pallas_call(
        _norm_projection_kernel,
        out_shape=[jax.ShapeDtypeStruct((tn // LANES, m, LANES), BF16),
                   jax.ShapeDtypeStruct((m, d), BF16)],
        grid=(m // tm,),
        in_specs=[pl.BlockSpec((tm, d), lambda i: (i, 0)),
                  pl.BlockSpec((1, d), lambda i: (0, 0)),
                  pl.BlockSpec((d, tn), lambda i: (0, col_tile), pipeline_mode=pl.Buffered(1))],
        out_specs=[pl.BlockSpec((tn // LANES, tm, LANES), lambda i: (0, i, 0)),
                   pl.BlockSpec((tm, d), lambda i: (i, 0))],
        compiler_params=_params(1, vmem=NORM_PROJ_VMEM_LIMIT),
        name="norm_projection",
    )(x2, norm_w, w_in)


def _input_projection(xn, w_in, cos_tables, sin_tables, tm=2048, tn=1024):
    m, d = xn.shape
    s_len = cos_tables.shape[1]
    col_tiles, tables, kinds = _section_tiles(_MAIN_GROUP, tn)
    n_rotary = kinds.count("rotary")
    assert kinds[:n_rotary] == ["rotary"] * n_rotary

    def table_map(j, i):
        rotary = j < n_rotary
        return (_select(jnp.minimum(j, n_rotary - 1), tables),
                jnp.where(rotary, i % (s_len // tm), s_len // tm - 1), 0)

    table_spec = pl.BlockSpec((None, tm, LANES), table_map)
    return pl.pallas_call(
        functools.partial(_projection_kernel, tile_kinds=kinds),
        out_shape=jax.ShapeDtypeStruct((len(col_tiles) * tn // LANES, m, LANES), BF16),
        grid=(len(col_tiles), m // tm),
        in_specs=[pl.BlockSpec((tm, d), lambda j, i: (i, 0)),
                  pl.BlockSpec((d, tn), lambda j, i: (0, _select(j, col_tiles))),
                  table_spec, table_spec],
        out_specs=pl.BlockSpec((tn // LANES, tm, LANES), lambda j, i: (j, i, 0)),
        compiler_params=_params(2),
        name="epilogue_projection",
    )(xn, w_in, cos_tables, sin_tables)


def _retention_pieces(lg, q_ref, k_ref, v_ref, z_ref, o_ref, qx_ref, kz_ref, vc_ref):
    s_len = q_ref.shape[0]
    c = RET_CHUNK
    n_chunks = s_len // c

    rowq = lax.broadcasted_iota(jnp.int32, (c, RET_QK_DIM), 0).astype(F32)
    zeta = jnp.exp(lg * (c - 1.0 - rowq))
    xi = jnp.exp(lg * (rowq + 1.0))
    for n in range(n_chunks):
        sl = slice(n * c, (n + 1) * c)
        qx_ref[sl, :] = (q_ref[sl, :].astype(F32) * xi).astype(BF16)
        kz_ref[sl, :] = (k_ref[sl, :].astype(F32) * zeta).astype(BF16)
    vc_ref[...] = jnp.concatenate([v_ref[0], v_ref[1]], axis=-1)

    row = lax.broadcasted_iota(jnp.int32, (c, c), 0).astype(F32)
    col = lax.broadcasted_iota(jnp.int32, (c, c), 1).astype(F32)
    rel = row - col
    decay_mask = jnp.where(rel >= 0.0, jnp.exp(lg * jnp.maximum(rel, 0.0)), 0.0)
    chunk_decay = jnp.exp(jnp.zeros((1, RET_V_DIM), F32) + lg * c)
    yield

    state = jnp.zeros((RET_QK_DIM, RET_V_DIM), F32)
    for n in range(n_chunks):
        sl = slice(n * c, (n + 1) * c)
        vc = vc_ref[sl, :]
        scores = lax.dot_general(q_ref[sl, :], k_ref[sl, :], (((1,), (1,)), ((), ())),
                                 preferred_element_type=F32) * decay_mask
        o = jnp.dot(scores.astype(BF16), vc, preferred_element_type=F32)
        o = o + jnp.dot(qx_ref[sl, :], state.astype(BF16), preferred_element_type=F32)
        if n + 1 < n_chunks:
            kv = lax.dot_general(kz_ref[sl, :], vc, (((0,), (0,)), ((), ())),
                                 preferred_element_type=F32)
            state = state * chunk_decay + kv

        ms = jnp.mean(o * o, axis=-1, keepdims=True)
        o = o * lax.rsqrt(ms + NORM_EPS)
        silu_z = jnp.concatenate([z_ref[0, sl, :], z_ref[1, sl, :]], axis=-1).astype(F32)
        o_ref[sl, :] = (o * silu_z).astype(o_ref.dtype)
        yield


def _diff_attn_pieces(q_ref, k_ref, v_ref, z_ref, lq1_ref, lk1_ref, lq2_ref, lk2_ref, subln_ref,
                      o_ref, vc_ref):
    t = ATT_BLOCK
    s_len = vc_ref.shape[0]
    vc_ref[...] = jnp.concatenate([v_ref[0], v_ref[1]], axis=-1)

    lam = (jnp.exp(jnp.sum(lq1_ref[...] * lk1_ref[...], axis=-1, keepdims=True))
           - jnp.exp(jnp.sum(lq2_ref[...] * lk2_ref[...], axis=-1, keepdims=True))
           + LAMBDA_INIT)
    subln_gain = subln_ref[...] * (1.0 - LAMBDA_INIT)
    causal = (lax.broadcasted_iota(jnp.int32, (t, t), 1)
              <= lax.broadcasted_iota(jnp.int32, (t, t), 0))
    nt_dims = (((1,), (1,)), ((), ()))

    units = [(n, c) for n in reversed(range(s_len // t)) for c in range(2)]

    def scores(u):
        n, c = units[u]
        r0 = n * t
        q = q_ref[c, r0:r0 + t, :]
        s_diag = jnp.where(causal,
                           lax.dot_general(q, k_ref[c, r0:r0 + t, :], nt_dims,
                                           preferred_element_type=F32), -jnp.inf)
        m = jnp.max(s_diag, axis=-1, keepdims=True)
        s_off = None
        if n > 0:
            s_off = lax.dot_general(q, k_ref[c, 0:r0, :], nt_dims, preferred_element_type=F32)
            m = jnp.maximum(m, jnp.max(s_off, axis=-1, keepdims=True))
        return s_diag, s_off, m

    def probs(sc):
        s_diag, s_off, m = sc
        p_diag = jnp.exp2(s_diag - m)
        l = jnp.sum(p_diag, axis=-1, keepdims=True)
        p_off = None
        if s_off is not None:
            p_off = jnp.exp2(s_off - m)
            l = l + jnp.sum(p_off, axis=-1, keepdims=True)
            p_off = p_off.astype(BF16)
        return p_diag.astype(BF16), p_off, l

    def weighted(u, pr):
        n, _ = units[u]
        r0 = n * t
        p_diag, p_off, l = pr
        acc = jnp.dot(p_diag, vc_ref[r0:r0 + t, :], preferred_element_type=F32)
        if p_off is not None:
            acc = acc + jnp.dot(p_off, vc_ref[0:r0, :], preferred_element_type=F32)
        return acc, l

    def finish(n, res):
        r0 = n * t
        o = res[0][0] * (1.0 / res[0][1]) - res[1][0] * (lam / res[1][1])
        ms = jnp.mean(o * o, axis=-1, keepdims=True)
        o = o * lax.rsqrt(ms + SUBLN_EPS) * subln_gain
        silu_z = jnp.concatenate([z_ref[0, r0:r0 + t, :], z_ref[1, r0:r0 + t, :]],
                                 axis=-1).astype(F32)
        o_ref[r0:r0 + t, :] = (o * silu_z).astype(o_ref.dtype)

    skew = ATT_SKEW
    sc, pr, res = {}, {}, {}
    for step in range(len(units) + 2 * skew):
        u_sc, u_pr, u_pv = step, step - skew, step - 2 * skew
        if 0 <= u_sc < len(units):
            sc[u_sc] = scores(u_sc)
        if 0 <= u_pr < len(units):
            pr[u_pr] = probs(sc.pop(u_pr))
        if 0 <= u_pv < len(units):
            res[u_pv] = weighted(u_pv, pr.pop(u_pv))
            if units[u_pv][1] == 1:
                finish(units[u_pv][0], (res.pop(u_pv - 1), res.pop(u_pv)))
        yield


def _mixers_kernel(lg_ref, rq_ref, rk_ref, rv_ref, rz_ref, dq_ref, dk_ref, dv_ref, dz_ref,
                   lq1_ref, lk1_ref, lq2_ref, lk2_ref, subln_ref, s1_ref, s2_ref, s3_ref,
                   oret_ref, odiff_ref, s1bf_ref, s2bf_ref, s3bf_ref,
                   qx_ref, kz_ref, rvc_ref, dvc_ref):
    for src, dst in ((s1_ref, s1bf_ref), (s2_ref, s2bf_ref), (s3_ref, s3bf_ref)):
        dst[...] = src[...].astype(BF16)

    retention = _retention_pieces(lg_ref[pl.program_id(1)], rq_ref, rk_ref, rv_ref, rz_ref,
                                  oret_ref, qx_ref, kz_ref, rvc_ref)
    attention = _diff_attn_pieces(dq_ref, dk_ref, dv_ref, dz_ref, lq1_ref, lk1_ref, lq2_ref,
                                  lk2_ref, subln_ref, odiff_ref, dvc_ref)
    live = [attention, retention]
    while live:
        for gen, count in zip((attention, retention), MIX_PATTERN):
            for _ in range(count):
                if gen in live and next(gen, "done") == "done":
                    live.remove(gen)


def _token_mixers(main4, norm4, log_gamma, lq1, lk1, lq2, lk2, subln_w, side_weights):
    _, b, s, _ = main4.shape
    by_section = lambda names: [norm4 if n in _NORM_GROUP else main4 for n in names]
    assert RET_HEADS == DIFF_HEADS
    one = lambda name: pl.BlockSpec(
        (None, None, s, LANES), lambda bi, h, base=_slab_base(name): (base + h, bi, 0, 0))
    pair = lambda name: pl.BlockSpec(
        (2, None, s, LANES), lambda bi, h, base=_slab_base(name): (base // 2 + h, bi, 0, 0))
    vec = lambda n: pl.BlockSpec((1, n), lambda bi, h: (0, 0))
    side_rows, side_cols = side_weights[0].shape
    assert all(w.shape == (side_rows, side_cols) for w in side_weights)
    band = side_rows // (b * DIFF_HEADS)
    side_spec = pl.BlockSpec((band, side_cols), lambda bi, h: (bi * DIFF_HEADS + h, 0))
    head_out = pl.BlockSpec((None, None, s, RET_V_DIM), lambda bi, h: (h, bi, 0, 0))
    outs = pl.pallas_call(
        _mixers_kernel,
        out_shape=[jax.ShapeDtypeStruct((RET_HEADS, b, s, RET_V_DIM), BF16),
                   jax.ShapeDtypeStruct((DIFF_HEADS, b, s, DIFF_V_DIM), BF16)]
                  + [jax.ShapeDtypeStruct((side_rows, side_cols), BF16)] * len(side_weights),
        grid=(b, DIFF_HEADS),
        in_specs=[pl.BlockSpec(memory_space=pltpu.SMEM),
                  one("rq"), one("rk"), pair("rv"), pair("rz"),
                  pair("dq"), pair("dk"), pair("dv"), pair("dz"),
                  vec(DIFF_HEAD_DIM), vec(DIFF_HEAD_DIM), vec(DIFF_HEAD_DIM), vec(DIFF_HEAD_DIM),
                  vec(DIFF_V_DIM)] + [side_spec] * len(side_weights),
        out_specs=[head_out, head_out] + [side_spec] * len(side_weights),
        scratch_shapes=[pltpu.VMEM((s, RET_QK_DIM), BF16)] * 2
                       + [pltpu.VMEM((s, RET_V_DIM), BF16), pltpu.VMEM((s, DIFF_V_DIM), BF16)],
        compiler_params=_params(2),
        name="token_mixers",
    )(log_gamma, *by_section(("rq", "rk", "rv", "rz", "dq", "dk", "dv", "dz")),
      lq1, lk1, lq2, lk2, subln_w, *side_weights)
    return outs[0], outs[1], outs[2:]


def _tail_kernel(a_ref, b_ref, g1_ref, g2_ref, x_ref, w1_ref, w2_ref, wo_ref, fw_ref, o_ref):
    kw = a_ref.shape[2]

    def up(x_slabs, w_ref, rows):
        acc = None
        for k in range(x_slabs.shape[0]):
            part = jnp.dot(x_slabs[k, rows, :], w_ref[k * kw:(k + 1) * kw, :],
                           preferred_element_type=F32)
            acc = part if acc is None else acc + part
        return acc

    for rows in _row_bands(a_ref.shape[1], TAIL_ROW_BANDS):
        up1 = up(a_ref, w1_ref, rows)
        up2 = up(b_ref, w2_ref, rows)
        mixed = jnp.concatenate(
            [(g1_ref[j, rows, :].astype(F32) * up1[:, j * LANES:(j + 1) * LANES]
              + g2_ref[j, rows, :].astype(F32) * up2[:, j * LANES:(j + 1) * LANES]).astype(BF16)
             for j in range(g1_ref.shape[0])], axis=-1)
        h = x_ref[rows, :] + jnp.dot(mixed, wo_ref[...], preferred_element_type=F32)
        ms = jnp.mean(h * h, axis=-1, keepdims=True)
        o_ref[rows, :] = h * lax.rsqrt(ms + NORM_EPS) * fw_ref[...]


def _merge_and_output(o_ret, o_diff, gates, x2, w_ret_up_bf, w_diff_up_bf, w_out_bf, final_w,
                      tm=256):
    nk, m, kw = o_ret.shape
    d = x2.shape[1]
    gs = d // LANES
    x_spec = pl.BlockSpec((nk, tm, kw), lambda i: (0, i, 0))
    w_spec = pl.BlockSpec((d, d), lambda i: (0, 0), pipeline_mode=pl.Buffered(1))
    g_spec = lambda name: pl.BlockSpec(
        (gs, tm, LANES), lambda i, base=_slab_base(name): (base // gs, i, 0))
    row_spec = pl.BlockSpec((tm, d), lambda i: (i, 0))
    return pl.pallas_call(
        _tail_kernel,
        out_shape=jax.ShapeDtypeStruct((m, d), F32),
        grid=(m // tm,),
        in_specs=[x_spec, x_spec, g_spec("gr"), g_spec("gd"), row_spec, w_spec, w_spec, w_spec,
                  pl.BlockSpec((1, d), lambda i: (0, 0))],
        out_specs=row_spec,
        compiler_params=_params(1),
        name="merge_and_output",
    )(o_ret, o_diff, gates, gates, x2, w_ret_up_bf, w_diff_up_bf, w_out_bf, final_w)


def _rotary_tables(s, inv_freq):
    pos = np.arange(s, dtype=np.float32)
    ang = pos[:, None] * inv_freq[None, :].astype(np.float32)
    emb = np.concatenate([ang, ang], axis=-1)
    half = emb.shape[-1] // 2
    sign = np.concatenate([-np.ones((half,), np.float32), np.ones((half,), np.float32)])
    return np.cos(emb), np.sin(emb) * sign[None, :]


def kernel(x, norm_w, w_in, w_ret_up, w_diff_up, w_out, lambda_q1, lambda_k1, lambda_q2,
           lambda_k2, subln_w, final_norm_w):
    b, s, d = x.shape
    assert d == D_MODEL and w_in.shape == (1, D_MODEL, IN_WIDTH)
    x2 = x.reshape(b * s, d)

    f32 = np.float32
    ret_inv_freq = np.exp(f32(-math.log(10000.0)) * np.linspace(0.0, 1.0, RET_QK_DIM // 2, dtype=f32))
    rope_inv_freq = f32(ROPE_THETA) ** (-np.arange(0, DIFF_HEAD_DIM, 2, dtype=f32) / f32(DIFF_HEAD_DIM))
    ret_cos, ret_sin = _rotary_tables(s, ret_inv_freq)
    rope_cos, rope_sin = _rotary_tables(s, rope_inv_freq)
    log_gamma = jnp.asarray(np.log1p(-np.exp2(f32(-5.0) - np.arange(RET_HEADS, dtype=f32))))

    ret_k_scale = f32(RET_QK_DIM ** -0.5)
    diff_q_scale = f32((DIFF_HEAD_DIM ** -0.5) * math.log2(math.e))
    cos_tables = jnp.asarray(
        np.stack([ret_cos, ret_cos * ret_k_scale, rope_cos * diff_q_scale, rope_cos]))
    sin_tables = jnp.asarray(
        np.stack([ret_sin, ret_sin * ret_k_scale, rope_sin * diff_q_scale, rope_sin]))

    proj_norm, xn = _norm_projection(x2, norm_w[0][None, :], w_in[0])
    proj_main = _input_projection(xn, w_in[0], cos_tables, sin_tables)
    by_pos = lambda a: a.reshape(a.shape[0], b, s, LANES)

    o_ret, o_diff, (w_ret_up_bf, w_diff_up_bf, w_out_bf) = _token_mixers(
        by_pos(proj_main), by_pos(proj_norm), log_gamma, lambda_q1, lambda_k1, lambda_q2,
        lambda_k2, subln_w, (w_ret_up[0], w_diff_up[0], w_out[0]))
    o_ret = o_ret.reshape(RET_HEADS, b * s, RET_V_DIM)
    o_diff = o_diff.reshape(DIFF_HEADS, b * s, DIFF_V_DIM)

    out = _merge_and_output(o_ret, o_diff, proj_main, x2, w_ret_up_bf, w_diff_up_bf, w_out_bf,
                            final_norm_w[None, :])
    return out.reshape(b, s, d)
```

```python
import functools
import math

import jax
import jax.numpy as jnp
import numpy as np
from jax import lax
from jax.experimental import pallas as pl
from jax.experimental.pallas import tpu as pltpu

F32 = jnp.float32
BF16 = jnp.bfloat16

D_MODEL = 2048
RET_QK_DIM = 128
RET_V_DIM = 256
RET_HEADS = D_MODEL // RET_V_DIM
DIFF_HEAD_DIM = 128
DIFF_V_DIM = 2 * DIFF_HEAD_DIM
DIFF_HEADS = D_MODEL // DIFF_V_DIM
ROPE_THETA = 10000.0
NORM_EPS = 1e-6
SUBLN_EPS = 1e-5
LAMBDA_INIT = 0.8 - 0.6 * math.exp(-0.3 * 0)

RET_QK_WIDTH = RET_HEADS * RET_QK_DIM
RET_WIDTH = RET_HEADS * RET_V_DIM
DIFF_QK_WIDTH = DIFF_HEADS * 2 * DIFF_HEAD_DIM
DIFF_WIDTH = DIFF_HEADS * DIFF_V_DIM
IN_WIDTH = 2 * RET_QK_WIDTH + 2 * RET_WIDTH + 2 * DIFF_QK_WIDTH + 2 * DIFF_WIDTH + 2 * D_MODEL

LANES = 128

_SECTIONS = (("rq", RET_QK_WIDTH, "rotary", 0), ("rk", RET_QK_WIDTH, "rotary", 1),
             ("rv", RET_WIDTH, "plain", 0), ("rz", RET_WIDTH, "silu", 0),
             ("dq", DIFF_QK_WIDTH, "rotary", 2), ("dk", DIFF_QK_WIDTH, "rotary", 3),
             ("dv", DIFF_WIDTH, "plain", 0), ("dz", DIFF_WIDTH, "silu", 0),
             ("gr", D_MODEL, "sigmoid", 0), ("gd", D_MODEL, "sigmoid", 0))
assert sum(sec[1] for sec in _SECTIONS) == IN_WIDTH
_NORM_GROUP = ("rv",)
_MAIN_GROUP = ("rq", "rk", "dq", "dk", "dv", "rz", "dz", "gr", "gd")
assert sorted(_NORM_GROUP + _MAIN_GROUP) == sorted(sec[0] for sec in _SECTIONS)


def _group_sections(group):
    return [next(sec for sec in _SECTIONS if sec[0] == name) for name in group]


def _slab_base(name):
    for group in (_NORM_GROUP, _MAIN_GROUP):
        base = 0
        for sec_name, width, _, _ in _group_sections(group):
            if sec_name == name:
                return base
            base += width // LANES
    raise KeyError(name)


VMEM_LIMIT = 56 * 1024 * 1024
NORM_PROJ_VMEM_LIMIT = 58 * 1024 * 1024

RET_CHUNK = 256
ATT_BLOCK = 256
ATT_SKEW = 1
PROJ_ROW_BANDS = ((1, 4), (1, 4), (1, 4), (1, 8), (1, 16), (1, 16))
NORM_PROJ_ROW_BANDS = ((1, 4), (1, 4), (1, 4), (1, 8), (1, 8))
MIX_PATTERN = (1, 1)


def _params(n_axes, vmem=VMEM_LIMIT):
    return pltpu.CompilerParams(dimension_semantics=("arbitrary",) * n_axes,
                                vmem_limit_bytes=vmem)


def _rotary(x, cos, sin_signed):
    return x * cos + pltpu.roll(x, x.shape[-1] // 2, axis=x.ndim - 1) * sin_signed


def _sigmoid(g):
    return 0.5 * jnp.tanh(0.5 * g) + 0.5


def _silu(z):
    h = 0.5 * z
    return h * jnp.tanh(h) + h


def _row_bands(tm, fractions):
    bounds = [0]
    for num, den in fractions:
        bounds.append(bounds[-1] + tm * num // den)
    assert bounds[-1] == tm
    return [slice(r0, r1) for r0, r1 in zip(bounds[:-1], bounds[1:])]


def _select(j, values):
    return sum(jnp.where(j == t, v, 0) for t, v in enumerate(values) if v)


def _projection_kernel(x_ref, w_ref, cos_ref, sin_ref, o_ref, *, tile_kinds):
    epilogues = {
        "rotary": lambda a, rows: _rotary(a, cos_ref[rows, :], sin_ref[rows, :]),
        "plain": lambda a, rows: a,
        "silu": lambda a, rows: _silu(a),
        "sigmoid": lambda a, rows: _sigmoid(a),
    }
    j = pl.program_id(0)
    for kind in dict.fromkeys(tile_kinds):
        tiles = [t for t, k in enumerate(tile_kinds) if k == kind]

        @pl.when(functools.reduce(jnp.logical_or, [j == t for t in tiles]))
        def _(fn=epilogues[kind]):
            w = w_ref[...].astype(BF16)
            for rows in _row_bands(x_ref.shape[0], PROJ_ROW_BANDS):
                acc = jnp.dot(x_ref[rows, :], w, preferred_element_type=F32)
                for s in range(o_ref.shape[0]):
                    o_ref[s, rows, :] = fn(acc[:, s * LANES:(s + 1) * LANES],
                                           rows).astype(o_ref.dtype)


def _section_tiles(group, tn):
    col_of = {}
    col = 0
    for name, width, _, _ in _SECTIONS:
        col_of[name] = col
        col += width
    col_tiles, tables, kinds = [], [], []
    for name, width, kind, table in _group_sections(group):
        assert width % tn == 0 and col_of[name] % tn == 0
        n = width // tn
        col_tiles += [col_of[name] // tn + t for t in range(n)]
        tables += [table] * n
        kinds += [kind] * n
    return col_tiles, tables, kinds


def _norm_projection_kernel(x_ref, nw_ref, w_ref, o_ref, xn_ref):
    w = w_ref[...].astype(BF16)
    for rows in _row_bands(x_ref.shape[0], NORM_PROJ_ROW_BANDS):
        x = x_ref[rows, :]
        ms = jnp.mean(x * x, axis=-1, keepdims=True)
        xn = (x * lax.rsqrt(ms + NORM_EPS) * nw_ref[...]).astype(BF16)
        xn_ref[rows, :] = xn
        acc = jnp.dot(xn, w, preferred_element_type=F32)
        for s in range(o_ref.shape[0]):
            o_ref[s, rows, :] = acc[:, s * LANES:(s + 1) * LANES].astype(o_ref.dtype)


def _norm_projection(x2, norm_w, w_in, tm=1024):
    m, d = x2.shape
    (_, tn, kind, _), = _group_sections(_NORM_GROUP)
    assert kind == "plain"
    (col_tile,), _, _ = _section_tiles(_NORM_GROUP, tn)
    return pl.pallas_call(
        _norm_projection_kernel,
        out_shape=[jax.ShapeDtypeStruct((tn // LANES, m, LANES), BF16),
                   jax.ShapeDtypeStruct((m, d), BF16)],
        grid=(m // tm,),
        in_specs=[pl.BlockSpec((tm, d), lambda i: (i, 0)),
                  pl.BlockSpec((1, d), lambda i: (0, 0)),
                  pl.BlockSpec((d, tn), lambda i: (0, col_tile), pipeline_mode=pl.Buffered(1))],
        out_specs=[pl.BlockSpec((tn // LANES, tm, LANES), lambda i: (0, i, 0)),
                   pl.BlockSpec((tm, d), lambda i: (i, 0))],
        compiler_params=_params(1, vmem=NORM_PROJ_VMEM_LIMIT),
        name="norm_projection",
    )(x2, norm_w, w_in)


def _input_projection(xn, w_in, cos_tables, sin_tables, tm=2048, tn=1024):
    m, d = xn.shape
    s_len = cos_tables.shape[1]
    col_tiles, tables, kinds = _section_tiles(_MAIN_GROUP, tn)
    n_rotary = kinds.count("rotary")
    assert kinds[:n_rotary] == ["rotary"] * n_rotary

    def table_map(j, i):
        rotary = j < n_rotary
        return (_select(jnp.minimum(j, n_rotary - 1), tables),
                jnp.where(rotary, i % (s_len // tm), s_len // tm - 1), 0)

    table_spec = pl.BlockSpec((None, tm, LANES), table_map)
    return pl.pallas_call(
        functools.partial(_projection_kernel, tile_kinds=kinds),
        out_shape=jax.ShapeDtypeStruct((len(col_tiles) * tn // LANES, m, LANES), BF16),
        grid=(len(col_tiles), m // tm),
        in_specs=[pl.BlockSpec((tm, d), lambda j, i: (i, 0)),
                  pl.BlockSpec((d, tn), lambda j, i: (0, _select(j, col_tiles))),
                  table_spec, table_spec],
        out_specs=pl.BlockSpec((tn // LANES, tm, LANES), lambda j, i: (j, i, 0)),
        compiler_params=_params(2),
        name="epilogue_projection",
    )(xn, w_in, cos_tables, sin_tables)


def _retention_pieces(lg, q_ref, k_ref, v_ref, z_ref, o_ref, qx_ref, kz_ref, vc_ref):
    s_len = q_ref.shape[0]
    c = RET_CHUNK
    n_chunks = s_len // c

    rowq = lax.broadcasted_iota(jnp.int32, (c, RET_QK_DIM), 0).astype(F32)
    zeta = jnp.exp(lg * (c - 1.0 - rowq))
    xi = jnp.exp(lg * (rowq + 1.0))
    for n in range(n_chunks):
        sl = slice(n * c, (n + 1) * c)
        qx_ref[sl, :] = (q_ref[sl, :].astype(F32) * xi).astype(BF16)
        kz_ref[sl, :] = (k_ref[sl, :].astype(F32) * zeta).astype(BF16)
    vc_ref[...] = jnp.concatenate([v_ref[0], v_ref[1]], axis=-1)

    row = lax.broadcasted_iota(jnp.int32, (c, c), 0).astype(F32)
    col = lax.broadcasted_iota(jnp.int32, (c, c), 1).astype(F32)
    rel = row - col
    decay_mask = jnp.where(rel >= 0.0, jnp.exp(lg * jnp.maximum(rel, 0.0)), 0.0)
    chunk_decay = jnp.exp(jnp.zeros((1, RET_V_DIM), F32) + lg * c)
    yield

    state = jnp.zeros((RET_QK_DIM, RET_V_DIM), F32)
    for n in range(n_chunks):
        sl = slice(n * c, (n + 1) * c)
        vc = vc_ref[sl, :]
        scores = lax.dot_general(q_ref[sl, :], k_ref[sl, :], (((1,), (1,)), ((), ())),
                                 preferred_element_type=F32) * decay_mask
        o = jnp.dot(scores.astype(BF16), vc, preferred_element_type=F32)
        o = o + jnp.dot(qx_ref[sl, :], state.astype(BF16), preferred_element_type=F32)
        if n + 1 < n_chunks:
            kv = lax.dot_general(kz_ref[sl, :], vc, (((0,), (0,)), ((), ())),
                                 preferred_element_type=F32)
            state = state * chunk_decay + kv

        ms = jnp.mean(o * o, axis=-1, keepdims=True)
        o = o * lax.rsqrt(ms + NORM_EPS)
        silu_z = jnp.concatenate([z_ref[0, sl, :], z_ref[1, sl, :]], axis=-1).astype(F32)
        o_ref[sl, :] = (o * silu_z).astype(o_ref.dtype)
        yield


def _diff_attn_pieces(q_ref, k_ref, v_ref, z_ref, lq1_ref, lk1_ref, lq2_ref, lk2_ref, subln_ref,
                      o_ref, vc_ref):
    t = ATT_BLOCK
    s_len = vc_ref.shape[0]
    vc_ref[...] = jnp.concatenate([v_ref[0], v_ref[1]], axis=-1)

    lam = (jnp.exp(jnp.sum(lq1_ref[...] * lk1_ref[...], axis=-1, keepdims=True))
           - jnp.exp(jnp.sum(lq2_ref[...] * lk2_ref[...], axis=-1, keepdims=True))
           + LAMBDA_INIT)
    subln_gain = subln_ref[...] * (1.0 - LAMBDA_INIT)
    causal = (lax.broadcasted_iota(jnp.int32, (t, t), 1)
              <= lax.broadcasted_iota(jnp.int32, (t, t), 0))
    nt_dims = (((1,), (1,)), ((), ()))

    units = [(n, c) for n in reversed(range(s_len // t)) for c in range(2)]

    def scores(u):
        n, c = units[u]
        r0 = n * t
        q = q_ref[c, r0:r0 + t, :]
        s_diag = jnp.where(causal,
                           lax.dot_general(q, k_ref[c, r0:r0 + t, :], nt_dims,
                                           preferred_element_type=F32), -jnp.inf)
        m = jnp.max(s_diag, axis=-1, keepdims=True)
        s_off = None
        if n > 0:
            s_off = lax.dot_general(q, k_ref[c, 0:r0, :], nt_dims, preferred_element_type=F32)
            m = jnp.maximum(m, jnp.max(s_off, axis=-1, keepdims=True))
        return s_diag, s_off, m

    def probs(sc):
        s_diag, s_off, m = sc
        p_diag = jnp.exp2(s_diag - m)
        l = jnp.sum(p_diag, axis=-1, keepdims=True)
        p_off = None
        if s_off is not None:
            p_off = jnp.exp2(s_off - m)
            l = l + jnp.sum(p_off, axis=-1, keepdims=True)
            p_off = p_off.astype(BF16)
        return p_diag.astype(BF16), p_off, l

    def weighted(u, pr):
        n, _ = units[u]
        r0 = n * t
        p_diag, p_off, l = pr
        acc = jnp.dot(p_diag, vc_ref[r0:r0 + t, :], preferred_element_type=F32)
        if p_off is not None:
            acc = acc + jnp.dot(p_off, vc_ref[0:r0, :], preferred_element_type=F32)
        return acc, l

    def finish(n, res):
        r0 = n * t
        o = res[0][0] * (1.0 / res[0][1]) - res[1][0] * (lam / res[1][1])
        ms = jnp.mean(o * o, axis=-1, keepdims=True)
        o = o * lax.rsqrt(ms + SUBLN_EPS) * subln_gain
        silu_z = jnp.concatenate([z_ref[0, r0:r0 + t, :], z_ref[1, r0:r0 + t, :]],
                                 axis=-1).astype(F32)
        o_ref[r0:r0 + t, :] = (o * silu_z).astype(o_ref.dtype)

    skew = ATT_SKEW
    sc, pr, res = {}, {}, {}
    for step in range(len(units) + 2 * skew):
        u_sc, u_pr, u_pv = step, step - skew, step - 2 * skew
        if 0 <= u_sc < len(units):
            sc[u_sc] = scores(u_sc)
        if 0 <= u_pr < len(units):
            pr[u_pr] = probs(sc.pop(u_pr))
        if 0 <= u_pv < len(units):
            res[u_pv] = weighted(u_pv, pr.pop(u_pv))
            if units[u_pv][1] == 1:
                finish(units[u_pv][0], (res.pop(u_pv - 1), res.pop(u_pv)))
        yield


def _mixers_kernel(lg_ref, rq_ref, rk_ref, rv_ref, rz_ref, dq_ref, dk_ref, dv_ref, dz_ref,
                   lq1_ref, lk1_ref, lq2_ref, lk2_ref, subln_ref, s1_ref, s2_ref, s3_ref,
                   oret_ref, odiff_ref, s1bf_ref, s2bf_ref, s3bf_ref,
                   qx_ref, kz_ref, rvc_ref, dvc_ref):
    for src, dst in ((s1_ref, s1bf_ref), (s2_ref, s2bf_ref), (s3_ref, s3bf_ref)):
        dst[...] = src[...].astype(BF16)

    retention = _retention_pieces(lg_ref[pl.program_id(1)], rq_ref, rk_ref, rv_ref, rz_ref,
                                  oret_ref, qx_ref, kz_ref, rvc_ref)
    attention = _diff_attn_pieces(dq_ref, dk_ref, dv_ref, dz_ref, lq1_ref, lk1_ref, lq2_ref,
                                  lk2_ref, subln_ref, odiff_ref, dvc_ref)
    live = [attention, retention]
    while live:
        for gen, count in zip((attention, retention), MIX_PATTERN):
            for _ in range(count):
                if gen in live and next(gen, "done") == "done":
                    live.remove(gen)


def _token_mixers(main4, norm4, log_gamma, lq1, lk1, lq2, lk2, subln_w, side_weights):
    _, b, s, _ = main4.shape
    by_section = lambda names: [norm4 if n in _NORM_GROUP else main4 for n in names]
    assert RET_HEADS == DIFF_HEADS
    one = lambda name: pl.BlockSpec(
        (None, None, s, LANES), lambda bi, h, base=_slab_base(name): (base + h, bi, 0, 0))
    pair = lambda name: pl.BlockSpec(
        (2, None, s, LANES), lambda bi, h, base=_slab_base(name): (base // 2 + h, bi, 0, 0))
    vec = lambda n: pl.BlockSpec((1, n), lambda bi, h: (0, 0))
    side_rows, side_cols = side_weights[0].shape
    assert all(w.shape == (side_rows, side_cols) for w in side_weights)
    band = side_rows // (b * DIFF_HEADS)
    side_spec = pl.BlockSpec((band, side_cols), lambda bi, h: (bi * DIFF_HEADS + h, 0))
    head_out = pl.BlockSpec((None, None, s, RET_V_DIM), lambda bi, h: (h, bi, 0, 0))
    outs = pl.pallas_call(
        _mixers_kernel,
        out_shape=[jax.ShapeDtypeStruct((RET_HEADS, b, s, RET_V_DIM), BF16),
                   jax.ShapeDtypeStruct((DIFF_HEADS, b, s, DIFF_V_DIM), BF16)]
                  + [jax.ShapeDtypeStruct((side_rows, side_cols), BF16)] * len(side_weights),
        grid=(b, DIFF_HEADS),
        in_specs=[pl.BlockSpec(memory_space=pltpu.SMEM),
                  one("rq"), one("rk"), pair("rv"), pair("rz"),
                  pair("dq"), pair("dk"), pair("dv"), pair("dz"),
                  vec(DIFF_HEAD_DIM), vec(DIFF_HEAD_DIM), vec(DIFF_HEAD_DIM), vec(DIFF_HEAD_DIM),
                  vec(DIFF_V_DIM)] + [side_spec] * len(side_weights),
        out_specs=[head_out, head_out] + [side_spec] * len(side_weights),
        scratch_shapes=[pltpu.VMEM((s, RET_QK_DIM), BF16)] * 2
                       + [pltpu.VMEM((s, RET_V_DIM), BF16), pltpu.VMEM((s, DIFF_V_DIM), BF16)],
        compiler_params=_params(2),
        name="token_mixers",
    )(log_gamma, *by_section(("rq", "rk", "rv", "rz", "dq", "dk", "dv", "dz")),
      lq1, lk1, lq2, lk2, subln_w, *side_weights)
    return outs[0], outs[1], outs[2:]


def _tail_kernel(a_ref, b_ref, g1_ref, g2_ref, x_ref, w1_ref, w2_ref, wo_ref, fw_ref, o_ref):
    kw = a_ref.shape[2]

    def up(x_slabs, w_ref):
        acc = None
        for k in range(x_slabs.shape[0]):
            part = jnp.dot(x_slabs[k], w_ref[k * kw:(k + 1) * kw, :], preferred_element_type=F32)
            acc = part if acc is None else acc + part
        return acc

    up1 = up(a_ref, w1_ref)
    up2 = up(b_ref, w2_ref)
    mixed = jnp.concatenate(
        [(g1_ref[j].astype(F32) * up1[:, j * LANES:(j + 1) * LANES]
          + g2_ref[j].astype(F32) * up2[:, j * LANES:(j + 1) * LANES]).astype(BF16)
         for j in range(g1_ref.shape[0])], axis=-1)
    h = x_ref[...] + jnp.dot(mixed, wo_ref[...], preferred_element_type=F32)
    ms = jnp.mean(h * h, axis=-1, keepdims=True)
    o_ref[...] = h * lax.rsqrt(ms + NORM_EPS) * fw_ref[...]


def _merge_and_output(o_ret, o_diff, gates, x2, w_ret_up_bf, w_diff_up_bf, w_out_bf, final_w,
                      tm=256):
    nk, m, kw = o_ret.shape
    d = x2.shape[1]
    gs = d // LANES
    x_spec = pl.BlockSpec((nk, tm, kw), lambda i: (0, i, 0))
    w_spec = pl.BlockSpec((d, d), lambda i: (0, 0), pipeline_mode=pl.Buffered(1))
    g_spec = lambda name: pl.BlockSpec(
        (gs, tm, LANES), lambda i, base=_slab_base(name): (base // gs, i, 0))
    row_spec = pl.BlockSpec((tm, d), lambda i: (i, 0))
    return pl.pallas_call(
        _tail_kernel,
        out_shape=jax.ShapeDtypeStruct((m, d), F32),
        grid=(m // tm,),
        in_specs=[x_spec, x_spec, g_spec("gr"), g_spec("gd"), row_spec, w_spec, w_spec, w_spec,
                  pl.BlockSpec((1, d), lambda i: (0, 0))],
        out_specs=row_spec,
        compiler_params=_params(1),
        name="merge_and_output",
    )(o_ret, o_diff, gates, gates, x2, w_ret_up_bf, w_diff_up_bf, w_out_bf, final_w)


def _rotary_tables(s, inv_freq):
    pos = np.arange(s, dtype=np.float32)
    ang = pos[:, None] * inv_freq[None, :].astype(np.float32)
    emb = np.concatenate([ang, ang], axis=-1)
    half = emb.shape[-1] // 2
    sign = np.concatenate([-np.ones((half,), np.float32), np.ones((half,), np.float32)])
    return np.cos(emb), np.sin(emb) * sign[None, :]


def kernel(x, norm_w, w_in, w_ret_up, w_diff_up, w_out, lambda_q1, lambda_k1, lambda_q2,
           lambda_k2, subln_w, final_norm_w):
    b, s, d = x.shape
    assert d == D_MODEL and w_in.shape == (1, D_MODEL, IN_WIDTH)
    x2 = x.reshape(b * s, d)

    f32 = np.float32
    ret_inv_freq = np.exp(f32(-math.log(10000.0)) * np.linspace(0.0, 1.0, RET_QK_DIM // 2, dtype=f32))
    rope_inv_freq = f32(ROPE_THETA) ** (-np.arange(0, DIFF_HEAD_DIM, 2, dtype=f32) / f32(DIFF_HEAD_DIM))
    ret_cos, ret_sin = _rotary_tables(s, ret_inv_freq)
    rope_cos, rope_sin = _rotary_tables(s, rope_inv_freq)
    log_gamma = jnp.asarray(np.log1p(-np.exp2(f32(-5.0) - np.arange(RET_HEADS, dtype=f32))))

    ret_k_scale = f32(RET_QK_DIM ** -0.5)
    diff_q_scale = f32((DIFF_HEAD_DIM ** -0.5) * math.log2(math.e))
    cos_tables = jnp.asarray(
        np.stack([ret_cos, ret_cos * ret_k_scale, rope_cos * diff_q_scale, rope_cos]))
    sin_tables = jnp.asarray(
        np.stack([ret_sin, ret_sin * ret_k_scale, rope_sin * diff_q_scale, rope_sin]))

    proj_norm, xn = _norm_projection(x2, norm_w[0][None, :], w_in[0])
    proj_main = _input_projection(xn, w_in[0], cos_tables, sin_tables)
    by_pos = lambda a: a.reshape(a.shape[0], b, s, LANES)

    o_ret, o_diff, (w_ret_up_bf, w_diff_up_bf, w_out_bf) = _token_mixers(
        by_pos(proj_main), by_pos(proj_norm), log_gamma, lambda_q1, lambda_k1, lambda_q2,
        lambda_k2, subln_w, (w_ret_up[0], w_diff_up[0], w_out[0]))
    o_ret = o_ret.reshape(RET_HEADS, b * s, RET_V_DIM)
    o_diff = o_diff.reshape(DIFF_HEADS, b * s, DIFF_V_DIM)

    out = _merge_and_output(o_ret, o_diff, proj_main, x2, w_ret_up_bf, w_diff_up_bf, w_out_bf,
                            final_norm_w[None, :])
    return out.reshape(b, s, d)
```

```python
import functools
import math

import jax
import jax.numpy as jnp
import numpy as np
from jax import lax
from jax.experimental import pallas as pl
from jax.experimental.pallas import tpu as pltpu

F32 = jnp.float32
BF16 = jnp.bfloat16

D_MODEL = 2048
RET_QK_DIM = 128
RET_V_DIM = 256
RET_HEADS = D_MODEL // RET_V_DIM
DIFF_HEAD_DIM = 128
DIFF_V_DIM = 2 * DIFF_HEAD_DIM
DIFF_HEADS = D_MODEL // DIFF_V_DIM
ROPE_THETA = 10000.0
NORM_EPS = 1e-6
SUBLN_EPS = 1e-5
LAMBDA_INIT = 0.8 - 0.6 * math.exp(-0.3 * 0)

RET_QK_WIDTH = RET_HEADS * RET_QK_DIM
RET_WIDTH = RET_HEADS * RET_V_DIM
DIFF_QK_WIDTH = DIFF_HEADS * 2 * DIFF_HEAD_DIM
DIFF_WIDTH = DIFF_HEADS * DIFF_V_DIM
IN_WIDTH = 2 * RET_QK_WIDTH + 2 * RET_WIDTH + 2 * DIFF_QK_WIDTH + 2 * DIFF_WIDTH + 2 * D_MODEL

LANES = 128

_SECTIONS = (("rq", RET_QK_WIDTH, "rotary", 0), ("rk", RET_QK_WIDTH, "rotary", 1),
             ("rv", RET_WIDTH, "plain", 0), ("rz", RET_WIDTH, "silu", 0),
             ("dq", DIFF_QK_WIDTH, "rotary", 2), ("dk", DIFF_QK_WIDTH, "rotary", 3),
             ("dv", DIFF_WIDTH, "plain", 0), ("dz", DIFF_WIDTH, "silu", 0),
             ("gr", D_MODEL, "sigmoid", 0), ("gd", D_MODEL, "sigmoid", 0))
assert sum(sec[1] for sec in _SECTIONS) == IN_WIDTH
_NORM_GROUP = ("rv",)
_MAIN_GROUP = ("rq", "rk", "dq", "dk", "dv", "rz", "dz", "gr", "gd")
assert sorted(_NORM_GROUP + _MAIN_GROUP) == sorted(sec[0] for sec in _SECTIONS)


def _group_sections(group):
    return [next(sec for sec in _SECTIONS if sec[0] == name) for name in group]


def _slab_base(name):
    for group in (_NORM_GROUP, _MAIN_GROUP):
        base = 0
        for sec_name, width, _, _ in _group_sections(group):
            if sec_name == name:
                return base
            base += width // LANES
    raise KeyError(name)


VMEM_LIMIT = 56 * 1024 * 1024
NORM_PROJ_VMEM_LIMIT = 58 * 1024 * 1024

RET_CHUNK = 256
ATT_BLOCK = 256
ATT_SKEW = 1
PROJ_ROW_BANDS = ((1, 4), (1, 4), (1, 4), (1, 8), (1, 16), (1, 16))
NORM_PROJ_ROW_BANDS = ((1, 4), (1, 4), (1, 4), (1, 8), (1, 8))
MIX_PATTERN = (1, 1)


def _params(n_axes, vmem=VMEM_LIMIT):
    return pltpu.CompilerParams(dimension_semantics=("arbitrary",) * n_axes,
                                vmem_limit_bytes=vmem)


def _rotary(x, cos, sin_signed):
    return x * cos + pltpu.roll(x, x.shape[-1] // 2, axis=x.ndim - 1) * sin_signed


def _sigmoid(g):
    return 0.5 * jnp.tanh(0.5 * g) + 0.5


def _silu(z):
    h = 0.5 * z
    return h * jnp.tanh(h) + h


def _row_bands(tm, fractions):
    bounds = [0]
    for num, den in fractions:
        bounds.append(bounds[-1] + tm * num // den)
    assert bounds[-1] == tm
    return [slice(r0, r1) for r0, r1 in zip(bounds[:-1], bounds[1:])]


def _select(j, values):
    return sum(jnp.where(j == t, v, 0) for t, v in enumerate(values) if v)


def _projection_kernel(x_ref, w_ref, cos_ref, sin_ref, o_ref, *, tile_kinds):
    epilogues = {
        "rotary": lambda a, rows: _rotary(a, cos_ref[rows, :], sin_ref[rows, :]),
        "plain": lambda a, rows: a,
        "silu": lambda a, rows: _silu(a),
        "sigmoid": lambda a, rows: _sigmoid(a),
    }
    j = pl.program_id(0)
    for kind in dict.fromkeys(tile_kinds):
        tiles = [t for t, k in enumerate(tile_kinds) if k == kind]

        @pl.when(functools.reduce(jnp.logical_or, [j == t for t in tiles]))
        def _(fn=epilogues[kind]):
            w = w_ref[...].astype(BF16)
            for rows in _row_bands(x_ref.shape[0], PROJ_ROW_BANDS):
                acc = jnp.dot(x_ref[rows, :], w, preferred_element_type=F32)
                for s in range(o_ref.shape[0]):
                    o_ref[s, rows, :] = fn(acc[:, s * LANES:(s + 1) * LANES],
                                           rows).astype(o_ref.dtype)


def _section_tiles(group, tn):
    col_of = {}
    col = 0
    for name, width, _, _ in _SECTIONS:
        col_of[name] = col
        col += width
    col_tiles, tables, kinds = [], [], []
    for name, width, kind, table in _group_sections(group):
        assert width % tn == 0 and col_of[name] % tn == 0
        n = width // tn
        col_tiles += [col_of[name] // tn + t for t in range(n)]
        tables += [table] * n
        kinds += [kind] * n
    return col_tiles, tables, kinds


def _norm_projection_kernel(x_ref, nw_ref, w_ref, o_ref, xn_ref):
    w = w_ref[...].astype(BF16)
    for rows in _row_bands(x_ref.shape[0], NORM_PROJ_ROW_BANDS):
        x = x_ref[rows, :]
        ms = jnp.mean(x * x, axis=-1, keepdims=True)
        xn = (x * lax.rsqrt(ms + NORM_EPS) * nw_ref[...]).astype(BF16)
        xn_ref[rows, :] = xn
        acc = jnp.dot(xn, w, preferred_element_type=F32)
        for s in range(o_ref.shape[0]):
            o_ref[s, rows, :] = acc[:, s * LANES:(s + 1) * LANES].astype(o_ref.dtype)


def _norm_projection(x2, norm_w, w_in, tm=1024):
    m, d = x2.shape
    (_, tn, kind, _), = _group_sections(_NORM_GROUP)
    assert kind == "plain"
    (col_tile,), _, _ = _section_tiles(_NORM_GROUP, tn)
    return pl.pallas_call(
        _norm_projection_kernel,
        out_shape=[jax.ShapeDtypeStruct((tn // LANES, m, LANES), BF16),
                   jax.ShapeDtypeStruct((m, d), BF16)],
        grid=(m // tm,),
        in_specs=[pl.BlockSpec((tm, d), lambda i: (i, 0)),
                  pl.BlockSpec((1, d), lambda i: (0, 0)),
                  pl.BlockSpec((d, tn), lambda i: (0, col_tile), pipeline_mode=pl.Buffered(1))],
        out_specs=[pl.BlockSpec((tn // LANES, tm, LANES), lambda i: (0, i, 0)),
                   pl.BlockSpec((tm, d), lambda i: (i, 0))],
        compiler_params=_params(1, vmem=NORM_PROJ_VMEM_LIMIT),
        name="norm_projection",
    )(x2, norm_w, w_in)


def _input_projection(xn, w_in, cos_tables, sin_tables, tm=2048, tn=1024):
    m, d = xn.shape
    s_len = cos_tables.shape[1]
    col_tiles, tables, kinds = _section_tiles(_MAIN_GROUP, tn)
    n_rotary = kinds.count("rotary")
    assert kinds[:n_rotary] == ["rotary"] * n_rotary

    def table_map(j, i):
        rotary = j < n_rotary
        return (_select(jnp.minimum(j, n_rotary - 1), tables),
                jnp.where(rotary, i % (s_len // tm), s_len // tm - 1), 0)

    table_spec = pl.BlockSpec((None, tm, LANES), table_map)
    return pl.pallas_call(
        functools.partial(_projection_kernel, tile_kinds=kinds),
        out_shape=jax.ShapeDtypeStruct((len(col_tiles) * tn // LANES, m, LANES), BF16),
        grid=(len(col_tiles), m // tm),
        in_specs=[pl.BlockSpec((tm, d), lambda j, i: (i, 0)),
                  pl.BlockSpec((d, tn), lambda j, i: (0, _select(j, col_tiles))),
                  table_spec, table_spec],
        out_specs=pl.BlockSpec((tn // LANES, tm, LANES), lambda j, i: (j, i, 0)),
        compiler_params=_params(2),
        name="epilogue_projection",
    )(xn, w_in, cos_tables, sin_tables)


def _retention_pieces(lg, q_ref, k_ref, v_ref, o_ref, qx_ref, kz_ref, vc_ref):
    s_len = q_ref.shape[0]
    c = RET_CHUNK
    n_chunks = s_len // c

    rowq = lax.broadcasted_iota(jnp.int32, (c, RET_QK_DIM), 0).astype(F32)
    zeta = jnp.exp(lg * (c - 1.0 - rowq))
    xi = jnp.exp(lg * (rowq + 1.0))
    for n in range(n_chunks):
        sl = slice(n * c, (n + 1) * c)
        qx_ref[sl, :] = (q_ref[sl, :].astype(F32) * xi).astype(BF16)
        kz_ref[sl, :] = (k_ref[sl, :].astype(F32) * zeta).astype(BF16)
    vc_ref[...] = jnp.concatenate([v_ref[0], v_ref[1]], axis=-1)

    row = lax.broadcasted_iota(jnp.int32, (c, c), 0).astype(F32)
    col = lax.broadcasted_iota(jnp.int32, (c, c), 1).astype(F32)
    rel = row - col
    decay_mask = jnp.where(rel >= 0.0, jnp.exp(lg * jnp.maximum(rel, 0.0)), 0.0)
    chunk_decay = jnp.exp(jnp.zeros((1, RET_V_DIM), F32) + lg * c)
    yield

    state = jnp.zeros((RET_QK_DIM, RET_V_DIM), F32)
    for n in range(n_chunks):
        sl = slice(n * c, (n + 1) * c)
        vc = vc_ref[sl, :]
        scores = lax.dot_general(q_ref[sl, :], k_ref[sl, :], (((1,), (1,)), ((), ())),
                                 preferred_element_type=F32) * decay_mask
        o = jnp.dot(scores.astype(BF16), vc, preferred_element_type=F32)
        o = o + jnp.dot(qx_ref[sl, :], state.astype(BF16), preferred_element_type=F32)
        if n + 1 < n_chunks:
            kv = lax.dot_general(kz_ref[sl, :], vc, (((0,), (0,)), ((), ())),
                                 preferred_element_type=F32)
            state = state * chunk_decay + kv

        o_ref[sl, :] = o.astype(o_ref.dtype)
        yield


def _diff_attn_pieces(q_ref, k_ref, v_ref, lq1_ref, lk1_ref, lq2_ref, lk2_ref, o_ref, vc_ref):
    t = ATT_BLOCK
    s_len = vc_ref.shape[0]
    vc_ref[...] = jnp.concatenate([v_ref[0], v_ref[1]], axis=-1)

    lam = (jnp.exp(jnp.sum(lq1_ref[...] * lk1_ref[...], axis=-1, keepdims=True))
           - jnp.exp(jnp.sum(lq2_ref[...] * lk2_ref[...], axis=-1, keepdims=True))
           + LAMBDA_INIT)
    causal = (lax.broadcasted_iota(jnp.int32, (t, t), 1)
              <= lax.broadcasted_iota(jnp.int32, (t, t), 0))
    nt_dims = (((1,), (1,)), ((), ()))

    units = [(n, c) for n in reversed(range(s_len // t)) for c in range(2)]

    def scores(u):
        n, c = units[u]
        r0 = n * t
        q = q_ref[c, r0:r0 + t, :]
        s_diag = jnp.where(causal,
                           lax.dot_general(q, k_ref[c, r0:r0 + t, :], nt_dims,
                                           preferred_element_type=F32), -jnp.inf)
        m = jnp.max(s_diag, axis=-1, keepdims=True)
        s_off = None
        if n > 0:
            s_off = lax.dot_general(q, k_ref[c, 0:r0, :], nt_dims, preferred_element_type=F32)
            m = jnp.maximum(m, jnp.max(s_off, axis=-1, keepdims=True))
        return s_diag, s_off, m

    def probs(sc):
        s_diag, s_off, m = sc
        p_diag = jnp.exp2(s_diag - m)
        l = jnp.sum(p_diag, axis=-1, keepdims=True)
        p_off = None
        if s_off is not None:
            p_off = jnp.exp2(s_off - m)
            l = l + jnp.sum(p_off, axis=-1, keepdims=True)
            p_off = p_off.astype(BF16)
        return p_diag.astype(BF16), p_off, l

    def weighted(u, pr):
        n, _ = units[u]
        r0 = n * t
        p_diag, p_off, l = pr
        acc = jnp.dot(p_diag, vc_ref[r0:r0 + t, :], preferred_element_type=F32)
        if p_off is not None:
            acc = acc + jnp.dot(p_off, vc_ref[0:r0, :], preferred_element_type=F32)
        return acc, l

    def finish(n, res):
        r0 = n * t
        o = res[0][0] * (1.0 / res[0][1]) - res[1][0] * (lam / res[1][1])
        o_ref[r0:r0 + t, :] = o.astype(o_ref.dtype)

    skew = ATT_SKEW
    sc, pr, res = {}, {}, {}
    for step in range(len(units) + 2 * skew):
        u_sc, u_pr, u_pv = step, step - skew, step - 2 * skew
        if 0 <= u_sc < len(units):
            sc[u_sc] = scores(u_sc)
        if 0 <= u_pr < len(units):
            pr[u_pr] = probs(sc.pop(u_pr))
        if 0 <= u_pv < len(units):
            res[u_pv] = weighted(u_pv, pr.pop(u_pv))
            if units[u_pv][1] == 1:
                finish(units[u_pv][0], (res.pop(u_pv - 1), res.pop(u_pv)))
        yield


def _mixers_kernel(lg_ref, rq_ref, rk_ref, rv_ref, dq_ref, dk_ref, dv_ref,
                   lq1_ref, lk1_ref, lq2_ref, lk2_ref, s1_ref, s2_ref, s3_ref,
                   oret_ref, odiff_ref, s1bf_ref, s2bf_ref, s3bf_ref,
                   qx_ref, kz_ref, rvc_ref, dvc_ref):
    for src, dst in ((s1_ref, s1bf_ref), (s2_ref, s2bf_ref), (s3_ref, s3bf_ref)):
        dst[...] = src[...].astype(BF16)

    retention = _retention_pieces(lg_ref[pl.program_id(1)], rq_ref, rk_ref, rv_ref,
                                  oret_ref, qx_ref, kz_ref, rvc_ref)
    attention = _diff_attn_pieces(dq_ref, dk_ref, dv_ref, lq1_ref, lk1_ref, lq2_ref, lk2_ref,
                                  odiff_ref, dvc_ref)
    live = [attention, retention]
    while live:
        for gen, count in zip((attention, retention), MIX_PATTERN):
            for _ in range(count):
                if gen in live and next(gen, "done") == "done":
                    live.remove(gen)


def _token_mixers(main4, norm4, log_gamma, lq1, lk1, lq2, lk2, side_weights):
    _, b, s, _ = main4.shape
    by_section = lambda names: [norm4 if n in _NORM_GROUP else main4 for n in names]
    assert RET_HEADS == DIFF_HEADS
    one = lambda name: pl.BlockSpec(
        (None, None, s, LANES), lambda bi, h, base=_slab_base(name): (base + h, bi, 0, 0))
    pair = lambda name: pl.BlockSpec(
        (2, None, s, LANES), lambda bi, h, base=_slab_base(name): (base // 2 + h, bi, 0, 0))
    vec = lambda n: pl.BlockSpec((1, n), lambda bi, h: (0, 0))
    side_rows, side_cols = side_weights[0].shape
    assert all(w.shape == (side_rows, side_cols) for w in side_weights)
    band = side_rows // (b * DIFF_HEADS)
    side_spec = pl.BlockSpec((band, side_cols), lambda bi, h: (bi * DIFF_HEADS + h, 0))
    head_out = pl.BlockSpec((None, None, s, RET_V_DIM), lambda bi, h: (h, bi, 0, 0))
    outs = pl.pallas_call(
        _mixers_kernel,
        out_shape=[jax.ShapeDtypeStruct((RET_HEADS, b, s, RET_V_DIM), BF16),
                   jax.ShapeDtypeStruct((DIFF_HEADS, b, s, DIFF_V_DIM), BF16)]
                  + [jax.ShapeDtypeStruct((side_rows, side_cols), BF16)] * len(side_weights),
        grid=(b, DIFF_HEADS),
        in_specs=[pl.BlockSpec(memory_space=pltpu.SMEM),
                  one("rq"), one("rk"), pair("rv"), pair("dq"), pair("dk"), pair("dv"),
                  vec(DIFF_HEAD_DIM), vec(DIFF_HEAD_DIM), vec(DIFF_HEAD_DIM), vec(DIFF_HEAD_DIM)]
                 + [side_spec] * len(side_weights),
        out_specs=[head_out, head_out] + [side_spec] * len(side_weights),
        scratch_shapes=[pltpu.VMEM((s, RET_QK_DIM), BF16)] * 2
                       + [pltpu.VMEM((s, RET_V_DIM), BF16), pltpu.VMEM((s, DIFF_V_DIM), BF16)],
        compiler_params=_params(2),
        name="token_mixers",
    )(log_gamma, *by_section(("rq", "rk", "rv", "dq", "dk", "dv")),
      lq1, lk1, lq2, lk2, *side_weights)
    return outs[0], outs[1], outs[2:]


def _tail_kernel(a_ref, b_ref, za_ref, zb_ref, g1_ref, g2_ref, x_ref, w1_ref, w2_ref, wo_ref,
                 subln_ref, fw_ref, o_ref):
    kw = a_ref.shape[2]
    slabs = kw // LANES

    def up(x_slabs, z_slabs, w_ref, eps, gain):
        acc = None
        for k in range(x_slabs.shape[0]):
            o = x_slabs[k].astype(F32)
            ms = jnp.mean(o * o, axis=-1, keepdims=True)
            o = o * lax.rsqrt(ms + eps)
            if gain is not None:
                o = o * gain
            silu_z = jnp.concatenate([z_slabs[slabs * k + s] for s in range(slabs)],
                                     axis=-1).astype(F32)
            part = jnp.dot((o * silu_z).astype(BF16), w_ref[k * kw:(k + 1) * kw, :],
                           preferred_element_type=F32)
            acc = part if acc is None else acc + part
        return acc

    up1 = up(a_ref, za_ref, w1_ref, NORM_EPS, None)
    up2 = up(b_ref, zb_ref, w2_ref, SUBLN_EPS, subln_ref[...] * (1.0 - LAMBDA_INIT))
    mixed = jnp.concatenate(
        [(g1_ref[j].astype(F32) * up1[:, j * LANES:(j + 1) * LANES]
          + g2_ref[j].astype(F32) * up2[:, j * LANES:(j + 1) * LANES]).astype(BF16)
         for j in range(g1_ref.shape[0])], axis=-1)
    h = x_ref[...] + jnp.dot(mixed, wo_ref[...], preferred_element_type=F32)
    ms = jnp.mean(h * h, axis=-1, keepdims=True)
    o_ref[...] = h * lax.rsqrt(ms + NORM_EPS) * fw_ref[...]


def _merge_and_output(o_ret, o_diff, gates, x2, w_ret_up_bf, w_diff_up_bf, w_out_bf, subln_w,
                      final_w, tm=256):
    nk, m, kw = o_ret.shape
    d = x2.shape[1]
    gs = d // LANES
    x_spec = pl.BlockSpec((nk, tm, kw), lambda i: (0, i, 0))
    w_spec = pl.BlockSpec((d, d), lambda i: (0, 0), pipeline_mode=pl.Buffered(1))
    g_spec = lambda name: pl.BlockSpec(
        (gs, tm, LANES), lambda i, base=_slab_base(name): (base // gs, i, 0))
    row_spec = pl.BlockSpec((tm, d), lambda i: (i, 0))
    return pl.pallas_call(
        _tail_kernel,
        out_shape=jax.ShapeDtypeStruct((m, d), F32),
        grid=(m // tm,),
        in_specs=[x_spec, x_spec, g_spec("rz"), g_spec("dz"), g_spec("gr"), g_spec("gd"), row_spec,
                  w_spec, w_spec, w_spec, pl.BlockSpec((1, kw), lambda i: (0, 0)),
                  pl.BlockSpec((1, d), lambda i: (0, 0))],
        out_specs=row_spec,
        compiler_params=_params(1),
        name="merge_and_output",
    )(o_ret, o_diff, gates, gates, gates, gates, x2, w_ret_up_bf, w_diff_up_bf, w_out_bf,
      subln_w, final_w)


def _rotary_tables(s, inv_freq):
    pos = np.arange(s, dtype=np.float32)
    ang = pos[:, None] * inv_freq[None, :].astype(np.float32)
    emb = np.concatenate([ang, ang], axis=-1)
    half = emb.shape[-1] // 2
    sign = np.concatenate([-np.ones((half,), np.float32), np.ones((half,), np.float32)])
    return np.cos(emb), np.sin(emb) * sign[None, :]


def kernel(x, norm_w, w_in, w_ret_up, w_diff_up, w_out, lambda_q1, lambda_k1, lambda_q2,
           lambda_k2, subln_w, final_norm_w):
    b, s, d = x.shape
    assert d == D_MODEL and w_in.shape == (1, D_MODEL, IN_WIDTH)
    x2 = x.reshape(b * s, d)

    f32 = np.float32
    ret_inv_freq = np.exp(f32(-math.log(10000.0)) * np.linspace(0.0, 1.0, RET_QK_DIM // 2, dtype=f32))
    rope_inv_freq = f32(ROPE_THETA) ** (-np.arange(0, DIFF_HEAD_DIM, 2, dtype=f32) / f32(DIFF_HEAD_DIM))
    ret_cos, ret_sin = _rotary_tables(s, ret_inv_freq)
    rope_cos, rope_sin = _rotary_tables(s, rope_inv_freq)
    log_gamma = jnp.asarray(np.log1p(-np.exp2(f32(-5.0) - np.arange(RET_HEADS, dtype=f32))))

    ret_k_scale = f32(RET_QK_DIM ** -0.5)
    diff_q_scale = f32((DIFF_HEAD_DIM ** -0.5) * math.log2(math.e))
    cos_tables = jnp.asarray(
        np.stack([ret_cos, ret_cos * ret_k_scale, rope_cos * diff_q_scale, rope_cos]))
    sin_tables = jnp.asarray(
        np.stack([ret_sin, ret_sin * ret_k_scale, rope_sin * diff_q_scale, rope_sin]))

    proj_norm, xn = _norm_projection(x2, norm_w[0][None, :], w_in[0])
    proj_main = _input_projection(xn, w_in[0], cos_tables, sin_tables)
    by_pos = lambda a: a.reshape(a.shape[0], b, s, LANES)

    o_ret, o_diff, (w_ret_up_bf, w_diff_up_bf, w_out_bf) = _token_mixers(
        by_pos(proj_main), by_pos(proj_norm), log_gamma, lambda_q1, lambda_k1, lambda_q2,
        lambda_k2, (w_ret_up[0], w_diff_up[0], w_out[0]))
    o_ret = o_ret.reshape(RET_HEADS, b * s, RET_V_DIM)
    o_diff = o_diff.reshape(DIFF_HEADS, b * s, DIFF_V_DIM)

    out = _merge_and_output(o_ret, o_diff, proj_main, x2, w_ret_up_bf, w_diff_up_bf, w_out_bf,
                            subln_w, final_norm_w[None, :])
    return out.reshape(b, s, d)
```

```python
import functools
import math

import jax
import jax.numpy as jnp
import numpy as np
from jax import lax
from jax.experimental import pallas as pl
from jax.experimental.pallas import tpu as pltpu

F32 = jnp.float32
BF16 = jnp.bfloat16

D_MODEL = 2048
RET_QK_DIM = 128
RET_V_DIM = 256
RET_HEADS = D_MODEL // RET_V_DIM
DIFF_HEAD_DIM = 128
DIFF_V_DIM = 2 * DIFF_HEAD_DIM
DIFF_HEADS = D_MODEL // DIFF_V_DIM
ROPE_THETA = 10000.0
NORM_EPS = 1e-6
SUBLN_EPS = 1e-5
LAMBDA_INIT = 0.8 - 0.6 * math.exp(-0.3 * 0)

RET_QK_WIDTH = RET_HEADS * RET_QK_DIM
RET_WIDTH = RET_HEADS * RET_V_DIM
DIFF_QK_WIDTH = DIFF_HEADS * 2 * DIFF_HEAD_DIM
DIFF_WIDTH = DIFF_HEADS * DIFF_V_DIM
IN_WIDTH = 2 * RET_QK_WIDTH + 2 * RET_WIDTH + 2 * DIFF_QK_WIDTH + 2 * DIFF_WIDTH + 2 * D_MODEL

LANES = 128

_SECTIONS = (("rq", RET_QK_WIDTH, "rotary", 0), ("rk", RET_QK_WIDTH, "rotary", 1),
             ("rv", RET_WIDTH, "plain", 0), ("rz", RET_WIDTH, "silu", 0),
             ("dq", DIFF_QK_WIDTH, "rotary", 2), ("dk", DIFF_QK_WIDTH, "rotary", 3),
             ("dv", DIFF_WIDTH, "plain", 0), ("dz", DIFF_WIDTH, "silu", 0),
             ("gr", D_MODEL, "sigmoid", 0), ("gd", D_MODEL, "sigmoid", 0))
assert sum(sec[1] for sec in _SECTIONS) == IN_WIDTH
_NORM_GROUP = ("rv",)
_MAIN_GROUP = ("rq", "rk", "dq", "dk", "dv", "rz", "dz", "gr", "gd")
assert sorted(_NORM_GROUP + _MAIN_GROUP) == sorted(sec[0] for sec in _SECTIONS)


def _group_sections(group):
    return [next(sec for sec in _SECTIONS if sec[0] == name) for name in group]


def _slab_base(name):
    for group in (_NORM_GROUP, _MAIN_GROUP):
        base = 0
        for sec_name, width, _, _ in _group_sections(group):
            if sec_name == name:
                return base
            base += width // LANES
    raise KeyError(name)


VMEM_LIMIT = 56 * 1024 * 1024
NORM_PROJ_VMEM_LIMIT = 58 * 1024 * 1024

RET_CHUNK = 256
ATT_BLOCK = 256
ATT_SKEW = 1
PROJ_ROW_BANDS = ((1, 4), (1, 4), (1, 4), (1, 8), (1, 16), (1, 16))
NORM_PROJ_ROW_BANDS = ((1, 4), (1, 4), (1, 4), (1, 8), (1, 8))
MIX_PATTERN = (1, 1)


def _params(n_axes, vmem=VMEM_LIMIT):
    return pltpu.CompilerParams(dimension_semantics=("arbitrary",) * n_axes,
                                vmem_limit_bytes=vmem)


def _rotary(x, cos, sin_signed):
    return x * cos + pltpu.roll(x, x.shape[-1] // 2, axis=x.ndim - 1) * sin_signed


def _sigmoid(g):
    return 0.5 * jnp.tanh(0.5 * g) + 0.5


def _silu(z):
    h = 0.5 * z
    return h * jnp.tanh(h) + h


def _row_bands(tm, fractions):
    bounds = [0]
    for num, den in fractions:
        bounds.append(bounds[-1] + tm * num // den)
    assert bounds[-1] == tm
    return [slice(r0, r1) for r0, r1 in zip(bounds[:-1], bounds[1:])]


def _select(j, values):
    return sum(jnp.where(j == t, v, 0) for t, v in enumerate(values) if v)


def _projection_kernel(x_ref, w_ref, cos_ref, sin_ref, o_ref, *, tile_kinds):
    epilogues = {
        "rotary": lambda a, rows: _rotary(a, cos_ref[rows, :], sin_ref[rows, :]),
        "plain": lambda a, rows: a,
        "silu": lambda a, rows: _silu(a),
        "sigmoid": lambda a, rows: _sigmoid(a),
    }
    j = pl.program_id(0)
    for kind in dict.fromkeys(tile_kinds):
        tiles = [t for t, k in enumerate(tile_kinds) if k == kind]

        @pl.when(functools.reduce(jnp.logical_or, [j == t for t in tiles]))
        def _(fn=epilogues[kind]):
            w = w_ref[...].astype(BF16)
            for rows in _row_bands(x_ref.shape[0], PROJ_ROW_BANDS):
                acc = jnp.dot(x_ref[rows, :], w, preferred_element_type=F32)
                for s in range(o_ref.shape[0]):
                    o_ref[s, rows, :] = fn(acc[:, s * LANES:(s + 1) * LANES],
                                           rows).astype(o_ref.dtype)


def _section_tiles(group, tn):
    col_of = {}
    col = 0
    for name, width, _, _ in _SECTIONS:
        col_of[name] = col
        col += width
    col_tiles, tables, kinds = [], [], []
    for name, width, kind, table in _group_sections(group):
        assert width % tn == 0 and col_of[name] % tn == 0
        n = width // tn
        col_tiles += [col_of[name] // tn + t for t in range(n)]
        tables += [table] * n
        kinds += [kind] * n
    return col_tiles, tables, kinds


def _norm_projection_kernel(x_ref, nw_ref, w_ref, o_ref, xn_ref):
    w = w_ref[...].astype(BF16)
    for rows in _row_bands(x_ref.shape[0], NORM_PROJ_ROW_BANDS):
        x = x_ref[rows, :]
        ms = jnp.mean(x * x, axis=-1, keepdims=True)
        xn = (x * lax.rsqrt(ms + NORM_EPS) * nw_ref[...]).astype(BF16)
        xn_ref[rows, :] = xn
        acc = jnp.dot(xn, w, preferred_element_type=F32)
        width = o_ref.shape[2]
        for s in range(o_ref.shape[0]):
            o_ref[s, rows, :] = acc[:, s * width:(s + 1) * width].astype(o_ref.dtype)


def _norm_projection(x2, norm_w, w_in, head_width, tm=1024):
    m, d = x2.shape
    (_, tn, kind, _), = _group_sections(_NORM_GROUP)
    assert kind == "plain" and tn % head_width == 0
    (col_tile,), _, _ = _section_tiles(_NORM_GROUP, tn)
    return pl.pallas_call(
        _norm_projection_kernel,
        out_shape=[jax.ShapeDtypeStruct((tn // head_width, m, head_width), BF16),
                   jax.ShapeDtypeStruct((m, d), BF16)],
        grid=(m // tm,),
        in_specs=[pl.BlockSpec((tm, d), lambda i: (i, 0)),
                  pl.BlockSpec((1, d), lambda i: (0, 0)),
                  pl.BlockSpec((d, tn), lambda i: (0, col_tile), pipeline_mode=pl.Buffered(1))],
        out_specs=[pl.BlockSpec((tn // head_width, tm, head_width), lambda i: (0, i, 0)),
                   pl.BlockSpec((tm, d), lambda i: (i, 0))],
        compiler_params=_params(1, vmem=NORM_PROJ_VMEM_LIMIT),
        name="norm_projection",
    )(x2, norm_w, w_in)


def _input_projection(xn, w_in, cos_tables, sin_tables, tm=2048, tn=1024):
    m, d = xn.shape
    s_len = cos_tables.shape[1]
    col_tiles, tables, kinds = _section_tiles(_MAIN_GROUP, tn)
    n_rotary = kinds.count("rotary")
    assert kinds[:n_rotary] == ["rotary"] * n_rotary

    def table_map(j, i):
        rotary = j < n_rotary
        return (_select(jnp.minimum(j, n_rotary - 1), tables),
                jnp.where(rotary, i % (s_len // tm), s_len // tm - 1), 0)

    table_spec = pl.BlockSpec((None, tm, LANES), table_map)
    return pl.pallas_call(
        functools.partial(_projection_kernel, tile_kinds=kinds),
        out_shape=jax.ShapeDtypeStruct((len(col_tiles) * tn // LANES, m, LANES), BF16),
        grid=(len(col_tiles), m // tm),
        in_specs=[pl.BlockSpec((tm, d), lambda j, i: (i, 0)),
                  pl.BlockSpec((d, tn), lambda j, i: (0, _select(j, col_tiles))),
                  table_spec, table_spec],
        out_specs=pl.BlockSpec((tn // LANES, tm, LANES), lambda j, i: (j, i, 0)),
        compiler_params=_params(2),
        name="epilogue_projection",
    )(xn, w_in, cos_tables, sin_tables)


def _retention_pieces(lg, q_ref, k_ref, v_ref, o_ref, qx_ref, kz_ref):
    s_len = q_ref.shape[0]
    c = RET_CHUNK
    n_chunks = s_len // c

    rowq = lax.broadcasted_iota(jnp.int32, (c, RET_QK_DIM), 0).astype(F32)
    zeta = jnp.exp(lg * (c - 1.0 - rowq))
    xi = jnp.exp(lg * (rowq + 1.0))
    for n in range(n_chunks):
        sl = slice(n * c, (n + 1) * c)
        qx_ref[sl, :] = (q_ref[sl, :].astype(F32) * xi).astype(BF16)
        kz_ref[sl, :] = (k_ref[sl, :].astype(F32) * zeta).astype(BF16)

    row = lax.broadcasted_iota(jnp.int32, (c, c), 0).astype(F32)
    col = lax.broadcasted_iota(jnp.int32, (c, c), 1).astype(F32)
    rel = row - col
    decay_mask = jnp.where(rel >= 0.0, jnp.exp(lg * jnp.maximum(rel, 0.0)), 0.0)
    chunk_decay = jnp.exp(jnp.zeros((1, RET_V_DIM), F32) + lg * c)
    yield

    state = jnp.zeros((RET_QK_DIM, RET_V_DIM), F32)
    for n in range(n_chunks):
        sl = slice(n * c, (n + 1) * c)
        vc = v_ref[sl, :]
        scores = lax.dot_general(q_ref[sl, :], k_ref[sl, :], (((1,), (1,)), ((), ())),
                                 preferred_element_type=F32) * decay_mask
        o = jnp.dot(scores.astype(BF16), vc, preferred_element_type=F32)
        o = o + jnp.dot(qx_ref[sl, :], state.astype(BF16), preferred_element_type=F32)
        if n + 1 < n_chunks:
            kv = lax.dot_general(kz_ref[sl, :], vc, (((0,), (0,)), ((), ())),
                                 preferred_element_type=F32)
            state = state * chunk_decay + kv

        o_ref[sl, :] = o.astype(o_ref.dtype)
        yield


def _diff_attn_pieces(q_ref, k_ref, v_ref, lq1_ref, lk1_ref, lq2_ref, lk2_ref, o_ref, vc_ref):
    t = ATT_BLOCK
    s_len = vc_ref.shape[0]
    vc_ref[...] = jnp.concatenate([v_ref[0], v_ref[1]], axis=-1)

    lam = (jnp.exp(jnp.sum(lq1_ref[...] * lk1_ref[...], axis=-1, keepdims=True))
           - jnp.exp(jnp.sum(lq2_ref[...] * lk2_ref[...], axis=-1, keepdims=True))
           + LAMBDA_INIT)
    causal = (lax.broadcasted_iota(jnp.int32, (t, t), 1)
              <= lax.broadcasted_iota(jnp.int32, (t, t), 0))
    nt_dims = (((1,), (1,)), ((), ()))

    units = [(n, c) for n in reversed(range(s_len // t)) for c in range(2)]

    def scores(u):
        n, c = units[u]
        r0 = n * t
        q = q_ref[c, r0:r0 + t, :]
        s_diag = jnp.where(causal,
                           lax.dot_general(q, k_ref[c, r0:r0 + t, :], nt_dims,
                                           preferred_element_type=F32), -jnp.inf)
        m = jnp.max(s_diag, axis=-1, keepdims=True)
        s_off = None
        if n > 0:
            s_off = lax.dot_general(q, k_ref[c, 0:r0, :], nt_dims, preferred_element_type=F32)
            m = jnp.maximum(m, jnp.max(s_off, axis=-1, keepdims=True))
        return s_diag, s_off, m

    def probs(sc):
        s_diag, s_off, m = sc
        p_diag = jnp.exp2(s_diag - m)
        l = jnp.sum(p_diag, axis=-1, keepdims=True)
        p_off = None
        if s_off is not None:
            p_off = jnp.exp2(s_off - m)
            l = l + jnp.sum(p_off, axis=-1, keepdims=True)
            p_off = p_off.astype(BF16)
        return p_diag.astype(BF16), p_off, l

    def weighted(u, pr):
        n, _ = units[u]
        r0 = n * t
        p_diag, p_off, l = pr
        acc = jnp.dot(p_diag, vc_ref[r0:r0 + t, :], preferred_element_type=F32)
        if p_off is not None:
            acc = acc + jnp.dot(p_off, vc_ref[0:r0, :], preferred_element_type=F32)
        return acc, l

    def finish(n, res):
        r0 = n * t
        o = res[0][0] * (1.0 / res[0][1]) - res[1][0] * (lam / res[1][1])
        o_ref[r0:r0 + t, :] = o.astype(o_ref.dtype)

    skew = ATT_SKEW
    sc, pr, res = {}, {}, {}
    for step in range(len(units) + 2 * skew):
        u_sc, u_pr, u_pv = step, step - skew, step - 2 * skew
        if 0 <= u_sc < len(units):
            sc[u_sc] = scores(u_sc)
        if 0 <= u_pr < len(units):
            pr[u_pr] = probs(sc.pop(u_pr))
        if 0 <= u_pv < len(units):
            res[u_pv] = weighted(u_pv, pr.pop(u_pv))
            if units[u_pv][1] == 1:
                finish(units[u_pv][0], (res.pop(u_pv - 1), res.pop(u_pv)))
        yield


def _mixers_kernel(lg_ref, rq_ref, rk_ref, rv_ref, dq_ref, dk_ref, dv_ref,
                   lq1_ref, lk1_ref, lq2_ref, lk2_ref, s1_ref, s2_ref, s3_ref,
                   oret_ref, odiff_ref, s1bf_ref, s2bf_ref, s3bf_ref,
                   qx_ref, kz_ref, dvc_ref):
    for src, dst in ((s1_ref, s1bf_ref), (s2_ref, s2bf_ref), (s3_ref, s3bf_ref)):
        dst[...] = src[...].astype(BF16)

    retention = _retention_pieces(lg_ref[pl.program_id(1)], rq_ref, rk_ref, rv_ref,
                                  oret_ref, qx_ref, kz_ref)
    attention = _diff_attn_pieces(dq_ref, dk_ref, dv_ref, lq1_ref, lk1_ref, lq2_ref, lk2_ref,
                                  odiff_ref, dvc_ref)
    live = [attention, retention]
    while live:
        for gen, count in zip((attention, retention), MIX_PATTERN):
            for _ in range(count):
                if gen in live and next(gen, "done") == "done":
                    live.remove(gen)


def _token_mixers(main4, rv4, log_gamma, lq1, lk1, lq2, lk2, side_weights):
    _, b, s, _ = main4.shape
    assert RET_HEADS == DIFF_HEADS and _NORM_GROUP == ("rv",)
    one = lambda name: pl.BlockSpec(
        (None, None, s, LANES), lambda bi, h, base=_slab_base(name): (base + h, bi, 0, 0))
    pair = lambda name: pl.BlockSpec(
        (2, None, s, LANES), lambda bi, h, base=_slab_base(name): (base // 2 + h, bi, 0, 0))
    vec = lambda n: pl.BlockSpec((1, n), lambda bi, h: (0, 0))
    side_rows, side_cols = side_weights[0].shape
    assert all(w.shape == (side_rows, side_cols) for w in side_weights)
    band = side_rows // (b * DIFF_HEADS)
    side_spec = pl.BlockSpec((band, side_cols), lambda bi, h: (bi * DIFF_HEADS + h, 0))
    head_out = pl.BlockSpec((None, None, s, RET_V_DIM), lambda bi, h: (h, bi, 0, 0))
    outs = pl.pallas_call(
        _mixers_kernel,
        out_shape=[jax.ShapeDtypeStruct((RET_HEADS, b, s, RET_V_DIM), BF16),
                   jax.ShapeDtypeStruct((DIFF_HEADS, b, s, DIFF_V_DIM), BF16)]
                  + [jax.ShapeDtypeStruct((side_rows, side_cols), BF16)] * len(side_weights),
        grid=(b, DIFF_HEADS),
        in_specs=[pl.BlockSpec(memory_space=pltpu.SMEM),
                  one("rq"), one("rk"), head_out, pair("dq"), pair("dk"), pair("dv"),
                  vec(DIFF_HEAD_DIM), vec(DIFF_HEAD_DIM), vec(DIFF_HEAD_DIM), vec(DIFF_HEAD_DIM)]
                 + [side_spec] * len(side_weights),
        out_specs=[head_out, head_out] + [side_spec] * len(side_weights),
        scratch_shapes=[pltpu.VMEM((s, RET_QK_DIM), BF16)] * 2
                       + [pltpu.VMEM((s, DIFF_V_DIM), BF16)],
        compiler_params=_params(2),
        name="token_mixers",
    )(log_gamma, main4, main4, rv4, main4, main4, main4, lq1, lk1, lq2, lk2, *side_weights)
    return outs[0], outs[1], outs[2:]


def _tail_kernel(a_ref, b_ref, za_ref, zb_ref, g1_ref, g2_ref, x_ref, w1_ref, w2_ref, wo_ref,
                 subln_ref, fw_ref, o_ref):
    kw = a_ref.shape[2]
    slabs = kw // LANES

    def up(x_slabs, z_slabs, w_ref, eps, gain):
        acc = None
        for k in range(x_slabs.shape[0]):
            o = x_slabs[k].astype(F32)
            ms = jnp.mean(o * o, axis=-1, keepdims=True)
            o = o * lax.rsqrt(ms + eps)
            if gain is not None:
                o = o * gain
            silu_z = jnp.concatenate([z_slabs[slabs * k + s] for s in range(slabs)],
                                     axis=-1).astype(F32)
            part = jnp.dot((o * silu_z).astype(BF16), w_ref[k * kw:(k + 1) * kw, :],
                           preferred_element_type=F32)
            acc = part if acc is None else acc + part
        return acc

    up1 = up(a_ref, za_ref, w1_ref, NORM_EPS, None)
    up2 = up(b_ref, zb_ref, w2_ref, SUBLN_EPS, subln_ref[...] * (1.0 - LAMBDA_INIT))
    mixed = jnp.concatenate(
        [(g1_ref[j].astype(F32) * up1[:, j * LANES:(j + 1) * LANES]
          + g2_ref[j].astype(F32) * up2[:, j * LANES:(j + 1) * LANES]).astype(BF16)
         for j in range(g1_ref.shape[0])], axis=-1)
    h = x_ref[...] + jnp.dot(mixed, wo_ref[...], preferred_element_type=F32)
    ms = jnp.mean(h * h, axis=-1, keepdims=True)
    o_ref[...] = h * lax.rsqrt(ms + NORM_EPS) * fw_ref[...]


def _merge_and_output(o_ret, o_diff, gates, x2, w_ret_up_bf, w_diff_up_bf, w_out_bf, subln_w,
                      final_w, tm=256):
    nk, m, kw = o_ret.shape
    d = x2.shape[1]
    gs = d // LANES
    x_spec = pl.BlockSpec((nk, tm, kw), lambda i: (0, i, 0))
    w_spec = pl.BlockSpec((d, d), lambda i: (0, 0), pipeline_mode=pl.Buffered(1))
    g_spec = lambda name: pl.BlockSpec(
        (gs, tm, LANES), lambda i, base=_slab_base(name): (base // gs, i, 0))
    row_spec = pl.BlockSpec((tm, d), lambda i: (i, 0))
    return pl.pallas_call(
        _tail_kernel,
        out_shape=jax.ShapeDtypeStruct((m, d), F32),
        grid=(m // tm,),
        in_specs=[x_spec, x_spec, g_spec("rz"), g_spec("dz"), g_spec("gr"), g_spec("gd"), row_spec,
                  w_spec, w_spec, w_spec, pl.BlockSpec((1, kw), lambda i: (0, 0)),
                  pl.BlockSpec((1, d), lambda i: (0, 0))],
        out_specs=row_spec,
        compiler_params=_params(1),
        name="merge_and_output",
    )(o_ret, o_diff, gates, gates, gates, gates, x2, w_ret_up_bf, w_diff_up_bf, w_out_bf,
      subln_w, final_w)


def _rotary_tables(s, inv_freq):
    pos = np.arange(s, dtype=np.float32)
    ang = pos[:, None] * inv_freq[None, :].astype(np.float32)
    emb = np.concatenate([ang, ang], axis=-1)
    half = emb.shape[-1] // 2
    sign = np.concatenate([-np.ones((half,), np.float32), np.ones((half,), np.float32)])
    return np.cos(emb), np.sin(emb) * sign[None, :]


def kernel(x, norm_w, w_in, w_ret_up, w_diff_up, w_out, lambda_q1, lambda_k1, lambda_q2,
           lambda_k2, subln_w, final_norm_w):
    b, s, d = x.shape
    assert d == D_MODEL and w_in.shape == (1, D_MODEL, IN_WIDTH)
    x2 = x.reshape(b * s, d)

    f32 = np.float32
    ret_inv_freq = np.exp(f32(-math.log(10000.0)) * np.linspace(0.0, 1.0, RET_QK_DIM // 2, dtype=f32))
    rope_inv_freq = f32(ROPE_THETA) ** (-np.arange(0, DIFF_HEAD_DIM, 2, dtype=f32) / f32(DIFF_HEAD_DIM))
    ret_cos, ret_sin = _rotary_tables(s, ret_inv_freq)
    rope_cos, rope_sin = _rotary_tables(s, rope_inv_freq)
    log_gamma = jnp.asarray(np.log1p(-np.exp2(f32(-5.0) - np.arange(RET_HEADS, dtype=f32))))

    ret_k_scale = f32(RET_QK_DIM ** -0.5)
    diff_q_scale = f32((DIFF_HEAD_DIM ** -0.5) * math.log2(math.e))
    cos_tables = jnp.asarray(
        np.stack([ret_cos, ret_cos * ret_k_scale, rope_cos * diff_q_scale, rope_cos]))
    sin_tables = jnp.asarray(
        np.stack([ret_sin, ret_sin * ret_k_scale, rope_sin * diff_q_scale, rope_sin]))

    ret_v, xn = _norm_projection(x2, norm_w[0][None, :], w_in[0], RET_V_DIM)
    proj_main = _input_projection(xn, w_in[0], cos_tables, sin_tables)
    by_pos = lambda a: a.reshape(a.shape[0], b, s, a.shape[-1])

    o_ret, o_diff, (w_ret_up_bf, w_diff_up_bf, w_out_bf) = _token_mixers(
        by_pos(proj_main), by_pos(ret_v), log_gamma, lambda_q1, lambda_k1, lambda_q2,
        lambda_k2, (w_ret_up[0], w_diff_up[0], w_out[0]))
    o_ret = o_ret.reshape(RET_HEADS, b * s, RET_V_DIM)
    o_diff = o_diff.reshape(DIFF_HEADS, b * s, DIFF_V_DIM)

    out = _merge_and_output(o_ret, o_diff, proj_main, x2, w_ret_up_bf, w_diff_up_bf, w_out_bf,
                            subln_w, final_norm_w[None, :])
    return out.reshape(b, s, d)
```

```python
import functools
import math

import jax
import jax.numpy as jnp
import numpy as np
from jax import lax
from jax.experimental import pallas as pl
from jax.experimental.pallas import tpu as pltpu

F32 = jnp.float32
BF16 = jnp.bfloat16

D_MODEL = 2048
RET_QK_DIM = 128
RET_V_DIM = 256
RET_HEADS = D_MODEL // RET_V_DIM
DIFF_HEAD_DIM = 128
DIFF_V_DIM = 2 * DIFF_HEAD_DIM
DIFF_HEADS = D_MODEL // DIFF_V_DIM
ROPE_THETA = 10000.0
NORM_EPS = 1e-6
SUBLN_EPS = 1e-5
LAMBDA_INIT = 0.8 - 0.6 * math.exp(-0.3 * 0)

RET_QK_WIDTH = RET_HEADS * RET_QK_DIM
RET_WIDTH = RET_HEADS * RET_V_DIM
DIFF_QK_WIDTH = DIFF_HEADS * 2 * DIFF_HEAD_DIM
DIFF_WIDTH = DIFF_HEADS * DIFF_V_DIM
IN_WIDTH = 2 * RET_QK_WIDTH + 2 * RET_WIDTH + 2 * DIFF_QK_WIDTH + 2 * DIFF_WIDTH + 2 * D_MODEL

LANES = 128

_SECTIONS = (("rq", RET_QK_WIDTH, "rotary", 0), ("rk", RET_QK_WIDTH, "rotary", 1),
             ("rv", RET_WIDTH, "plain", 0), ("rz", RET_WIDTH, "silu", 0),
             ("dq", DIFF_QK_WIDTH, "rotary", 2), ("dk", DIFF_QK_WIDTH, "rotary", 3),
             ("dv", DIFF_WIDTH, "plain", 0), ("dz", DIFF_WIDTH, "silu", 0),
             ("gr", D_MODEL, "sigmoid", 0), ("gd", D_MODEL, "sigmoid", 0))
assert sum(sec[1] for sec in _SECTIONS) == IN_WIDTH
_NORM_GROUP = ("rv",)
_MAIN_GROUP = ("rq", "rk", "dq", "dk", "dv", "rz", "dz", "gr", "gd")
assert sorted(_NORM_GROUP + _MAIN_GROUP) == sorted(sec[0] for sec in _SECTIONS)


def _group_sections(group):
    return [next(sec for sec in _SECTIONS if sec[0] == name) for name in group]


def _slab_base(name):
    for group in (_NORM_GROUP, _MAIN_GROUP):
        base = 0
        for sec_name, width, _, _ in _group_sections(group):
            if sec_name == name:
                return base
            base += width // LANES
    raise KeyError(name)


VMEM_LIMIT = 56 * 1024 * 1024
NORM_PROJ_VMEM_LIMIT = 58 * 1024 * 1024

RET_CHUNK = 256
ATT_BLOCK = 256
ATT_SKEW = 1
PROJ_ROW_BANDS = ((1, 4), (1, 4), (1, 4), (1, 8), (1, 16), (1, 16))
NORM_PROJ_ROW_BANDS = ((1, 4), (1, 4), (1, 4), (1, 8), (1, 8))
MIX_PATTERN = (1, 1)


def _params(n_axes, vmem=VMEM_LIMIT):
    return pltpu.CompilerParams(dimension_semantics=("arbitrary",) * n_axes,
                                vmem_limit_bytes=vmem)


def _rotary(x, cos, sin_signed):
    return x * cos + pltpu.roll(x, x.shape[-1] // 2, axis=x.ndim - 1) * sin_signed


def _sigmoid(g):
    return 0.5 * jnp.tanh(0.5 * g) + 0.5


def _silu(z):
    h = 0.5 * z
    return h * jnp.tanh(h) + h


def _row_bands(tm, fractions):
    bounds = [0]
    for num, den in fractions:
        bounds.append(bounds[-1] + tm * num // den)
    assert bounds[-1] == tm
    return [slice(r0, r1) for r0, r1 in zip(bounds[:-1], bounds[1:])]


def _select(j, values):
    return sum(jnp.where(j == t, v, 0) for t, v in enumerate(values) if v)


def _projection_kernel(x_ref, w_ref, cos_ref, sin_ref, o_ref, *, tile_kinds):
    epilogues = {
        "rotary": lambda a, rows: _rotary(a, cos_ref[rows, :], sin_ref[rows, :]),
        "plain": lambda a, rows: a,
        "silu": lambda a, rows: _silu(a),
        "sigmoid": lambda a, rows: _sigmoid(a),
    }
    j = pl.program_id(0)
    for kind in dict.fromkeys(tile_kinds):
        tiles = [t for t, k in enumerate(tile_kinds) if k == kind]

        @pl.when(functools.reduce(jnp.logical_or, [j == t for t in tiles]))
        def _(fn=epilogues[kind]):
            w = w_ref[...].astype(BF16)
            for rows in _row_bands(x_ref.shape[0], PROJ_ROW_BANDS):
                acc = jnp.dot(x_ref[rows, :], w, preferred_element_type=F32)
                for s in range(o_ref.shape[0]):
                    o_ref[s, rows, :] = fn(acc[:, s * LANES:(s + 1) * LANES],
                                           rows).astype(o_ref.dtype)


def _section_tiles(group, tn):
    col_of = {}
    col = 0
    for name, width, _, _ in _SECTIONS:
        col_of[name] = col
        col += width
    col_tiles, tables, kinds = [], [], []
    for name, width, kind, table in _group_sections(group):
        assert width % tn == 0 and col_of[name] % tn == 0
        n = width // tn
        col_tiles += [col_of[name] // tn + t for t in range(n)]
        tables += [table] * n
        kinds += [kind] * n
    return col_tiles, tables, kinds


def _norm_projection_kernel(x_ref, nw_ref, w_ref, o_ref, xn_ref):
    w = w_ref[...].astype(BF16)
    for rows in _row_bands(x_ref.shape[0], NORM_PROJ_ROW_BANDS):
        x = x_ref[rows, :]
        ms = jnp.mean(x * x, axis=-1, keepdims=True)
        xn = (x * lax.rsqrt(ms + NORM_EPS) * nw_ref[...]).astype(BF16)
        xn_ref[rows, :] = xn
        acc = jnp.dot(xn, w, preferred_element_type=F32)
        width = o_ref.shape[2]
        for s in range(o_ref.shape[0]):
            o_ref[s, rows, :] = acc[:, s * width:(s + 1) * width].astype(o_ref.dtype)


def _norm_projection(x2, norm_w, w_in, head_width, tm=1024):
    m, d = x2.shape
    (_, tn, kind, _), = _group_sections(_NORM_GROUP)
    assert kind == "plain" and tn % head_width == 0
    (col_tile,), _, _ = _section_tiles(_NORM_GROUP, tn)
    return pl.pallas_call(
        _norm_projection_kernel,
        out_shape=[jax.ShapeDtypeStruct((tn // head_width, m, head_width), BF16),
                   jax.ShapeDtypeStruct((m, d), BF16)],
        grid=(m // tm,),
        in_specs=[pl.BlockSpec((tm, d), lambda i: (i, 0)),
                  pl.BlockSpec((1, d), lambda i: (0, 0)),
                  pl.BlockSpec((d, tn), lambda i: (0, col_tile), pipeline_mode=pl.Buffered(1))],
        out_specs=[pl.BlockSpec((tn // head_width, tm, head_width), lambda i: (0, i, 0)),
                   pl.BlockSpec((tm, d), lambda i: (i, 0))],
        compiler_params=_params(1, vmem=NORM_PROJ_VMEM_LIMIT),
        name="norm_projection",
    )(x2, norm_w, w_in)


def _input_projection(xn, w_in, cos_tables, sin_tables, tm=2048, tn=1024):
    m, d = xn.shape
    s_len = cos_tables.shape[1]
    col_tiles, tables, kinds = _section_tiles(_MAIN_GROUP, tn)
    n_rotary = kinds.count("rotary")
    assert kinds[:n_rotary] == ["rotary"] * n_rotary

    def table_map(j, i):
        rotary = j < n_rotary
        return (_select(jnp.minimum(j, n_rotary - 1), tables),
                jnp.where(rotary, i % (s_len // tm), s_len // tm - 1), 0)

    table_spec = pl.BlockSpec((None, tm, LANES), table_map)
    return pl.pallas_call(
        functools.partial(_projection_kernel, tile_kinds=kinds),
        out_shape=jax.ShapeDtypeStruct((len(col_tiles) * tn // LANES, m, LANES), BF16),
        grid=(len(col_tiles), m // tm),
        in_specs=[pl.BlockSpec((tm, d), lambda j, i: (i, 0)),
                  pl.BlockSpec((d, tn), lambda j, i: (0, _select(j, col_tiles))),
                  table_spec, table_spec],
        out_specs=pl.BlockSpec((tn // LANES, tm, LANES), lambda j, i: (j, i, 0)),
        compiler_params=_params(2),
        name="epilogue_projection",
    )(xn, w_in, cos_tables, sin_tables)


def _retention_pieces(lg, q_ref, k_ref, v_ref, o_ref, qx_ref, kz_ref):
    s_len = q_ref.shape[0]
    c = RET_CHUNK
    n_chunks = s_len // c

    rowq = lax.broadcasted_iota(jnp.int32, (c, RET_QK_DIM), 0).astype(F32)
    zeta = jnp.exp(lg * (c - 1.0 - rowq))
    xi = jnp.exp(lg * (rowq + 1.0))
    for n in range(n_chunks):
        sl = slice(n * c, (n + 1) * c)
        qx_ref[sl, :] = (q_ref[sl, :].astype(F32) * xi).astype(BF16)
        kz_ref[sl, :] = (k_ref[sl, :].astype(F32) * zeta).astype(BF16)

    row = lax.broadcasted_iota(jnp.int32, (c, c), 0).astype(F32)
    col = lax.broadcasted_iota(jnp.int32, (c, c), 1).astype(F32)
    rel = row - col
    decay_mask = jnp.where(rel >= 0.0, jnp.exp(lg * jnp.maximum(rel, 0.0)), 0.0)
    chunk_decay = jnp.exp(jnp.zeros((1, RET_V_DIM), F32) + lg * c)
    yield

    state = jnp.zeros((RET_QK_DIM, RET_V_DIM), F32)
    for n in range(n_chunks):
        sl = slice(n * c, (n + 1) * c)
        vc = v_ref[sl, :]
        scores = lax.dot_general(q_ref[sl, :], k_ref[sl, :], (((1,), (1,)), ((), ())),
                                 preferred_element_type=F32) * decay_mask
        o = jnp.dot(scores.astype(BF16), vc, preferred_element_type=F32)
        o = o + jnp.dot(qx_ref[sl, :], state.astype(BF16), preferred_element_type=F32)
        if n + 1 < n_chunks:
            kv = lax.dot_general(kz_ref[sl, :], vc, (((0,), (0,)), ((), ())),
                                 preferred_element_type=F32)
            state = state * chunk_decay + kv

        o_ref[sl, :] = o.astype(o_ref.dtype)
        yield


def _diff_attn_pieces(q_ref, k_ref, v_ref, lq1_ref, lk1_ref, lq2_ref, lk2_ref, o_ref, vc_ref):
    t = ATT_BLOCK
    s_len = vc_ref.shape[0]
    vc_ref[...] = jnp.concatenate([v_ref[0], v_ref[1]], axis=-1)

    lam = (jnp.exp(jnp.sum(lq1_ref[...] * lk1_ref[...], axis=-1, keepdims=True))
           - jnp.exp(jnp.sum(lq2_ref[...] * lk2_ref[...], axis=-1, keepdims=True))
           + LAMBDA_INIT)
    causal = (lax.broadcasted_iota(jnp.int32, (t, t), 1)
              <= lax.broadcasted_iota(jnp.int32, (t, t), 0))
    nt_dims = (((1,), (1,)), ((), ()))

    units = [(n, c) for n in reversed(range(s_len // t)) for c in range(2)]

    def scores(u):
        n, c = units[u]
        r0 = n * t
        q = q_ref[c, r0:r0 + t, :]
        s_diag = jnp.where(causal,
                           lax.dot_general(q, k_ref[c, r0:r0 + t, :], nt_dims,
                                           preferred_element_type=F32), -jnp.inf)
        m = jnp.max(s_diag, axis=-1, keepdims=True)
        s_off = None
        if n > 0:
            s_off = lax.dot_general(q, k_ref[c, 0:r0, :], nt_dims, preferred_element_type=F32)
            m = jnp.maximum(m, jnp.max(s_off, axis=-1, keepdims=True))
        return s_diag, s_off, m

    def probs(sc):
        s_diag, s_off, m = sc
        p_diag = jnp.exp2(s_diag - m)
        l = jnp.sum(p_diag, axis=-1, keepdims=True)
        p_off = None
        if s_off is not None:
            p_off = jnp.exp2(s_off - m)
            l = l + jnp.sum(p_off, axis=-1, keepdims=True)
            p_off = p_off.astype(BF16)
        return p_diag.astype(BF16), p_off, l

    def weighted(u, pr):
        n, _ = units[u]
        r0 = n * t
        p_diag, p_off, l = pr
        acc = jnp.dot(p_diag, vc_ref[r0:r0 + t, :], preferred_element_type=F32)
        if p_off is not None:
            acc = acc + jnp.dot(p_off, vc_ref[0:r0, :], preferred_element_type=F32)
        return acc, l

    def finish(n, res):
        r0 = n * t
        o = res[0][0] * (1.0 / res[0][1]) - res[1][0] * (lam / res[1][1])
        o_ref[r0:r0 + t, :] = o.astype(o_ref.dtype)

    skew = ATT_SKEW
    sc, pr, res = {}, {}, {}
    for step in range(len(units) + 2 * skew):
        u_sc, u_pr, u_pv = step, step - skew, step - 2 * skew
        if 0 <= u_sc < len(units):
            sc[u_sc] = scores(u_sc)
        if 0 <= u_pr < len(units):
            pr[u_pr] = probs(sc.pop(u_pr))
        if 0 <= u_pv < len(units):
            res[u_pv] = weighted(u_pv, pr.pop(u_pv))
            if units[u_pv][1] == 1:
                finish(units[u_pv][0], (res.pop(u_pv - 1), res.pop(u_pv)))
        yield


def _mixers_kernel(lg_ref, rq_ref, rk_ref, rv_ref, dq_ref, dk_ref, dv_ref,
                   lq1_ref, lk1_ref, lq2_ref, lk2_ref, s1_ref, s2_ref, s3_ref,
                   oret_ref, odiff_ref, s1bf_ref, s2bf_ref, s3bf_ref,
                   qx_ref, kz_ref, dvc_ref):
    for src, dst in ((s1_ref, s1bf_ref), (s2_ref, s2bf_ref), (s3_ref, s3bf_ref)):
        dst[...] = src[...].astype(BF16)

    retention = _retention_pieces(lg_ref[pl.program_id(1)], rq_ref, rk_ref, rv_ref,
                                  oret_ref, qx_ref, kz_ref)
    attention = _diff_attn_pieces(dq_ref, dk_ref, dv_ref, lq1_ref, lk1_ref, lq2_ref, lk2_ref,
                                  odiff_ref, dvc_ref)
    live = [attention, retention]
    while live:
        for gen, count in zip((attention, retention), MIX_PATTERN):
            for _ in range(count):
                if gen in live and next(gen, "done") == "done":
                    live.remove(gen)


def _token_mixers(main4, rv4, log_gamma, lq1, lk1, lq2, lk2, side_weights):
    _, b, s, _ = main4.shape
    assert RET_HEADS == DIFF_HEADS and _NORM_GROUP == ("rv",)
    one = lambda name: pl.BlockSpec(
        (None, None, s, LANES), lambda bi, h, base=_slab_base(name): (base + h, bi, 0, 0))
    pair = lambda name: pl.BlockSpec(
        (2, None, s, LANES), lambda bi, h, base=_slab_base(name): (base // 2 + h, bi, 0, 0))
    vec = lambda n: pl.BlockSpec((1, n), lambda bi, h: (0, 0))
    side_rows, side_cols = side_weights[0].shape
    assert all(w.shape == (side_rows, side_cols) for w in side_weights)
    band = side_rows // (b * DIFF_HEADS)
    side_spec = pl.BlockSpec((band, side_cols), lambda bi, h: (bi * DIFF_HEADS + h, 0))
    head_out = pl.BlockSpec((None, None, s, RET_V_DIM), lambda bi, h: (h, bi, 0, 0))
    outs = pl.pallas_call(
        _mixers_kernel,
        out_shape=[jax.ShapeDtypeStruct((RET_HEADS, b, s, RET_V_DIM), BF16),
                   jax.ShapeDtypeStruct((DIFF_HEADS, b, s, DIFF_V_DIM), BF16)]
                  + [jax.ShapeDtypeStruct((side_rows, side_cols), BF16)] * len(side_weights),
        grid=(b, DIFF_HEADS),
        in_specs=[pl.BlockSpec(memory_space=pltpu.SMEM),
                  one("rq"), one("rk"), head_out, pair("dq"), pair("dk"), pair("dv"),
                  vec(DIFF_HEAD_DIM), vec(DIFF_HEAD_DIM), vec(DIFF_HEAD_DIM), vec(DIFF_HEAD_DIM)]
                 + [side_spec] * len(side_weights),
        out_specs=[head_out, head_out] + [side_spec] * len(side_weights),
        scratch_shapes=[pltpu.VMEM((s, RET_QK_DIM), BF16)] * 2
                       + [pltpu.VMEM((s, DIFF_V_DIM), BF16)],
        compiler_params=_params(2),
        name="token_mixers",
    )(log_gamma, main4, main4, rv4, main4, main4, main4, lq1, lk1, lq2, lk2, *side_weights)
    return outs[0], outs[1], outs[2:]


def _tail_kernel(a_ref, b_ref, za_ref, zb_ref, g1_ref, g2_ref, x_ref, w1_hbm, w2_hbm, wo_hbm,
                 subln_ref, fw_ref, o_ref, w_vmem, w_sems):
    n_chunks, _, kw = a_ref.shape
    slabs = kw // LANES
    w_hbm = (w1_hbm, w2_hbm, wo_hbm)

    def chunk_copy(w, k):
        rows = slice(k * kw, (k + 1) * kw)
        return pltpu.make_async_copy(w_hbm[w].at[rows, :], w_vmem.at[w, rows, :], w_sems.at[w, k])

    def body(first_step):
        def up(x_slabs, z_slabs, w, eps, gain):
            acc = None
            for k in range(n_chunks):
                o = x_slabs[k].astype(F32)
                ms = jnp.mean(o * o, axis=-1, keepdims=True)
                o = o * lax.rsqrt(ms + eps)
                if gain is not None:
                    o = o * gain
                silu_z = jnp.concatenate([z_slabs[slabs * k + s] for s in range(slabs)],
                                         axis=-1).astype(F32)
                if first_step:
                    chunk_copy(w, k).wait()
                part = jnp.dot((o * silu_z).astype(BF16), w_vmem[w, k * kw:(k + 1) * kw, :],
                               preferred_element_type=F32)
                acc = part if acc is None else acc + part
            return acc

        up1 = up(a_ref, za_ref, 0, NORM_EPS, None)
        up2 = up(b_ref, zb_ref, 1, SUBLN_EPS, subln_ref[...] * (1.0 - LAMBDA_INIT))
        mixed = jnp.concatenate(
            [(g1_ref[j].astype(F32) * up1[:, j * LANES:(j + 1) * LANES]
              + g2_ref[j].astype(F32) * up2[:, j * LANES:(j + 1) * LANES]).astype(BF16)
             for j in range(g1_ref.shape[0])], axis=-1)
        if first_step:
            for k in range(n_chunks):
                chunk_copy(2, k).wait()
        h = x_ref[...] + jnp.dot(mixed, w_vmem[2], preferred_element_type=F32)
        ms = jnp.mean(h * h, axis=-1, keepdims=True)
        o_ref[...] = h * lax.rsqrt(ms + NORM_EPS) * fw_ref[...]

    first = pl.program_id(0) == 0

    @pl.when(first)
    def _():
        for w in range(len(w_hbm)):
            for k in range(n_chunks):
                chunk_copy(w, k).start()
        body(True)

    @pl.when(jnp.logical_not(first))
    def _():
        body(False)


def _merge_and_output(o_ret, o_diff, gates, x2, w_ret_up_bf, w_diff_up_bf, w_out_bf, subln_w,
                      final_w, tm=256):
    nk, m, kw = o_ret.shape
    d = x2.shape[1]
    gs = d // LANES
    x_spec = pl.BlockSpec((nk, tm, kw), lambda i: (0, i, 0))
    w_spec = pl.BlockSpec(memory_space=pl.ANY)
    weights = (w_ret_up_bf, w_diff_up_bf, w_out_bf)
    assert all(w.shape == (nk * kw, d) == (d, d) for w in weights)
    g_spec = lambda name: pl.BlockSpec(
        (gs, tm, LANES), lambda i, base=_slab_base(name): (base // gs, i, 0))
    row_spec = pl.BlockSpec((tm, d), lambda i: (i, 0))
    return pl.pallas_call(
        _tail_kernel,
        out_shape=jax.ShapeDtypeStruct((m, d), F32),
        grid=(m // tm,),
        in_specs=[x_spec, x_spec, g_spec("rz"), g_spec("dz"), g_spec("gr"), g_spec("gd"), row_spec,
                  w_spec, w_spec, w_spec, pl.BlockSpec((1, kw), lambda i: (0, 0)),
                  pl.BlockSpec((1, d), lambda i: (0, 0))],
        out_specs=row_spec,
        scratch_shapes=[pltpu.VMEM((len(weights), d, d), BF16),
                        pltpu.SemaphoreType.DMA((len(weights), nk))],
        compiler_params=_params(1),
        name="merge_and_output",
    )(o_ret, o_diff, gates, gates, gates, gates, x2, *weights, subln_w, final_w)


def _rotary_tables(s, inv_freq):
    pos = np.arange(s, dtype=np.float32)
    ang = pos[:, None] * inv_freq[None, :].astype(np.float32)
    emb = np.concatenate([ang, ang], axis=-1)
    half = emb.shape[-1] // 2
    sign = np.concatenate([-np.ones((half,), np.float32), np.ones((half,), np.float32)])
    return np.cos(emb), np.sin(emb) * sign[None, :]


def kernel(x, norm_w, w_in, w_ret_up, w_diff_up, w_out, lambda_q1, lambda_k1, lambda_q2,
           lambda_k2, subln_w, final_norm_w):
    b, s, d = x.shape
    assert d == D_MODEL and w_in.shape == (1, D_MODEL, IN_WIDTH)
    x2 = x.reshape(b * s, d)

    f32 = np.float32
    ret_inv_freq = np.exp(f32(-math.log(10000.0)) * np.linspace(0.0, 1.0, RET_QK_DIM // 2, dtype=f32))
    rope_inv_freq = f32(ROPE_THETA) ** (-np.arange(0, DIFF_HEAD_DIM, 2, dtype=f32) / f32(DIFF_HEAD_DIM))
    ret_cos, ret_sin = _rotary_tables(s, ret_inv_freq)
    rope_cos, rope_sin = _rotary_tables(s, rope_inv_freq)
    log_gamma = jnp.asarray(np.log1p(-np.exp2(f32(-5.0) - np.arange(RET_HEADS, dtype=f32))))

    ret_k_scale = f32(RET_QK_DIM ** -0.5)
    diff_q_scale = f32((DIFF_HEAD_DIM ** -0.5) * math.log2(math.e))
    cos_tables = jnp.asarray(
        np.stack([ret_cos, ret_cos * ret_k_scale, rope_cos * diff_q_scale, rope_cos]))
    sin_tables = jnp.asarray(
        np.stack([ret_sin, ret_sin * ret_k_scale, rope_sin * diff_q_scale, rope_sin]))

    ret_v, xn = _norm_projection(x2, norm_w[0][None, :], w_in[0], RET_V_DIM)
    proj_main = _input_projection(xn, w_in[0], cos_tables, sin_tables)
    by_pos = lambda a: a.reshape(a.shape[0], b, s, a.shape[-1])

    o_ret, o_diff, (w_ret_up_bf, w_diff_up_bf, w_out_bf) = _token_mixers(
        by_pos(proj_main), by_pos(ret_v), log_gamma, lambda_q1, lambda_k1, lambda_q2,
        lambda_k2, (w_ret_up[0], w_diff_up[0], w_out[0]))
    o_ret = o_ret.reshape(RET_HEADS, b * s, RET_V_DIM)
    o_diff = o_diff.reshape(DIFF_HEADS, b * s, DIFF_V_DIM)

    out = _merge_and_output(o_ret, o_diff, proj_main, x2, w_ret_up_bf, w_diff_up_bf, w_out_bf,
                            subln_w, final_norm_w[None, :])
    return out.reshape(b, s, d)
```

```python
import functools
import math

import jax
import jax.numpy as jnp
import numpy as np
from jax import lax
from jax.experimental import pallas as pl
from jax.experimental.pallas import tpu as pltpu

F32 = jnp.float32
BF16 = jnp.bfloat16

D_MODEL = 2048
RET_QK_DIM = 128
RET_V_DIM = 256
RET_HEADS = D_MODEL // RET_V_DIM
DIFF_HEAD_DIM = 128
DIFF_V_DIM = 2 * DIFF_HEAD_DIM
DIFF_HEADS = D_MODEL // DIFF_V_DIM
ROPE_THETA = 10000.0
NORM_EPS = 1e-6
SUBLN_EPS = 1e-5
LAMBDA_INIT = 0.8 - 0.6 * math.exp(-0.3 * 0)

RET_QK_WIDTH = RET_HEADS * RET_QK_DIM
RET_WIDTH = RET_HEADS * RET_V_DIM
DIFF_QK_WIDTH = DIFF_HEADS * 2 * DIFF_HEAD_DIM
DIFF_WIDTH = DIFF_HEADS * DIFF_V_DIM
IN_WIDTH = 2 * RET_QK_WIDTH + 2 * RET_WIDTH + 2 * DIFF_QK_WIDTH + 2 * DIFF_WIDTH + 2 * D_MODEL

LANES = 128

_SECTIONS = (("rq", RET_QK_WIDTH, "rotary", 0), ("rk", RET_QK_WIDTH, "rotary", 1),
             ("rv", RET_WIDTH, "plain", 0), ("rz", RET_WIDTH, "silu", 0),
             ("dq", DIFF_QK_WIDTH, "rotary", 2), ("dk", DIFF_QK_WIDTH, "rotary", 3),
             ("dv", DIFF_WIDTH, "plain", 0), ("dz", DIFF_WIDTH, "silu", 0),
             ("gr", D_MODEL, "sigmoid", 0), ("gd", D_MODEL, "sigmoid", 0))
assert sum(sec[1] for sec in _SECTIONS) == IN_WIDTH
_NORM_GROUP = ("rv",)
_MAIN_GROUP = ("rq", "rk", "dq", "dk", "dv", "rz", "dz", "gr", "gd")
assert sorted(_NORM_GROUP + _MAIN_GROUP) == sorted(sec[0] for sec in _SECTIONS)


def _group_sections(group):
    return [next(sec for sec in _SECTIONS if sec[0] == name) for name in group]


def _slab_base(name):
    for group in (_NORM_GROUP, _MAIN_GROUP):
        base = 0
        for sec_name, width, _, _ in _group_sections(group):
            if sec_name == name:
                return base
            base += width // LANES
    raise KeyError(name)


VMEM_LIMIT = 56 * 1024 * 1024
NORM_PROJ_VMEM_LIMIT = 58 * 1024 * 1024

RET_CHUNK = 256
ATT_BLOCK = 256
ATT_SKEW = 1
PROJ_ROW_BANDS = ((1, 4), (1, 4), (1, 4), (1, 8), (1, 16), (1, 16))
NORM_PROJ_ROW_BANDS = ((1, 4), (1, 4), (1, 4), (1, 8), (1, 8))
MIX_PATTERN = (1, 1)


def _params(n_axes, vmem=VMEM_LIMIT):
    return pltpu.CompilerParams(dimension_semantics=("arbitrary",) * n_axes,
                                vmem_limit_bytes=vmem)


def _rotary(x, cos, sin_signed):
    return x * cos + pltpu.roll(x, x.shape[-1] // 2, axis=x.ndim - 1) * sin_signed


def _sigmoid(g):
    return 0.5 * jnp.tanh(0.5 * g) + 0.5


def _silu(z):
    h = 0.5 * z
    return h * jnp.tanh(h) + h


def _row_bands(tm, fractions):
    bounds = [0]
    for num, den in fractions:
        bounds.append(bounds[-1] + tm * num // den)
    assert bounds[-1] == tm
    return [slice(r0, r1) for r0, r1 in zip(bounds[:-1], bounds[1:])]


def _select(j, values):
    return sum(jnp.where(j == t, v, 0) for t, v in enumerate(values) if v)


def _projection_kernel(x_ref, w_ref, cos_ref, sin_ref, o_ref, *, tile_kinds):
    epilogues = {
        "rotary": lambda a, rows: _rotary(a, cos_ref[rows, :], sin_ref[rows, :]),
        "plain": lambda a, rows: a,
        "silu": lambda a, rows: _silu(a),
        "sigmoid": lambda a, rows: _sigmoid(a),
    }
    j = pl.program_id(0)
    for kind in dict.fromkeys(tile_kinds):
        tiles = [t for t, k in enumerate(tile_kinds) if k == kind]

        @pl.when(functools.reduce(jnp.logical_or, [j == t for t in tiles]))
        def _(fn=epilogues[kind]):
            w = w_ref[...].astype(BF16)
            for rows in _row_bands(x_ref.shape[0], PROJ_ROW_BANDS):
                acc = jnp.dot(x_ref[rows, :], w, preferred_element_type=F32)
                for s in range(o_ref.shape[0]):
                    o_ref[s, rows, :] = fn(acc[:, s * LANES:(s + 1) * LANES],
                                           rows).astype(o_ref.dtype)


def _section_tiles(group, tn):
    col_of = {}
    col = 0
    for name, width, _, _ in _SECTIONS:
        col_of[name] = col
        col += width
    col_tiles, tables, kinds = [], [], []
    for name, width, kind, table in _group_sections(group):
        assert width % tn == 0 and col_of[name] % tn == 0
        n = width // tn
        col_tiles += [col_of[name] // tn + t for t in range(n)]
        tables += [table] * n
        kinds += [kind] * n
    return col_tiles, tables, kinds


def _norm_projection_kernel(x_ref, nw_ref, w_ref, o_ref, xn_ref):
    w = w_ref[...].astype(BF16)
    for rows in _row_bands(x_ref.shape[0], NORM_PROJ_ROW_BANDS):
        x = x_ref[rows, :]
        ms = jnp.mean(x * x, axis=-1, keepdims=True)
        xn = (x * lax.rsqrt(ms + NORM_EPS) * nw_ref[...]).astype(BF16)
        xn_ref[rows, :] = xn
        acc = jnp.dot(xn, w, preferred_element_type=F32)
        width = o_ref.shape[2]
        for s in range(o_ref.shape[0]):
            o_ref[s, rows, :] = acc[:, s * width:(s + 1) * width].astype(o_ref.dtype)


def _norm_projection(x2, norm_w, w_in, head_width, tm=1024):
    m, d = x2.shape
    (_, tn, kind, _), = _group_sections(_NORM_GROUP)
    assert kind == "plain" and tn % head_width == 0
    (col_tile,), _, _ = _section_tiles(_NORM_GROUP, tn)
    return pl.pallas_call(
        _norm_projection_kernel,
        out_shape=[jax.ShapeDtypeStruct((tn // head_width, m, head_width), BF16),
                   jax.ShapeDtypeStruct((m, d), BF16)],
        grid=(m // tm,),
        in_specs=[pl.BlockSpec((tm, d), lambda i: (i, 0)),
                  pl.BlockSpec((1, d), lambda i: (0, 0)),
                  pl.BlockSpec((d, tn), lambda i: (0, col_tile), pipeline_mode=pl.Buffered(1))],
        out_specs=[pl.BlockSpec((tn // head_width, tm, head_width), lambda i: (0, i, 0)),
                   pl.BlockSpec((tm, d), lambda i: (i, 0))],
        compiler_params=_params(1, vmem=NORM_PROJ_VMEM_LIMIT),
        name="norm_projection",
    )(x2, norm_w, w_in)


def _input_projection(xn, w_in, cos_tables, sin_tables, tm=2048, tn=1024):
    m, d = xn.shape
    s_len = cos_tables.shape[1]
    col_tiles, tables, kinds = _section_tiles(_MAIN_GROUP, tn)
    n_rotary = kinds.count("rotary")
    assert kinds[:n_rotary] == ["rotary"] * n_rotary

    def table_map(j, i):
        rotary = j < n_rotary
        return (_select(jnp.minimum(j, n_rotary - 1), tables),
                jnp.where(rotary, i % (s_len // tm), s_len // tm - 1), 0)

    table_spec = pl.BlockSpec((None, tm, LANES), table_map)
    return pl.pallas_call(
        functools.partial(_projection_kernel, tile_kinds=kinds),
        out_shape=jax.ShapeDtypeStruct((len(col_tiles) * tn // LANES, m, LANES), BF16),
        grid=(len(col_tiles), m // tm),
        in_specs=[pl.BlockSpec((tm, d), lambda j, i: (i, 0)),
                  pl.BlockSpec((d, tn), lambda j, i: (0, _select(j, col_tiles))),
                  table_spec, table_spec],
        out_specs=pl.BlockSpec((tn // LANES, tm, LANES), lambda j, i: (j, i, 0)),
        compiler_params=_params(2),
        name="epilogue_projection",
    )(xn, w_in, cos_tables, sin_tables)


def _retention_pieces(lg, q_ref, k_ref, v_ref, o_ref, qx_ref, kz_ref):
    s_len = q_ref.shape[0]
    c = RET_CHUNK
    n_chunks = s_len // c

    rowq = lax.broadcasted_iota(jnp.int32, (c, RET_QK_DIM), 0).astype(F32)
    zeta = jnp.exp(lg * (c - 1.0 - rowq))
    xi = jnp.exp(lg * (rowq + 1.0))
    for n in range(n_chunks):
        sl = slice(n * c, (n + 1) * c)
        qx_ref[sl, :] = (q_ref[sl, :].astype(F32) * xi).astype(BF16)
        kz_ref[sl, :] = (k_ref[sl, :].astype(F32) * zeta).astype(BF16)

    row = lax.broadcasted_iota(jnp.int32, (c, c), 0).astype(F32)
    col = lax.broadcasted_iota(jnp.int32, (c, c), 1).astype(F32)
    rel = row - col
    decay_mask = jnp.where(rel >= 0.0, jnp.exp(lg * jnp.maximum(rel, 0.0)), 0.0)
    chunk_decay = jnp.exp(jnp.zeros((1, RET_V_DIM), F32) + lg * c)
    yield

    state = jnp.zeros((RET_QK_DIM, RET_V_DIM), F32)
    for n in range(n_chunks):
        sl = slice(n * c, (n + 1) * c)
        vc = v_ref[sl, :]
        scores = lax.dot_general(q_ref[sl, :], k_ref[sl, :], (((1,), (1,)), ((), ())),
                                 preferred_element_type=F32) * decay_mask
        o = jnp.dot(scores.astype(BF16), vc, preferred_element_type=F32)
        o = o + jnp.dot(qx_ref[sl, :], state.astype(BF16), preferred_element_type=F32)
        if n + 1 < n_chunks:
            kv = lax.dot_general(kz_ref[sl, :], vc, (((0,), (0,)), ((), ())),
                                 preferred_element_type=F32)
            state = state * chunk_decay + kv

        o_ref[sl, :] = o.astype(o_ref.dtype)
        yield


def _diff_attn_pieces(q_ref, k_ref, v_ref, lq1_ref, lk1_ref, lq2_ref, lk2_ref, o_ref, vc_ref):
    t = ATT_BLOCK
    s_len = vc_ref.shape[0]
    vc_ref[...] = jnp.concatenate([v_ref[0], v_ref[1]], axis=-1)

    lam = (jnp.exp(jnp.sum(lq1_ref[...] * lk1_ref[...], axis=-1, keepdims=True))
           - jnp.exp(jnp.sum(lq2_ref[...] * lk2_ref[...], axis=-1, keepdims=True))
           + LAMBDA_INIT)
    causal = (lax.broadcasted_iota(jnp.int32, (t, t), 1)
              <= lax.broadcasted_iota(jnp.int32, (t, t), 0))
    nt_dims = (((1,), (1,)), ((), ()))

    units = [(n, c) for n in reversed(range(s_len // t)) for c in range(2)]

    def scores(u):
        n, c = units[u]
        r0 = n * t
        q = q_ref[c, r0:r0 + t, :]
        s_diag = jnp.where(causal,
                           lax.dot_general(q, k_ref[c, r0:r0 + t, :], nt_dims,
                                           preferred_element_type=F32), -jnp.inf)
        m = jnp.max(s_diag, axis=-1, keepdims=True)
        s_off = None
        if n > 0:
            s_off = lax.dot_general(q, k_ref[c, 0:r0, :], nt_dims, preferred_element_type=F32)
            m = jnp.maximum(m, jnp.max(s_off, axis=-1, keepdims=True))
        return s_diag, s_off, m

    def probs(sc):
        s_diag, s_off, m = sc
        p_diag = jnp.exp2(s_diag - m)
        l = jnp.sum(p_diag, axis=-1, keepdims=True)
        p_off = None
        if s_off is not None:
            p_off = jnp.exp2(s_off - m)
            l = l + jnp.sum(p_off, axis=-1, keepdims=True)
            p_off = p_off.astype(BF16)
        return p_diag.astype(BF16), p_off, l

    def weighted(u, pr):
        n, _ = units[u]
        r0 = n * t
        p_diag, p_off, l = pr
        acc = jnp.dot(p_diag, vc_ref[r0:r0 + t, :], preferred_element_type=F32)
        if p_off is not None:
            acc = acc + jnp.dot(p_off, vc_ref[0:r0, :], preferred_element_type=F32)
        return acc, l

    def finish(n, res):
        r0 = n * t
        o = res[0][0] * (1.0 / res[0][1]) - res[1][0] * (lam / res[1][1])
        o_ref[r0:r0 + t, :] = o.astype(o_ref.dtype)

    skew = ATT_SKEW
    sc, pr, res = {}, {}, {}
    for step in range(len(units) + 2 * skew):
        u_sc, u_pr, u_pv = step, step - skew, step - 2 * skew
        if 0 <= u_sc < len(units):
            sc[u_sc] = scores(u_sc)
        if 0 <= u_pr < len(units):
            pr[u_pr] = probs(sc.pop(u_pr))
        if 0 <= u_pv < len(units):
            res[u_pv] = weighted(u_pv, pr.pop(u_pv))
            if units[u_pv][1] == 1:
                finish(units[u_pv][0], (res.pop(u_pv - 1), res.pop(u_pv)))
        yield


def _mixers_kernel(lg_ref, rq_ref, rk_ref, rv_ref, dq_ref, dk_ref, dv_ref,
                   lq1_ref, lk1_ref, lq2_ref, lk2_ref, s1_ref, s2_ref, s3_ref,
                   oret_ref, odiff_ref, s1bf_ref, s2bf_ref, s3bf_ref,
                   qx_ref, kz_ref, dvc_ref):
    for src, dst in ((s1_ref, s1bf_ref), (s2_ref, s2bf_ref), (s3_ref, s3bf_ref)):
        dst[...] = src[...].astype(BF16)

    retention = _retention_pieces(lg_ref[pl.program_id(1)], rq_ref, rk_ref, rv_ref,
                                  oret_ref, qx_ref, kz_ref)
    attention = _diff_attn_pieces(dq_ref, dk_ref, dv_ref, lq1_ref, lk1_ref, lq2_ref, lk2_ref,
                                  odiff_ref, dvc_ref)
    live = [attention, retention]
    while live:
        for gen, count in zip((attention, retention), MIX_PATTERN):
            for _ in range(count):
                if gen in live and next(gen, "done") == "done":
                    live.remove(gen)


def _token_mixers(main4, rv4, log_gamma, lq1, lk1, lq2, lk2, side_weights):
    _, b, s, _ = main4.shape
    assert RET_HEADS == DIFF_HEADS and _NORM_GROUP == ("rv",)
    one = lambda name: pl.BlockSpec(
        (None, None, s, LANES), lambda bi, h, base=_slab_base(name): (base + h, bi, 0, 0))
    pair = lambda name: pl.BlockSpec(
        (2, None, s, LANES), lambda bi, h, base=_slab_base(name): (base // 2 + h, bi, 0, 0))
    vec = lambda n: pl.BlockSpec((1, n), lambda bi, h: (0, 0))
    side_rows, side_cols = side_weights[0].shape
    assert all(w.shape == (side_rows, side_cols) for w in side_weights)
    band = side_rows // (b * DIFF_HEADS)
    side_spec = pl.BlockSpec((band, side_cols), lambda bi, h: (bi * DIFF_HEADS + h, 0))
    head_out = pl.BlockSpec((None, None, s, RET_V_DIM), lambda bi, h: (h, bi, 0, 0))
    outs = pl.pallas_call(
        _mixers_kernel,
        out_shape=[jax.ShapeDtypeStruct((RET_HEADS, b, s, RET_V_DIM), BF16),
                   jax.ShapeDtypeStruct((DIFF_HEADS, b, s, DIFF_V_DIM), BF16)]
                  + [jax.ShapeDtypeStruct((side_rows, side_cols), BF16)] * len(side_weights),
        grid=(b, DIFF_HEADS),
        in_specs=[pl.BlockSpec(memory_space=pltpu.SMEM),
                  one("rq"), one("rk"), head_out, pair("dq"), pair("dk"), pair("dv"),
                  vec(DIFF_HEAD_DIM), vec(DIFF_HEAD_DIM), vec(DIFF_HEAD_DIM), vec(DIFF_HEAD_DIM)]
                 + [side_spec] * len(side_weights),
        out_specs=[head_out, head_out] + [side_spec] * len(side_weights),
        scratch_shapes=[pltpu.VMEM((s, RET_QK_DIM), BF16)] * 2
                       + [pltpu.VMEM((s, DIFF_V_DIM), BF16)],
        compiler_params=_params(2),
        name="token_mixers",
    )(log_gamma, main4, main4, rv4, main4, main4, main4, lq1, lk1, lq2, lk2, *side_weights)
    return outs[0], outs[1], outs[2:]


def _tail_kernel(a_ref, b_ref, za_ref, zb_ref, g1_ref, g2_ref, x_ref, w1_hbm, w2_hbm, wo_hbm,
                 subln_ref, fw_ref, o_ref, w_vmem, w_sems):
    n_chunks, _, kw = a_ref.shape
    slabs = kw // LANES

    @pl.when(pl.program_id(0) == 0)
    def _():
        copies = [pltpu.make_async_copy(src, w_vmem.at[w], w_sems.at[w])
                  for w, src in enumerate((w1_hbm, w2_hbm, wo_hbm))]
        for copy in copies:
            copy.start()
        for copy in copies:
            copy.wait()

    def up(x_slabs, z_slabs, w, eps, gain):
        acc = None
        for k in range(n_chunks):
            o = x_slabs[k].astype(F32)
            ms = jnp.mean(o * o, axis=-1, keepdims=True)
            o = o * lax.rsqrt(ms + eps)
            if gain is not None:
                o = o * gain
            silu_z = jnp.concatenate([z_slabs[slabs * k + s] for s in range(slabs)],
                                     axis=-1).astype(F32)
            part = jnp.dot((o * silu_z).astype(BF16), w_vmem[w, k * kw:(k + 1) * kw, :],
                           preferred_element_type=F32)
            acc = part if acc is None else acc + part
        return acc

    up1 = up(a_ref, za_ref, 0, NORM_EPS, None)
    up2 = up(b_ref, zb_ref, 1, SUBLN_EPS, subln_ref[...] * (1.0 - LAMBDA_INIT))
    mixed = jnp.concatenate(
        [(g1_ref[j].astype(F32) * up1[:, j * LANES:(j + 1) * LANES]
          + g2_ref[j].astype(F32) * up2[:, j * LANES:(j + 1) * LANES]).astype(BF16)
         for j in range(g1_ref.shape[0])], axis=-1)
    h = x_ref[...] + jnp.dot(mixed, w_vmem[2], preferred_element_type=F32)
    ms = jnp.mean(h * h, axis=-1, keepdims=True)
    o_ref[...] = h * lax.rsqrt(ms + NORM_EPS) * fw_ref[...]


def _merge_and_output(o_ret, o_diff, gates, x2, w_ret_up_bf, w_diff_up_bf, w_out_bf, subln_w,
                      final_w, tm=256):
    nk, m, kw = o_ret.shape
    d = x2.shape[1]
    gs = d // LANES
    x_spec = pl.BlockSpec((nk, tm, kw), lambda i: (0, i, 0))
    w_spec = pl.BlockSpec(memory_space=pl.ANY)
    weights = (w_ret_up_bf, w_diff_up_bf, w_out_bf)
    assert all(w.shape == (nk * kw, d) == (d, d) for w in weights)
    g_spec = lambda name: pl.BlockSpec(
        (gs, tm, LANES), lambda i, base=_slab_base(name): (base // gs, i, 0))
    row_spec = pl.BlockSpec((tm, d), lambda i: (i, 0))
    return pl.pallas_call(
        _tail_kernel,
        out_shape=jax.ShapeDtypeStruct((m, d), F32),
        grid=(m // tm,),
        in_specs=[x_spec, x_spec, g_spec("rz"), g_spec("dz"), g_spec("gr"), g_spec("gd"), row_spec,
                  w_spec, w_spec, w_spec, pl.BlockSpec((1, kw), lambda i: (0, 0)),
                  pl.BlockSpec((1, d), lambda i: (0, 0))],
        out_specs=row_spec,
        scratch_shapes=[pltpu.VMEM((len(weights), d, d), BF16),
                        pltpu.SemaphoreType.DMA((len(weights),))],
        compiler_params=_params(1),
        name="merge_and_output",
    )(o_ret, o_diff, gates, gates, gates, gates, x2, *weights, subln_w, final_w)


def _rotary_tables(s, inv_freq):
    pos = np.arange(s, dtype=np.float32)
    ang = pos[:, None] * inv_freq[None, :].astype(np.float32)
    emb = np.concatenate([ang, ang], axis=-1)
    half = emb.shape[-1] // 2
    sign = np.concatenate([-np.ones((half,), np.float32), np.ones((half,), np.float32)])
    return np.cos(emb), np.sin(emb) * sign[None, :]


def kernel(x, norm_w, w_in, w_ret_up, w_diff_up, w_out, lambda_q1, lambda_k1, lambda_q2,
           lambda_k2, subln_w, final_norm_w):
    b, s, d = x.shape
    assert d == D_MODEL and w_in.shape == (1, D_MODEL, IN_WIDTH)
    x2 = x.reshape(b * s, d)

    f32 = np.float32
    ret_inv_freq = np.exp(f32(-math.log(10000.0)) * np.linspace(0.0, 1.0, RET_QK_DIM // 2, dtype=f32))
    rope_inv_freq = f32(ROPE_THETA) ** (-np.arange(0, DIFF_HEAD_DIM, 2, dtype=f32) / f32(DIFF_HEAD_DIM))
    ret_cos, ret_sin = _rotary_tables(s, ret_inv_freq)
    rope_cos, rope_sin = _rotary_tables(s, rope_inv_freq)
    log_gamma = jnp.asarray(np.log1p(-np.exp2(f32(-5.0) - np.arange(RET_HEADS, dtype=f32))))

    ret_k_scale = f32(RET_QK_DIM ** -0.5)
    diff_q_scale = f32((DIFF_HEAD_DIM ** -0.5) * math.log2(math.e))
    cos_tables = jnp.asarray(
        np.stack([ret_cos, ret_cos * ret_k_scale, rope_cos * diff_q_scale, rope_cos]))
    sin_tables = jnp.asarray(
        np.stack([ret_sin, ret_sin * ret_k_scale, rope_sin * diff_q_scale, rope_sin]))

    ret_v, xn = _norm_projection(x2, norm_w[0][None, :], w_in[0], RET_V_DIM)
    proj_main = _input_projection(xn, w_in[0], cos_tables, sin_tables)
    by_pos = lambda a: a.reshape(a.shape[0], b, s, a.shape[-1])

    o_ret, o_diff, (w_ret_up_bf, w_diff_up_bf, w_out_bf) = _token_mixers(
        by_pos(proj_main), by_pos(ret_v), log_gamma, lambda_q1, lambda_k1, lambda_q2,
        lambda_k2, (w_ret_up[0], w_diff_up[0], w_out[0]))
    o_ret = o_ret.reshape(RET_HEADS, b * s, RET_V_DIM)
    o_diff = o_diff.reshape(DIFF_HEADS, b * s, DIFF_V_DIM)

    out = _merge_and_output(o_ret, o_diff, proj_main, x2, w_ret_up_bf, w_diff_up_bf, w_out_bf,
                            subln_w, final_norm_w[None, :])
    return out.reshape(b, s, d)
```
